```python
import math
import jax, jax.numpy as jnp
from jax import lax
import numpy as np

D_MODEL = 1024
BATCH = 4
SEQ = 8192
DEPTH = 4

N_EVEN = (DEPTH + 1) // 2
N_ODD = DEPTH // 2
EPS = 1e-6
NEG = -1e30
BIG = 1e30

MLA_HEADS = 8
MLA_Q_RANK = 384
MLA_KV_RANK = 256
MLA_NOPE = 64
MLA_ROPE = 32
MLA_V = 64
ROPE_BASE = 10000.0
MLA_Q_BLOCK = 128

GDN_HEADS = 4
GDN_DK = 128
GDN_DV = 128
GDN_CONV = 4
GDN_CHUNK = 64

NSA_HEADS = 16
NSA_GROUPS = 4
NSA_HPG = NSA_HEADS // NSA_GROUPS
NSA_DIM = 64
CMP_BLOCK = 32
CMP_STRIDE = 16
CMP_HIDDEN = 256
SEL_BLOCK = 64
SEL_TOPN = 16
WINDOW = 512
NSA_Q_BLOCK = 64

FFN_HIDDEN = ((8 * D_MODEL + 3 * 256 - 1) // (3 * 256)) * 256

GDN_QK = GDN_HEADS * GDN_DK
GDN_VW = GDN_HEADS * GDN_DV
EVEN_SPLITS = (MLA_Q_RANK, MLA_KV_RANK, MLA_ROPE, GDN_QK, GDN_QK, GDN_VW, GDN_VW, GDN_HEADS, GDN_HEADS)
EVEN_IN = sum(EVEN_SPLITS)
EVEN_MIX = MLA_HEADS * MLA_V + GDN_VW
NSA_QW = NSA_HEADS * NSA_DIM
NSA_KVW = NSA_GROUPS * NSA_DIM
ODD_SPLITS = (NSA_QW,) + (NSA_KVW,) * 6 + (3 * NSA_HEADS,)
ODD_IN = sum(ODD_SPLITS)
ODD_MIX = NSA_QW

kernel_name = 'hybrid_mla_gdn_nsa_trunk'


def _split(z, sizes):
    cuts = np.cumsum(sizes)[:-1].tolist()
    return jnp.split(z, cuts, axis=-1)


def rmsnorm(x, g):
    xf = x.astype(jnp.float32)
    y = xf * lax.rsqrt(jnp.mean(xf * xf, axis=-1, keepdims=True) + EPS)
    return (y * g.astype(jnp.float32)).astype(x.dtype)


def l2norm(x):
    xf = x.astype(jnp.float32)
    return xf * lax.rsqrt(jnp.sum(xf * xf, axis=-1, keepdims=True) + EPS)


def alibi_slopes(n):
    return 2.0 ** (-8.0 * jnp.arange(1, n + 1, dtype=jnp.float32) / n)


def apply_rope(x, pos):
    half = x.shape[-1] // 2
    inv = ROPE_BASE ** (-jnp.arange(half, dtype=jnp.float32) / half)
    ang = pos.astype(jnp.float32)[:, None] * inv[None, :]
    cos = jnp.cos(ang)[None, :, None, :]
    sin = jnp.sin(ang)[None, :, None, :]
    xf = x.astype(jnp.float32)
    x1, x2 = xf[..., :half], xf[..., half:]
    return jnp.concatenate([x1 * cos - x2 * sin, x2 * cos + x1 * sin], axis=-1).astype(x.dtype)


def causal_attention_blocked(q, k, v, scale):
    B, S, H, DQ = q.shape
    DVh = v.shape[-1]
    nb = S // MLA_Q_BLOCK
    q_blocks = q.reshape(B, nb, MLA_Q_BLOCK, H, DQ).swapaxes(0, 1)
    kpos = jnp.arange(S)
    vf = v.astype(jnp.float32)

    def one(args):
        qi, i = args
        qpos = i * MLA_Q_BLOCK + jnp.arange(MLA_Q_BLOCK)
        s = jnp.einsum('bqhd,bkhd->bhqk', qi, k, preferred_element_type=jnp.float32) * scale
        s = jnp.where(kpos[None, :] <= qpos[:, None], s, NEG)
        p = jax.nn.softmax(s, axis=-1)
        return jnp.einsum('bhqk,bkhd->bqhd', p, vf)

    o = lax.map(one, (q_blocks, jnp.arange(nb)))
    return o.swapaxes(0, 1).reshape(B, S, H, DVh)


def mla_attention(c_q, c_kv, k_r, q_norm, kv_norm, w_uq, w_ukv, pos):
    B, S, _ = c_q.shape
    q = (rmsnorm(c_q, q_norm) @ w_uq).reshape(B, S, MLA_HEADS, MLA_NOPE + MLA_ROPE)
    kv = (rmsnorm(c_kv, kv_norm) @ w_ukv).reshape(B, S, MLA_HEADS, MLA_NOPE + MLA_V)
    q_rot = apply_rope(q[..., MLA_NOPE:], pos)
    k_rot = apply_rope(k_r[:, :, None, :], pos)
    q = jnp.concatenate([q[..., :MLA_NOPE], q_rot], axis=-1)
    k = jnp.concatenate([kv[..., :MLA_NOPE], jnp.broadcast_to(k_rot, (B, S, MLA_HEADS, MLA_ROPE))], axis=-1)
    v = kv[..., MLA_NOPE:]
    o = causal_attention_blocked(q, k, v, (MLA_NOPE + MLA_ROPE) ** -0.5)
    return o.reshape(B, S, MLA_HEADS * MLA_V).astype(c_q.dtype)


def causal_depthwise_conv(x, w):
    K, C = w.shape
    return lax.conv_general_dilated(x, w[:, None, :].astype(x.dtype), window_strides=(1,),
                                    padding=[(K - 1, 0)], dimension_numbers=('NWC', 'WIO', 'NWC'),
                                    feature_group_count=C)


def chunk_gated_delta_rule(q, k, v, beta, g):
    B, S, H, DK = q.shape
    DV = v.shape[-1]
    C = GDN_CHUNK
    N = S // C
    f32 = jnp.float32

    def chunks(t):
        t = t.astype(f32).reshape((B, N, C, H) + t.shape[3:])
        return jnp.moveaxis(t, 3, 2).swapaxes(0, 1)

    qc, kc, vc, bc, gc = [chunks(t) for t in (q, k, v, beta, g)]
    gc = jnp.cumsum(gc, axis=-1)
    lower = jnp.tril(jnp.ones((C, C), dtype=bool))
    strict = jnp.tril(jnp.ones((C, C), dtype=bool), -1)
    decay = jnp.exp(jnp.where(lower, gc[..., :, None] - gc[..., None, :], -jnp.inf))
    kb = kc * bc[..., None]
    a = jnp.where(strict, jnp.einsum('nbhid,nbhjd->nbhij', kb, kc) * decay, 0.0)
    eye = jnp.eye(C, dtype=f32)
    t_inv = lax.linalg.triangular_solve(a + eye, jnp.broadcast_to(eye, a.shape), left_side=True, lower=True)
    u = jnp.einsum('nbhij,nbhjd->nbhid', t_inv, vc * bc[..., None])
    w = jnp.einsum('nbhij,nbhjd->nbhid', t_inv, kb * jnp.exp(gc)[..., None])
    intra = jnp.einsum('nbhid,nbhjd->nbhij', qc, kc) * decay
    q_dec = qc * jnp.exp(gc)[..., None]
    k_dec = kc * jnp.exp(gc[..., -1:] - gc)[..., None]
    g_last = jnp.exp(gc[..., -1])

    def step(state, xs):
        u_i, w_i, qd_i, kd_i, a_i, gl_i = xs
        v_new = u_i - jnp.einsum('bhcd,bhde->bhce', w_i, state)
        o_i = jnp.einsum('bhcd,bhde->bhce', qd_i, state) + jnp.einsum('bhij,bhje->bhie', a_i, v_new)
        state = state * gl_i[..., None, None] + jnp.einsum('bhcd,bhce->bhde', kd_i, v_new)
        return state, o_i

    state0 = jnp.zeros((B, H, DK, DV), f32)
    _, o = lax.scan(step, state0, (u, w, q_dec, k_dec, intra, g_last))
    return o.swapaxes(0, 1).swapaxes(2, 3).reshape(B, S, H, DV)


def gated_deltanet(z_q, z_k, z_v, z_g, z_b, z_a, conv_w, a_log, dt_bias, out_norm):
    B, S, _ = z_q.shape
    f32 = jnp.float32
    qkv = jax.nn.silu(causal_depthwise_conv(jnp.concatenate([z_q, z_k, z_v], axis=-1), conv_w))
    q, k, v = _split(qkv, (GDN_QK, GDN_QK, GDN_VW))
    q = l2norm(q.reshape(B, S, GDN_HEADS, GDN_DK)) * (GDN_DK ** -0.5)
    k = l2norm(k.reshape(B, S, GDN_HEADS, GDN_DK))
    v = v.reshape(B, S, GDN_HEADS, GDN_DV)
    beta = jax.nn.sigmoid(z_b.astype(f32))
    g = -jnp.exp(a_log.astype(f32)) * jax.nn.softplus(z_a.astype(f32) + dt_bias.astype(f32))
    o = chunk_gated_delta_rule(q, k, v, beta, g)
    o = rmsnorm(o, out_norm) * jax.nn.silu(z_g.astype(f32).reshape(B, S, GDN_HEADS, GDN_DV))
    return o.reshape(B, S, GDN_VW).astype(z_q.dtype)


def compress_blocks(x, starts, pe, w1, w2):
    B, _, G, D = x.shape
    n = starts.shape[0]
    idx = starts[:, None] + jnp.arange(CMP_BLOCK)[None, :]
    blk = x[:, idx] + pe[None, None, :, None, :].astype(x.dtype)
    flat = blk.transpose(0, 1, 3, 2, 4).reshape(B, n, G, CMP_BLOCK * D)
    return jax.nn.gelu(flat @ w1) @ w2


def nsa_attention(z_q, z_kc, z_vc, z_ks, z_vs, z_kw, z_vw, z_g, pe_k, w1_k, w2_k, pe_v, w1_v, w2_v):
    B, S, _ = z_q.shape
    G, R, D = NSA_GROUPS, NSA_HPG, NSA_DIM
    f32 = jnp.float32
    Tq = NSA_Q_BLOCK
    q = z_q.reshape(B, S, G, R, D)
    kc, vc, ks, vs, kw, vw = [t.reshape(B, S, G, D) for t in (z_kc, z_vc, z_ks, z_vs, z_kw, z_vw)]
    gates = jax.nn.sigmoid(z_g.astype(f32)).reshape(B, S, G, R, 3)
    scale = D ** -0.5
    slopes = alibi_slopes(NSA_HEADS).reshape(G, R)

    n_cmp = (S - CMP_BLOCK) // CMP_STRIDE + 1
    cmp_start = jnp.arange(n_cmp) * CMP_STRIDE
    cmp_end = cmp_start + CMP_BLOCK - 1
    cmp_center = cmp_start.astype(f32) + 0.5 * (CMP_BLOCK - 1)
    k_cmp = compress_blocks(kc, cmp_start, pe_k, w1_k, w2_k)
    v_cmp = compress_blocks(vc, cmp_start, pe_v, w1_v, w2_v).astype(f32)

    n_sel = S // SEL_BLOCK
    top_n = min(SEL_TOPN, n_sel)
    sel_start = jnp.arange(n_sel) * SEL_BLOCK
    overlap = ((cmp_start[:, None] <= sel_start[None, :] + SEL_BLOCK - 1)
               & (cmp_end[:, None] >= sel_start[None, :])).astype(f32)
    ks_blk = ks.reshape(B, n_sel, SEL_BLOCK, G, D).transpose(0, 3, 1, 2, 4)
    vs_blk = vs.reshape(B, n_sel, SEL_BLOCK, G, D).transpose(0, 3, 1, 2, 4)
    gather = jax.vmap(jax.vmap(lambda blocks, idx: blocks[idx]))

    kw_pad = jnp.pad(kw, ((0, 0), (WINDOW, 0), (0, 0), (0, 0)))
    vw_pad = jnp.pad(vw, ((0, 0), (WINDOW, 0), (0, 0), (0, 0)))
    jsel = jnp.arange(n_sel)

    def one(i):
        t0 = i * Tq
        qi = lax.dynamic_slice_in_dim(q, t0, Tq, axis=1)
        gi = lax.dynamic_slice_in_dim(gates, t0, Tq, axis=1)
        qpos = t0 + jnp.arange(Tq)
        qpos_f = qpos.astype(f32)

        s = jnp.einsum('bqgrd,bngd->bgrqn', qi, k_cmp, preferred_element_type=f32) * scale
        s = s - slopes[:, :, None, None] * (qpos_f[:, None] - cmp_center[None, :])
        valid = cmp_end[None, :] <= qpos[:, None]
        p_cmp = jax.nn.softmax(jnp.where(valid, s, NEG), axis=-1) * jnp.any(valid, axis=-1)[:, None].astype(f32)
        o_cmp = jnp.einsum('bgrqn,bngd->bqgrd', p_cmp, v_cmp)

        imp = jnp.einsum('bgrqn,nj->bgqj', p_cmp, overlap)
        q_blk = qpos // SEL_BLOCK
        forced = (jsel[None, :] == 0) | (jsel[None, :] == q_blk[:, None]) | (jsel[None, :] == q_blk[:, None] - 1)
        causal_blk = jsel[None, :] <= q_blk[:, None]
        imp = jnp.where(forced, BIG, jnp.where(causal_blk, imp, NEG))
        _, sel = lax.top_k(imp, top_n)
        k_g = gather(ks_blk, sel)
        v_g = gather(vs_blk, sel).astype(f32)
        s_pos = sel[..., None] * SEL_BLOCK + jnp.arange(SEL_BLOCK)
        dist = (qpos[None, None, :, None, None] - s_pos).astype(f32)[:, :, None]
        s = jnp.einsum('bqgrd,bgqnsd->bgrqns', qi, k_g, preferred_element_type=f32) * scale
        s = jnp.where(dist >= 0, s - slopes[None, :, :, None, None, None] * dist, NEG)
        p = jax.nn.softmax(s.reshape(B, G, R, Tq, top_n * SEL_BLOCK), axis=-1).reshape(s.shape)
        o_sel = jnp.einsum('bgrqns,bgqnsd->bqgrd', p, v_g)

        k_win = lax.dynamic_slice_in_dim(kw_pad, t0, Tq + WINDOW, axis=1)
        v_win = lax.dynamic_slice_in_dim(vw_pad, t0, Tq + WINDOW, axis=1).astype(f32)
        kpos = t0 - WINDOW + jnp.arange(Tq + WINDOW)
        dw = qpos[:, None] - kpos[None, :]
        wvalid = (dw >= 0) & (dw < WINDOW) & (kpos[None, :] >= 0)
        s = jnp.einsum('bqgrd,bkgd->bgrqk', qi, k_win, preferred_element_type=f32) * scale
        s = s - slopes[:, :, None, None] * dw.astype(f32)
        p = jax.nn.softmax(jnp.where(wvalid, s, NEG), axis=-1)
        o_win = jnp.einsum('bgrqk,bkgd->bqgrd', p, v_win)

        o = gi[..., 0:1] * o_cmp + gi[..., 1:2] * o_sel + gi[..., 2:3] * o_win
        return o.reshape(B, Tq, G * R * D).astype(z_q.dtype)

    out = lax.map(one, jnp.arange(S // Tq))
    return out.swapaxes(0, 1).reshape(B, S, G * R * D)


def even_mixer(xn, w_in, q_norm, kv_norm, w_uq, w_ukv, conv_w, a_log, dt_bias, gdn_norm, w_out, pos):
    z = xn @ w_in
    c_q, c_kv, k_r, z_q, z_k, z_v, z_g, z_b, z_a = _split(z, EVEN_SPLITS)
    o_mla = mla_attention(c_q, c_kv, k_r, q_norm, kv_norm, w_uq, w_ukv, pos)
    o_gdn = gated_deltanet(z_q, z_k, z_v, z_g, z_b, z_a, conv_w, a_log, dt_bias, gdn_norm)
    return jnp.concatenate([o_mla, o_gdn], axis=-1) @ w_out


def odd_mixer(xn, w_in, pe_k, w1_k, w2_k, pe_v, w1_v, w2_v, w_out):
    z = xn @ w_in
    z_q, z_kc, z_vc, z_ks, z_vs, z_kw, z_vw, z_g = _split(z, ODD_SPLITS)
    o = nsa_attention(z_q, z_kc, z_vc, z_ks, z_vs, z_kw, z_vw, z_g, pe_k, w1_k, w2_k, pe_v, w1_v, w2_v)
    return o @ w_out


def swiglu(x, w_gate, w_up, w_down):
    return (jax.nn.silu(x @ w_gate) * (x @ w_up)) @ w_down


def setup_inputs(seed: int = 0) -> dict:
    key = jax.random.key(seed)
    keys = iter(jax.random.split(key, 32))
    f32 = jnp.float32
    NE, NO, D = N_EVEN, N_ODD, D_MODEL

    def dense(shape, fan_in):
        return jax.random.normal(next(keys), shape, f32) * (fan_in ** -0.5)

    def gain(shape):
        return 1.0 + 0.02 * jax.random.normal(next(keys), shape, f32)

    def small(shape):
        return 0.02 * jax.random.normal(next(keys), shape, f32)

    conv_ch = 2 * GDN_QK + GDN_VW
    a_init = jax.random.uniform(next(keys), (NE, GDN_HEADS), f32, 1.0, 16.0)
    dt = jnp.exp(jax.random.uniform(next(keys), (NE, GDN_HEADS), f32, math.log(1e-3), math.log(1e-1)))
    flat_cmp = CMP_BLOCK * NSA_DIM
    return {
        'x': jax.random.normal(next(keys), (BATCH, SEQ, D), f32),
        'ev_attn_norm': gain((NE, D)),
        'ev_w_in': dense((NE, D, EVEN_IN), D),
        'ev_q_norm': gain((NE, MLA_Q_RANK)),
        'ev_kv_norm': gain((NE, MLA_KV_RANK)),
        'ev_w_uq': dense((NE, MLA_Q_RANK, MLA_HEADS * (MLA_NOPE + MLA_ROPE)), MLA_Q_RANK),
        'ev_w_ukv': dense((NE, MLA_KV_RANK, MLA_HEADS * (MLA_NOPE + MLA_V)), MLA_KV_RANK),
        'ev_conv_w': dense((NE, GDN_CONV, conv_ch), GDN_CONV),
        'ev_a_log': jnp.log(a_init),
        'ev_dt_bias': dt + jnp.log(-jnp.expm1(-dt)),
        'ev_gdn_norm': gain((NE, GDN_DV)),
        'ev_w_out': dense((NE, EVEN_MIX, D), EVEN_MIX),
        'od_attn_norm': gain((NO, D)),
        'od_w_in': dense((NO, D, ODD_IN), D),
        'od_pe_k': small((NO, CMP_BLOCK, NSA_DIM)),
        'od_w1_k': dense((NO, flat_cmp, CMP_HIDDEN), flat_cmp),
        'od_w2_k': dense((NO, CMP_HIDDEN, NSA_DIM), CMP_HIDDEN),
        'od_pe_v': small((NO, CMP_BLOCK, NSA_DIM)),
        'od_w1_v': dense((NO, flat_cmp, CMP_HIDDEN), flat_cmp),
        'od_w2_v': dense((NO, CMP_HIDDEN, NSA_DIM), CMP_HIDDEN),
        'od_w_out': dense((NO, ODD_MIX, D), ODD_MIX),
        'ffn_norm': gain((DEPTH, D)),
        'ffn_w_gate': dense((DEPTH, D, FFN_HIDDEN), D),
        'ffn_w_up': dense((DEPTH, D, FFN_HIDDEN), D),
        'ffn_w_down': dense((DEPTH, FFN_HIDDEN, D), FFN_HIDDEN),
        'final_norm': gain((D,)),
    }


def reference(x, ev_attn_norm, ev_w_in, ev_q_norm, ev_kv_norm, ev_w_uq, ev_w_ukv, ev_conv_w,
              ev_a_log, ev_dt_bias, ev_gdn_norm, ev_w_out, od_attn_norm, od_w_in, od_pe_k, od_w1_k,
              od_w2_k, od_pe_v, od_w1_v, od_w2_v, od_w_out, ffn_norm, ffn_w_gate, ffn_w_up,
              ffn_w_down, final_norm):
    S = x.shape[1]
    pos = jnp.arange(S, dtype=jnp.int32)
    h = x
    for layer in range(DEPTH):
        i = layer // 2
        if layer % 2 == 0:
            mix = even_mixer(rmsnorm(h, ev_attn_norm[i]), ev_w_in[i], ev_q_norm[i], ev_kv_norm[i],
                             ev_w_uq[i], ev_w_ukv[i], ev_conv_w[i], ev_a_log[i], ev_dt_bias[i],
                             ev_gdn_norm[i], ev_w_out[i], pos)
        else:
            mix = odd_mixer(rmsnorm(h, od_attn_norm[i]), od_w_in[i], od_pe_k[i], od_w1_k[i], od_w2_k[i],
                            od_pe_v[i], od_w1_v[i], od_w2_v[i], od_w_out[i])
        h = h + mix.astype(h.dtype)
        h = h + swiglu(rmsnorm(h, ffn_norm[layer]), ffn_w_gate[layer], ffn_w_up[layer], ffn_w_down[layer]).astype(h.dtype)
    return rmsnorm(h, final_norm)
```

```python
import functools
import math

import jax
import jax.numpy as jnp
import numpy as np
from jax import lax
from jax.experimental import pallas as pl
from jax.experimental.pallas import tpu as pltpu

F32 = jnp.float32
BF16 = jnp.bfloat16

EPS = 1e-6
NEG = -1e30
BIG = 1e30
LANES = 128

MLA_HEADS = 8
MLA_Q_RANK = 384
MLA_KV_RANK = 256
MLA_NOPE = 64
MLA_ROPE = 32
MLA_V = 64
ROPE_BASE = 10000.0
GDN_HEADS = 4
GDN_DK = 128
GDN_DV = 128
GDN_CONV = 4
GDN_CHUNK = 64
NSA_HEADS = 16
NSA_GROUPS = 4
NSA_HPG = 4
NSA_DIM = 64
CMP_BLOCK = 32
CMP_STRIDE = 16
CMP_HIDDEN = 256
SEL_BLOCK = 64
SEL_TOPN = 16
WINDOW = 512

VMEM_LIMIT = 56 * 1024 * 1024
HIGHEST = lax.Precision.HIGHEST


def _cparams(sem):
    return pltpu.CompilerParams(dimension_semantics=sem, vmem_limit_bytes=VMEM_LIMIT)


def _dot(a, b):
    return jnp.dot(a.astype(BF16), b.astype(BF16), preferred_element_type=F32)


def _dot_nt(a, b):
    return lax.dot_general(a.astype(BF16), b.astype(BF16), (((1,), (1,)), ((), ())),
                           preferred_element_type=F32)


def _dot_tn(a, b):
    return lax.dot_general(a.astype(BF16), b.astype(BF16), (((0,), (0,)), ((), ())),
                           preferred_element_type=F32)


def _dot_f32(a, b):
    return jnp.dot(a, b, preferred_element_type=F32, precision=HIGHEST)


def _rms(x, g):
    var = jnp.mean(x * x, axis=-1, keepdims=True)
    return x * lax.rsqrt(var + EPS) * g


def _silu(x):
    return x * (1.0 / (1.0 + jnp.exp(-x)))


def _sigmoid(x):
    return 1.0 / (1.0 + jnp.exp(-x))


def _rms_matmul_kernel(x_ref, g_ref, w_ref, *out_refs, splits):
    xn = _rms(x_ref[...], g_ref[...])
    acc = _dot(xn, w_ref[...])
    off = 0
    for o_ref, n in zip(out_refs, splits):
        o_ref[...] = acc[:, off:off + n].astype(o_ref.dtype)
        off += n


def rms_matmul(x, g, w, splits, tm=256):
    T, K = x.shape
    N = w.shape[1]
    assert sum(splits) == N and T % tm == 0
    return pl.pallas_call(
        functools.partial(_rms_matmul_kernel, splits=splits),
        grid=(T // tm,),
        in_specs=[pl.BlockSpec((tm, K), lambda i: (i, 0)),
                  pl.BlockSpec((1, K), lambda i: (0, 0)),
                  pl.BlockSpec((K, N), lambda i: (0, 0))],
        out_specs=[pl.BlockSpec((tm, n), lambda i: (i, 0)) for n in splits],
        out_shape=[jax.ShapeDtypeStruct((T, n), F32) for n in splits],
        compiler_params=_cparams(("parallel",)),
        name="rms_matmul",
    )(x, g.reshape(1, K), w)


def _proj_residual_kernel(*refs, n_in):
    a_refs = refs[:n_in]
    w_refs = refs[n_in:2 * n_in]
    res_ref = refs[2 * n_in]
    o_ref = refs[2 * n_in + 1]
    acc = res_ref[...]
    for a_ref, w_ref in zip(a_refs, w_refs):
        acc = acc + _dot(a_ref[...], w_ref[...])
    o_ref[...] = acc


def proj_residual(a_list, w_list, res, tm=512):
    T, N = res.shape
    n_in = len(a_list)
    in_specs = [pl.BlockSpec((tm, a.shape[1]), lambda i: (i, 0)) for a in a_list]
    in_specs += [pl.BlockSpec(w.shape, lambda i: (0, 0)) for w in w_list]
    in_specs += [pl.BlockSpec((tm, N), lambda i: (i, 0))]
    return pl.pallas_call(
        functools.partial(_proj_residual_kernel, n_in=n_in),
        grid=(T // tm,),
        in_specs=in_specs,
        out_specs=pl.BlockSpec((tm, N), lambda i: (i, 0)),
        out_shape=jax.ShapeDtypeStruct((T, N), F32),
        compiler_params=_cparams(("parallel",)),
        name="proj_residual",
    )(*a_list, *w_list, res)


def _ffn_kernel(h_ref, g_ref, wg_ref, wu_ref, wd_ref, o_ref, *, chunks):
    h = h_ref[...]
    xn = _rms(h, g_ref[...]).astype(BF16)
    acc = h
    off = 0
    for n in chunks:
        gate = jnp.dot(xn, wg_ref[:, off:off + n], preferred_element_type=F32)
        up = jnp.dot(xn, wu_ref[:, off:off + n], preferred_element_type=F32)
        act = (_silu(gate) * up).astype(BF16)
        acc = acc + jnp.dot(act, wd_ref[off:off + n, :], preferred_element_type=F32)
        off += n
    o_ref[...] = acc


def ffn(h, g, wg, wu, wd, tm=512):
    T, D = h.shape
    Hd = wg.shape[1]
    nch = 2 if (Hd % 256 == 0) else 1
    chunks = (Hd // nch,) * nch
    single = pl.Buffered(1)
    return pl.pallas_call(
        functools.partial(_ffn_kernel, chunks=chunks),
        grid=(T // tm,),
        in_specs=[pl.BlockSpec((tm, D), lambda i: (i, 0)),
                  pl.BlockSpec((1, D), lambda i: (0, 0)),
                  pl.BlockSpec((D, Hd), lambda i: (0, 0), pipeline_mode=single),
                  pl.BlockSpec((D, Hd), lambda i: (0, 0), pipeline_mode=single),
                  pl.BlockSpec((Hd, D), lambda i: (0, 0), pipeline_mode=single)],
        out_specs=pl.BlockSpec((tm, D), lambda i: (i, 0)),
        out_shape=jax.ShapeDtypeStruct((T, D), F32),
        compiler_params=_cparams(("parallel",)),
        name="ffn",
    )(h, g.reshape(1, D), wg, wu, wd)


def _rmsnorm_kernel(x_ref, g_ref, o_ref):
    o_ref[...] = _rms(x_ref[...], g_ref[...])


def rmsnorm_call(x, g, tm=1024):
    T, D = x.shape
    return pl.pallas_call(
        _rmsnorm_kernel,
        grid=(T // tm,),
        in_specs=[pl.BlockSpec((tm, D), lambda i: (i, 0)),
                  pl.BlockSpec((1, D), lambda i: (0, 0))],
        out_specs=pl.BlockSpec((tm, D), lambda i: (i, 0)),
        out_shape=jax.ShapeDtypeStruct((T, D), F32),
        compiler_params=_cparams(("parallel",)),
        name="final_norm",
    )(x, g.reshape(1, D))


MLA_HB = 128


def _mla_prep_kernel(zm_ref, qn_ref, kvn_ref, wq_ref, wqr_ref, wk_ref, wv_ref, c_ref, s_ref,
                     q_out, k_out, v_out):
    zm = zm_ref[...]
    cq = zm[:, :MLA_Q_RANK]
    ckv = zm[:, MLA_Q_RANK:MLA_Q_RANK + MLA_KV_RANK]
    m1 = zm[:, 640:768]
    m2 = zm[:, 768:896]
    cqn = _rms(cq, qn_ref[...]).astype(BF16)
    ckvn = _rms(ckv, kvn_ref[...]).astype(BF16)
    q = jnp.dot(cqn, wq_ref[...], preferred_element_type=F32)
    qr = jnp.dot(cqn, wqr_ref[...], preferred_element_type=F32)
    kn = jnp.dot(ckvn, wk_ref[...], preferred_element_type=F32)
    cos = c_ref[...]
    sin = s_ref[...]
    lane = lax.broadcasted_iota(jnp.int32, cos.shape, 1)
    rope_lane = (lane >= MLA_NOPE) & (lane < MLA_NOPE + MLA_ROPE)
    krot = jnp.where(rope_lane, m1 * cos + m2 * sin, 0.0)
    scale = (MLA_NOPE + MLA_ROPE) ** -0.5
    for h in range(MLA_HEADS):
        sl = slice(h * MLA_HB, (h + 1) * MLA_HB)
        q_out[:, sl] = ((q[:, sl] * cos + qr[:, sl] * sin) * scale).astype(BF16)
        k_out[:, sl] = (kn[:, sl] + krot).astype(BF16)
    v_out[...] = jnp.dot(ckvn, wv_ref[...], preferred_element_type=F32).astype(BF16)


def mla_prep(zm, qn, kvn, wq, wqr, wk, wv, cos_t, sin_t, S, tm=512):
    T = zm.shape[0]
    nsb = S // tm
    HW = MLA_HEADS * MLA_HB
    full = lambda a: pl.BlockSpec(a.shape, lambda i: (0, 0))
    return pl.pallas_call(
        _mla_prep_kernel,
        grid=(T // tm,),
        in_specs=[pl.BlockSpec((tm, zm.shape[1]), lambda i: (i, 0)),
                  full(qn), full(kvn), full(wq), full(wqr), full(wk), full(wv),
                  pl.BlockSpec((tm, LANES), lambda i: (i % nsb, 0)),
                  pl.BlockSpec((tm, LANES), lambda i: (i % nsb, 0))],
        out_specs=[pl.BlockSpec((tm, HW), lambda i: (i, 0)),
                   pl.BlockSpec((tm, HW), lambda i: (i, 0)),
                   pl.BlockSpec((tm, MLA_HEADS * MLA_V), lambda i: (i, 0))],
        out_shape=[jax.ShapeDtypeStruct((T, HW), BF16),
                   jax.ShapeDtypeStruct((T, HW), BF16),
                   jax.ShapeDtypeStruct((T, MLA_HEADS * MLA_V), BF16)],
        compiler_params=_cparams(("parallel",)),
        name="mla_prep",
    )(zm, qn, kvn, wq, wqr, wk, wv, cos_t, sin_t)


def _mla_flash_kernel(qi_ref, kj_ref, q_ref, k_ref, v_ref, o_ref, m_scr, l_scr, acc_scr, *, tq, tk):
    n = pl.program_id(2)
    i = qi_ref[n]
    j = kj_ref[n]

    @pl.when(j == 0)
    def _():
        m_scr[...] = jnp.full(m_scr.shape, NEG, F32)
        l_scr[...] = jnp.zeros(l_scr.shape, F32)
        acc_scr[...] = jnp.zeros(acc_scr.shape, F32)

    qpos = i * tq + lax.broadcasted_iota(jnp.int32, (tq, tk), 0)
    kpos = j * tk + lax.broadcasted_iota(jnp.int32, (tq, tk), 1)
    mask = kpos <= qpos
    v = v_ref[...]
    for hh in range(2):
        sl = slice(hh * MLA_HB, (hh + 1) * MLA_HB)
        s = lax.dot_general(q_ref[:, sl], k_ref[:, sl], (((1,), (1,)), ((), ())),
                            preferred_element_type=F32)
        s = jnp.where(mask, s, NEG)
        m_prev = m_scr[hh]
        m_new = jnp.maximum(m_prev, jnp.max(s, axis=1, keepdims=True))
        alpha = jnp.exp(m_prev - m_new)
        p = jnp.exp(s - m_new[:, :1])
        l_scr[hh] = alpha * l_scr[hh] + jnp.sum(p, axis=1, keepdims=True)
        acc_scr[hh] = alpha * acc_scr[hh] + jnp.dot(p.astype(BF16), v, preferred_element_type=F32)
        m_scr[hh] = m_new

    @pl.when(j == i)
    def _():
        o0 = acc_scr[0] * (1.0 / l_scr[0])
        o1 = acc_scr[1] * (1.0 / l_scr[1])
        lane = lax.broadcasted_iota(jnp.int32, o0.shape, 1)
        o_ref[...] = jnp.where(lane < MLA_V, o0, o1)


def _tri_pairs(nq, ratio=1):
    qi, kj = [], []
    for i in range(nq):
        for j in range(i // ratio + 1):
            qi.append(i)
            kj.append(j)
    return jnp.asarray(qi, jnp.int32), jnp.asarray(kj, jnp.int32)


def mla_flash(q, k, v, B, S, tq=512):
    T = q.shape[0]
    tk = tq
    nq = S // tq
    qi, kj = _tri_pairs(nq)
    npairs = int(qi.shape[0])
    grid_spec = pltpu.PrefetchScalarGridSpec(
        num_scalar_prefetch=2,
        grid=(B, MLA_HEADS // 2, npairs),
        in_specs=[pl.BlockSpec((tq, 2 * MLA_HB), lambda b, p, n, qi, kj: (b * nq + qi[n], p)),
                  pl.BlockSpec((tk, 2 * MLA_HB), lambda b, p, n, qi, kj: (b * nq + kj[n], p)),
                  pl.BlockSpec((tk, 2 * MLA_V), lambda b, p, n, qi, kj: (b * nq + kj[n], p))],
        out_specs=pl.BlockSpec((tq, 2 * MLA_V), lambda b, p, n, qi, kj: (b * nq + qi[n], p)),
        scratch_shapes=[pltpu.VMEM((2, tq, LANES), F32),
                        pltpu.VMEM((2, tq, LANES), F32),
                        pltpu.VMEM((2, tq, LANES), F32)],
    )
    return pl.pallas_call(
        functools.partial(_mla_flash_kernel, tq=tq, tk=tk),
        grid_spec=grid_spec,
        out_shape=jax.ShapeDtypeStruct((T, MLA_HEADS * MLA_V), F32),
        compiler_params=_cparams(("parallel", "parallel", "arbitrary")),
        name="mla_flash",
    )(qi, kj, q, k, v)


GDN_QKV = 3 * GDN_HEADS * GDN_DK
BETA_LANE = 96
DECAY_LANE = 100
HALO = 8


def _gdn_prep_kernel(z_ref, halo_ref, m1_ref, cw_ref, alog_ref, dt_ref, qkv_out, gb_out, gbt_out,
                     *, tm, tiles_per_seq):
    i = pl.program_id(0)
    x = z_ref[...]
    halo = halo_ref[...]
    halo = jnp.where(i % tiles_per_seq == 0, jnp.zeros_like(halo), halo)
    xe = jnp.concatenate([halo, x], axis=0)
    cw = cw_ref[...]
    acc = x * cw[GDN_CONV - 1:GDN_CONV, :]
    for d in range(1, GDN_CONV):
        acc = acc + xe[HALO - d:HALO - d + tm, :] * cw[GDN_CONV - 1 - d:GDN_CONV - d, :]
    y = _silu(acc)
    nq = GDN_HEADS * GDN_DK
    for h in range(GDN_HEADS):
        sl = slice(h * GDN_DK, (h + 1) * GDN_DK)
        qh = y[:, sl]
        qkv_out[:, sl] = qh * lax.rsqrt(jnp.sum(qh * qh, axis=-1, keepdims=True) + EPS) * (GDN_DK ** -0.5)
        sl2 = slice(nq + h * GDN_DK, nq + (h + 1) * GDN_DK)
        kh = y[:, sl2]
        qkv_out[:, sl2] = kh * lax.rsqrt(jnp.sum(kh * kh, axis=-1, keepdims=True) + EPS)
    qkv_out[:, 2 * nq:] = y[:, 2 * nq:]
    m1 = m1_ref[...]
    lane = lax.broadcasted_iota(jnp.int32, m1.shape, 1)
    beta = _sigmoid(m1)
    xa = m1 + dt_ref[...]
    softplus = jnp.maximum(xa, 0.0) + jnp.log(1.0 + jnp.exp(-jnp.abs(xa)))
    decay = -jnp.exp(alog_ref[...]) * softplus
    gb = jnp.where(lane < DECAY_LANE, beta, decay)
    gb_out[...] = gb
    r = lax.broadcasted_iota(jnp.int32, (8, LANES), 0)
    c = lax.broadcasted_iota(jnp.int32, (8, LANES), 1)
    pick = (c == r + BETA_LANE).astype(F32)
    gbt_out[...] = lax.dot_general(pick, gb, (((1,), (1,)), ((), ())),
                                   preferred_element_type=F32, precision=HIGHEST)


def gdn_prep(zg, zm, conv_w, alog_v, dt_v, S, tm=512):
    T = zg.shape[0]
    tiles_per_seq = S // tm
    hb = tm // HALO
    m1_blk = 640 // LANES
    return pl.pallas_call(
        functools.partial(_gdn_prep_kernel, tm=tm, tiles_per_seq=tiles_per_seq),
        grid=(T // tm,),
        in_specs=[pl.BlockSpec((tm, GDN_QKV), lambda i: (i, 0)),
                  pl.BlockSpec((HALO, GDN_QKV), lambda i: (jnp.maximum(i * hb - 1, 0), 0)),
                  pl.BlockSpec((tm, LANES), lambda i: (i, m1_blk)),
                  pl.BlockSpec((GDN_CONV, GDN_QKV), lambda i: (0, 0)),
                  pl.BlockSpec((1, LANES), lambda i: (0, 0)),
                  pl.BlockSpec((1, LANES), lambda i: (0, 0))],
        out_specs=[pl.BlockSpec((tm, GDN_QKV), lambda i: (i, 0)),
                   pl.BlockSpec((tm, LANES), lambda i: (i, 0)),
                   pl.BlockSpec((8, tm), lambda i: (0, i))],
        out_shape=[jax.ShapeDtypeStruct((T, GDN_QKV), F32),
                   jax.ShapeDtypeStruct((T, LANES), F32),
                   jax.ShapeDtypeStruct((8, T), F32)],
        compiler_params=_cparams(("parallel",)),
        name="gdn_prep",
    )(zg, zg, zm, conv_w, alog_v, dt_v)


def _tri_inverse(a, eye, diag_blocks):
    ad = jnp.where(diag_blocks, a, 0.0)
    ao = a - ad
    a2 = _dot_f32(ad, ad)
    a4 = _dot_f32(a2, a2)
    a8 = _dot_f32(a4, a4)
    dinv = _dot_f32(_dot_f32(_dot_f32(eye - ad, eye + a2), eye + a4), eye + a8)
    n = _dot_f32(dinv, ao)
    n2 = _dot_f32(n, n)
    return _dot_f32(_dot_f32(eye - n, eye + n2), dinv)


def _gdn_chunk_kernel(qkv_ref, gb_ref, gbt_ref, zg_ref, norm_ref, o_ref, state_scr, *, lb):
    C = GDN_CHUNK
    DK = GDN_DK
    nq = GDN_HEADS * DK

    @pl.when(pl.program_id(1) == 0)
    def _():
        state_scr[...] = jnp.zeros(state_scr.shape, F32)

    ii = lax.broadcasted_iota(jnp.int32, (C, C), 0)
    jj = lax.broadcasted_iota(jnp.int32, (C, C), 1)
    lower = ii >= jj
    strict = ii > jj
    lmat = lower.astype(F32)
    umat = (ii <= jj).astype(F32)
    eye = (ii == jj).astype(F32)
    diag_blocks = (ii // 16) == (jj // 16)
    norm_w = norm_ref[...]

    for h in range(GDN_HEADS):
        state = state_scr[h]
        for c in range(lb // C):
            rows = slice(c * C, (c + 1) * C)
            q = qkv_ref[rows, h * DK:(h + 1) * DK]
            k = qkv_ref[rows, nq + h * DK:nq + (h + 1) * DK]
            v = qkv_ref[rows, 2 * nq + h * DK:2 * nq + (h + 1) * DK]
            gb = gb_ref[rows, :]
            beta = jnp.broadcast_to(gb[:, BETA_LANE + h:BETA_LANE + h + 1], (C, DK))
            g_col = jnp.broadcast_to(gb[:, DECAY_LANE + h:DECAY_LANE + h + 1], (C, DK))
            g_row = jnp.broadcast_to(gbt_ref[4 + h:5 + h, rows], (C, C))
            gc = _dot_f32(lmat, g_col)
            gr = _dot_f32(g_row, umat)
            decay = jnp.exp(jnp.where(lower, gc[:, :C] - gr, NEG))
            kb = k * beta
            eg = jnp.exp(gc)
            a = jnp.where(strict, _dot_nt(kb, k) * decay, 0.0)
            t_inv = _tri_inverse(a, eye, diag_blocks)
            u = _dot(t_inv, v * beta)
            w = _dot(t_inv, kb * eg)
            intra = _dot_nt(q, k) * decay
            g_last = gc[C - 1:C, :]
            k_dec = k * jnp.exp(g_last - gc)
            v_new = u - _dot(w, state)
            o = _dot(q * eg, state) + _dot(intra, v_new)
            state = state * jnp.exp(g_last) + _dot_tn(k_dec, v_new)
            on = o * lax.rsqrt(jnp.mean(o * o, axis=-1, keepdims=True) + EPS) * norm_w
            o_ref[rows, h * DK:(h + 1) * DK] = on * _silu(zg_ref[rows, h * DK:(h + 1) * DK])
        state_scr[h] = state


def gdn_chunk(qkv, gb, gbt, zg, norm_w, B, S, lb=256):
    T = qkv.shape[0]
    nsb = S // lb
    VW = GDN_HEADS * GDN_DV
    zg_blk = GDN_QKV // VW
    return pl.pallas_call(
        functools.partial(_gdn_chunk_kernel, lb=lb),
        grid=(B, nsb),
        in_specs=[pl.BlockSpec((lb, GDN_QKV), lambda b, s: (b * nsb + s, 0)),
                  pl.BlockSpec((lb, LANES), lambda b, s: (b * nsb + s, 0)),
                  pl.BlockSpec((8, lb), lambda b, s: (0, b * nsb + s)),
                  pl.BlockSpec((lb, VW), lambda b, s: (b * nsb + s, zg_blk)),
                  pl.BlockSpec((1, GDN_DV), lambda b, s: (0, 0))],
        out_specs=pl.BlockSpec((lb, VW), lambda b, s: (b * nsb + s, 0)),
        out_shape=jax.ShapeDtypeStruct((T, VW), F32),
        scratch_shapes=[pltpu.VMEM((GDN_HEADS, GDN_DK, GDN_DV), F32)],
        compiler_params=_cparams(("parallel", "arbitrary")),
        name="gdn_chunk",
    )(qkv, gb, gbt, zg, norm_w)


def _compress_kernel(x_ref, pea_ref, peb_ref, w1a_ref, w1b_ref, w2_ref, o_ref, p1_scr, p2_scr):
    kc = pl.program_id(1)

    @pl.when(kc == 0)
    def _():
        p1_scr[...] = jnp.zeros(p1_scr.shape, F32)
        p2_scr[...] = jnp.zeros(p2_scr.shape, F32)

    x = x_ref[0]
    p1_scr[...] += _dot(x + pea_ref[...], w1a_ref[...])
    p2_scr[...] += _dot(x + peb_ref[...], w1b_ref[...])

    @pl.when(kc == pl.num_programs(1) - 1)
    def _():
        p2 = p2_scr[...]
        rows = p2.shape[0]
        hid = p1_scr[...] + pltpu.roll(p2, rows - 1, 0)
        c0 = math.sqrt(2.0 / math.pi)
        act = 0.5 * hid * (1.0 + jnp.tanh(c0 * (hid + 0.044715 * (hid * hid * hid))))
        o_ref[0] = _dot(act, w2_ref[...])


def nsa_compress(x, pe_a, pe_b, w1a, w1b, w2, B, S, kchunk=512):
    nr = S // CMP_STRIDE
    xr = x.reshape(B, nr, CMP_STRIDE * NSA_GROUPS * NSA_DIM)
    KW = xr.shape[2]
    HW = NSA_GROUPS * CMP_HIDDEN
    OW = NSA_GROUPS * NSA_DIM
    return pl.pallas_call(
        _compress_kernel,
        grid=(B, KW // kchunk),
        in_specs=[pl.BlockSpec((1, nr, kchunk), lambda b, k: (b, 0, k)),
                  pl.BlockSpec((1, kchunk), lambda b, k: (0, k)),
                  pl.BlockSpec((1, kchunk), lambda b, k: (0, k)),
                  pl.BlockSpec((kchunk, HW), lambda b, k: (k, 0)),
                  pl.BlockSpec((kchunk, HW), lambda b, k: (k, 0)),
                  pl.BlockSpec((HW, OW), lambda b, k: (0, 0))],
        out_specs=pl.BlockSpec((1, nr, OW), lambda b, k: (b, 0, 0)),
        out_shape=jax.ShapeDtypeStruct((B, nr, OW), F32),
        scratch_shapes=[pltpu.VMEM((nr, HW), F32), pltpu.VMEM((nr, HW), F32)],
        compiler_params=_cparams(("parallel", "arbitrary")),
        name="nsa_compress",
    )(xr, pe_a, pe_b, w1a, w1b, w2)


def _slope(h):
    return float(2.0 ** (-8.0 * (h + 1) / NSA_HEADS))


def _nsa_cmp_kernel(q_ref, kc_ref, vc_ref, zg_ref, ov_ref, o_ref, sel_ref, *, tq, n_sel):
    i = pl.program_id(1)
    ncmp = kc_ref.shape[1]
    D = NSA_DIM
    scale = D ** -0.5
    qpos = i * tq + lax.broadcasted_iota(jnp.int32, (tq, 1), 0)
    nidx = lax.broadcasted_iota(jnp.int32, (1, ncmp), 1)
    valid = (nidx * CMP_STRIDE + (CMP_BLOCK - 1)) <= qpos
    any_valid = (qpos >= CMP_BLOCK - 1).astype(F32)
    dist = qpos.astype(F32) - (nidx.astype(F32) * CMP_STRIDE + 0.5 * (CMP_BLOCK - 1))
    gates = _sigmoid(zg_ref[...])
    kc = kc_ref[0].astype(BF16)
    vc = vc_ref[0].astype(BF16)
    jf = lax.broadcasted_iota(jnp.int32, (1, n_sel), 1).astype(F32)
    qblk = (qpos // SEL_BLOCK).astype(F32)
    forced = (jf == 0.0) | (jf == qblk) | (jf == qblk - 1.0)
    causal_blk = jf <= qblk
    for g in range(NSA_GROUPS):
        kg = kc[:, g * D:(g + 1) * D]
        vg = vc[:, g * D:(g + 1) * D]
        psum = jnp.zeros((tq, ncmp), F32)
        for r in range(NSA_HPG):
            h = g * NSA_HPG + r
            qh = q_ref[:, h * D:(h + 1) * D]
            s = _dot_nt(qh, kg) * scale - _slope(h) * dist
            s = jnp.where(valid, s, NEG)
            e = jnp.exp(s - jnp.max(s, axis=1, keepdims=True))
            p = e * (any_valid / jnp.sum(e, axis=1, keepdims=True))
            psum = psum + p
            o_ref[:, h * D:(h + 1) * D] = gates[:, 3 * h:3 * h + 1] * _dot(p, vg)
        imp = _dot_f32(psum, ov_ref[...])
        work = jnp.where(forced, BIG, jnp.where(causal_blk, imp, NEG))
        selm = jnp.zeros((tq, n_sel), F32)
        for _ in range(min(SEL_TOPN, n_sel)):
            mx = jnp.max(work, axis=1, keepdims=True)
            first = jnp.min(jnp.where(work == mx, jf, float(n_sel)), axis=1, keepdims=True)
            pick = jf == first
            selm = jnp.where(pick, 1.0, selm)
            work = jnp.where(pick, -jnp.inf, work)
        sel_ref[:, g * n_sel:(g + 1) * n_sel] = selm.astype(BF16)


def nsa_cmp(q, k_cmp, v_cmp, zg, overlap, B, S, tq=256):
    T = q.shape[0]
    nq = S // tq
    n_sel = S // SEL_BLOCK
    ncmp = k_cmp.shape[1]
    QW = NSA_HEADS * NSA_DIM
    KW = NSA_GROUPS * NSA_DIM
    return pl.pallas_call(
        functools.partial(_nsa_cmp_kernel, tq=tq, n_sel=n_sel),
        grid=(B, nq),
        in_specs=[pl.BlockSpec((tq, QW), lambda b, i: (b * nq + i, 0)),
                  pl.BlockSpec((1, ncmp, KW), lambda b, i: (b, 0, 0)),
                  pl.BlockSpec((1, ncmp, KW), lambda b, i: (b, 0, 0)),
                  pl.BlockSpec((tq, LANES), lambda b, i: (b * nq + i, 0)),
                  pl.BlockSpec((ncmp, n_sel), lambda b, i: (0, 0))],
        out_specs=[pl.BlockSpec((tq, QW), lambda b, i: (b * nq + i, 0)),
                   pl.BlockSpec((tq, NSA_GROUPS * n_sel), lambda b, i: (b * nq + i, 0))],
        out_shape=[jax.ShapeDtypeStruct((T, QW), F32),
                   jax.ShapeDtypeStruct((T, NSA_GROUPS * n_sel), BF16)],
        compiler_params=_cparams(("parallel", "parallel")),
        name="nsa_cmp",
    )(q, k_cmp, v_cmp, zg, overlap)


def _nsa_sel_kernel(qi_ref, kj_ref, q_ref, k_ref, v_ref, sel_ref, zg_ref, prev_ref, o_ref,
                    m_scr, l_scr, acc_scr, *, tq, tk, n_sel):
    n = pl.program_id(1)
    i = qi_ref[n]
    j = kj_ref[n]
    D = NSA_DIM
    scale = D ** -0.5

    @pl.when(j == 0)
    def _():
        m_scr[...] = jnp.full(m_scr.shape, NEG, F32)
        l_scr[...] = jnp.zeros(l_scr.shape, F32)
        acc_scr[...] = jnp.zeros(acc_scr.shape, F32)

    qpos = i * tq + lax.broadcasted_iota(jnp.int32, (tq, tk), 0)
    kpos = j * tk + lax.broadcasted_iota(jnp.int32, (tq, tk), 1)
    dist = (qpos - kpos).astype(F32)
    causal = kpos <= qpos
    blk = lax.broadcasted_iota(jnp.int32, (n_sel, tk), 0)
    key_blk = (j * tk + lax.broadcasted_iota(jnp.int32, (n_sel, tk), 1)) // SEL_BLOCK
    expand = jnp.where(blk == key_blk, 1.0, 0.0).astype(BF16)
    k = k_ref[...].astype(BF16)
    v = v_ref[...].astype(BF16)
    for g in range(NSA_GROUPS):
        picked = jnp.dot(sel_ref[:, g * n_sel:(g + 1) * n_sel], expand, preferred_element_type=F32)
        allowed = causal & (picked > 0.5)
        kg = k[:, g * D:(g + 1) * D]
        vg = v[:, g * D:(g + 1) * D]
        for r in range(NSA_HPG):
            h = g * NSA_HPG + r
            qh = q_ref[:, h * D:(h + 1) * D]
            s = _dot_nt(qh, kg) * scale - _slope(h) * dist
            s = jnp.where(allowed, s, NEG)
            m_prev = m_scr[h]
            m_new = jnp.maximum(m_prev, jnp.max(s, axis=1, keepdims=True))
            alpha = jnp.exp(m_prev - m_new)
            p = jnp.where(allowed, jnp.exp(s - m_new[:, :1]), 0.0)
            l_scr[h] = alpha * l_scr[h] + jnp.sum(p, axis=1, keepdims=True)
            acc_scr[h] = alpha[:, :D] * acc_scr[h] + _dot(p, vg)
            m_scr[h] = m_new

    @pl.when(j == (i * tq + tq - 1) // tk)
    def _():
        gates = _sigmoid(zg_ref[...])
        for h in range(NSA_HEADS):
            o_h = acc_scr[h] * (1.0 / l_scr[h][:, :D])
            o_ref[:, h * D:(h + 1) * D] = (prev_ref[:, h * D:(h + 1) * D]
                                           + gates[:, 3 * h + 1:3 * h + 2] * o_h)


def nsa_sel(q, ks, vs, sel, zg, prev, B, S, tq=256, tk=512):
    T = q.shape[0]
    tk = min(tk, S)
    nq = S // tq
    nk = S // tk
    n_sel = S // SEL_BLOCK
    QW = NSA_HEADS * NSA_DIM
    KW = NSA_GROUPS * NSA_DIM
    pairs = [(i, j) for i in range(nq) for j in range((i * tq + tq - 1) // tk + 1)]
    qi = jnp.asarray([p[0] for p in pairs], jnp.int32)
    kj = jnp.asarray([p[1] for p in pairs], jnp.int32)
    npairs = len(pairs)
    qmap = lambda b, n, qi, kj: (b * nq + qi[n], 0)
    kmap = lambda b, n, qi, kj: (b * nk + kj[n], 0)
    grid_spec = pltpu.PrefetchScalarGridSpec(
        num_scalar_prefetch=2,
        grid=(B, npairs),
        in_specs=[pl.BlockSpec((tq, QW), qmap),
                  pl.BlockSpec((tk, KW), kmap),
                  pl.BlockSpec((tk, KW), kmap),
                  pl.BlockSpec((tq, NSA_GROUPS * n_sel), qmap),
                  pl.BlockSpec((tq, LANES), qmap),
                  pl.BlockSpec((tq, QW), qmap)],
        out_specs=pl.BlockSpec((tq, QW), qmap),
        scratch_shapes=[pltpu.VMEM((NSA_HEADS, tq, LANES), F32),
                        pltpu.VMEM((NSA_HEADS, tq, LANES), F32),
                        pltpu.VMEM((NSA_HEADS, tq, NSA_DIM), F32)],
    )
    return pl.pallas_call(
        functools.partial(_nsa_sel_kernel, tq=tq, tk=tk, n_sel=n_sel),
        grid_spec=grid_spec,
        out_shape=jax.ShapeDtypeStruct((T, QW), F32),
        compiler_params=_cparams(("parallel", "arbitrary")),
        name="nsa_sel",
    )(qi, kj, q, ks, vs, sel, zg, prev)


def _nsa_win_kernel(q_ref, k0_ref, k1_ref, k2_ref, v0_ref, v1_ref, v2_ref, zg_ref, prev_ref, o_ref, *, tq):
    i = pl.program_id(1)
    D = NSA_DIM
    scale = D ** -0.5
    nback = WINDOW // tq
    tkw = (nback + 1) * tq
    k = jnp.concatenate([k0_ref[...], k1_ref[...], k2_ref[...]], axis=0).astype(BF16)
    v = jnp.concatenate([v0_ref[...], v1_ref[...], v2_ref[...]], axis=0).astype(BF16)
    qpos = i * tq + lax.broadcasted_iota(jnp.int32, (tq, tkw), 0)
    kpos = (i - nback) * tq + lax.broadcasted_iota(jnp.int32, (tq, tkw), 1)
    dw = qpos - kpos
    wvalid = (dw >= 0) & (dw < WINDOW) & (kpos >= 0)
    dist = dw.astype(F32)
    gates = _sigmoid(zg_ref[...])
    for g in range(NSA_GROUPS):
        kg = k[:, g * D:(g + 1) * D]
        vg = v[:, g * D:(g + 1) * D]
        for r in range(NSA_HPG):
            h = g * NSA_HPG + r
            qh = q_ref[:, h * D:(h + 1) * D]
            s = _dot_nt(qh, kg) * scale - _slope(h) * dist
            s = jnp.where(wvalid, s, NEG)
            e = jnp.exp(s - jnp.max(s, axis=1, keepdims=True))
            p = e * (1.0 / jnp.sum(e, axis=1, keepdims=True))
            o_ref[:, h * D:(h + 1) * D] = (prev_ref[:, h * D:(h + 1) * D]
                                           + gates[:, 3 * h + 2:3 * h + 3] * _dot(p, vg))


def nsa_win(q, kw, vw, zg, prev, B, S, tq=256):
    T = q.shape[0]
    assert WINDOW % tq == 0 and WINDOW // tq == 2
    nq = S // tq
    QW = NSA_HEADS * NSA_DIM
    KW = NSA_GROUPS * NSA_DIM
    qmap = lambda b, i: (b * nq + i, 0)
    back = lambda d: (lambda b, i: (b * nq + jnp.maximum(i - d, 0), 0))
    kspecs = [pl.BlockSpec((tq, KW), back(2)), pl.BlockSpec((tq, KW), back(1)), pl.BlockSpec((tq, KW), back(0))]
    return pl.pallas_call(
        functools.partial(_nsa_win_kernel, tq=tq),
        grid=(B, nq),
        in_specs=[pl.BlockSpec((tq, QW), qmap)] + kspecs + kspecs
                 + [pl.BlockSpec((tq, LANES), qmap), pl.BlockSpec((tq, QW), qmap)],
        out_specs=pl.BlockSpec((tq, QW), qmap),
        out_shape=jax.ShapeDtypeStruct((T, QW), F32),
        compiler_params=_cparams(("parallel", "parallel")),
        name="nsa_win",
    )(q, kw, kw, kw, vw, vw, vw, zg, prev)


def _rot_half_cols(w):
    half = w.shape[-1] // 2
    return jnp.concatenate([-w[..., half:], w[..., :half]], axis=-1)


def _even_weights(w_in, w_uq, w_ukv):
    D = w_in.shape[0]
    o = 0
    cuts = {}
    for name, n in (("cq", MLA_Q_RANK), ("ckv", MLA_KV_RANK), ("kr", MLA_ROPE), ("zq", 512), ("zk", 512),
                    ("zv", 512), ("zg", 512), ("zb", GDN_HEADS), ("za", GDN_HEADS)):
        cuts[name] = w_in[:, o:o + n]
        o += n
    z = lambda n: jnp.zeros((D, n), F32)
    misc1 = jnp.concatenate([z(MLA_NOPE), cuts["kr"], cuts["zb"], cuts["za"],
                             z(LANES - MLA_NOPE - MLA_ROPE - 2 * GDN_HEADS)], axis=1)
    misc2 = jnp.concatenate([z(MLA_NOPE), _rot_half_cols(cuts["kr"]), z(LANES - MLA_NOPE - MLA_ROPE)], axis=1)
    w_even = jnp.concatenate([cuts["cq"], cuts["ckv"], misc1, misc2,
                              cuts["zq"], cuts["zk"], cuts["zv"], cuts["zg"]], axis=1).astype(BF16)
    qd = MLA_NOPE + MLA_ROPE
    wq3 = w_uq.reshape(MLA_Q_RANK, MLA_HEADS, qd)
    zq = jnp.zeros((MLA_Q_RANK, MLA_HEADS, MLA_HB - qd), F32)
    wq = jnp.concatenate([wq3, zq], axis=2).reshape(MLA_Q_RANK, MLA_HEADS * MLA_HB).astype(BF16)
    wqr = jnp.concatenate([jnp.zeros((MLA_Q_RANK, MLA_HEADS, MLA_NOPE), F32),
                           _rot_half_cols(wq3[:, :, MLA_NOPE:]), zq], axis=2)
    wqr = wqr.reshape(MLA_Q_RANK, MLA_HEADS * MLA_HB).astype(BF16)
    wkv3 = w_ukv.reshape(MLA_KV_RANK, MLA_HEADS, MLA_NOPE + MLA_V)
    wk = jnp.concatenate([wkv3[:, :, :MLA_NOPE], jnp.zeros((MLA_KV_RANK, MLA_HEADS, MLA_HB - MLA_NOPE), F32)],
                         axis=2).reshape(MLA_KV_RANK, MLA_HEADS * MLA_HB).astype(BF16)
    wv = wkv3[:, :, MLA_NOPE:].reshape(MLA_KV_RANK, MLA_HEADS * MLA_V).astype(BF16)
    return w_even, wq, wqr, wk, wv


def _rope_tables(S):
    half = MLA_ROPE // 2
    inv = ROPE_BASE ** (-jnp.arange(half, dtype=F32) / half)
    ang = jnp.arange(S, dtype=F32)[:, None] * inv[None, :]
    cos = jnp.cos(ang)
    sin = jnp.sin(ang)
    pad = jnp.zeros((S, LANES - MLA_NOPE - MLA_ROPE), F32)
    cos_t = jnp.concatenate([jnp.ones((S, MLA_NOPE), F32), cos, cos, pad], axis=1)
    sin_t = jnp.concatenate([jnp.zeros((S, MLA_NOPE), F32), sin, sin, pad], axis=1)
    return cos_t, sin_t


def _lane_vec(vals, start):
    return jnp.zeros((1, LANES), F32).at[0, start:start + vals.shape[0]].set(vals)


def even_mixer_layer(h, B, S, attn_norm, w_in, q_norm, kv_norm, w_uq, w_ukv, conv_w, a_log, dt_bias,
                     gdn_norm, w_out, tables):
    w_even, wq, wqr, wk, wv = _even_weights(w_in, w_uq, w_ukv)
    zm, zg = rms_matmul(h, attn_norm, w_even, (896, 2048))
    cos_t, sin_t = tables
    q, k, v = mla_prep(zm, q_norm.reshape(1, -1), kv_norm.reshape(1, -1), wq, wqr, wk, wv, cos_t, sin_t, S)
    o_mla = mla_flash(q, k, v, B, S, tq=min(512, S))
    qkv, gb, gbt = gdn_prep(zg, zm, conv_w, _lane_vec(a_log, DECAY_LANE), _lane_vec(dt_bias, DECAY_LANE), S)
    o_gdn = gdn_chunk(qkv, gb, gbt, zg, gdn_norm.reshape(1, -1), B, S)
    nm = MLA_HEADS * MLA_V
    return proj_residual([o_mla, o_gdn], [w_out[:nm].astype(BF16), w_out[nm:].astype(BF16)], h)


def _compress_weights(pe, w1, w2):
    G, D = NSA_GROUPS, NSA_DIM
    eye = jnp.eye(G, dtype=F32)
    w1r = w1.reshape(CMP_BLOCK, D, CMP_HIDDEN)

    def expand(wpart):
        return jnp.einsum('ldh,gk->lgdkh', wpart, eye).reshape(CMP_STRIDE * G * D, G * CMP_HIDDEN).astype(BF16)

    def pe_vec(p):
        return jnp.broadcast_to(p[:, None, :], (CMP_STRIDE, G, D)).reshape(1, CMP_STRIDE * G * D)

    w2e = jnp.einsum('hd,gk->ghkd', w2, eye).reshape(G * CMP_HIDDEN, G * D).astype(BF16)
    return (pe_vec(pe[:CMP_STRIDE]), pe_vec(pe[CMP_STRIDE:]), expand(w1r[:CMP_STRIDE]),
            expand(w1r[CMP_STRIDE:]), w2e)


def _overlap_matrix(S):
    nr = S // CMP_STRIDE
    n_sel = S // SEL_BLOCK
    n = np.arange(nr)[:, None]
    j = np.arange(n_sel)[None, :]
    start = n * CMP_STRIDE
    ov = (start <= j * SEL_BLOCK + SEL_BLOCK - 1) & (start + CMP_BLOCK - 1 >= j * SEL_BLOCK)
    ov = ov & (n < nr - 1)
    return jnp.asarray(ov.astype(np.float32))


def odd_mixer_layer(h, B, S, attn_norm, w_in, pe_k, w1_k, w2_k, pe_v, w1_v, w2_v, w_out):
    D = w_in.shape[0]
    n_g = 3 * NSA_HEADS
    w_odd = jnp.concatenate([w_in, jnp.zeros((D, LANES - n_g), F32)], axis=1).astype(BF16)
    kvw = NSA_GROUPS * NSA_DIM
    q, kc, vc, ks, vs, kw, vw, zg = rms_matmul(h, attn_norm, w_odd, (NSA_HEADS * NSA_DIM,) + (kvw,) * 6 + (LANES,))
    k_cmp = nsa_compress(kc, *_compress_weights(pe_k, w1_k, w2_k), B, S)
    v_cmp = nsa_compress(vc, *_compress_weights(pe_v, w1_v, w2_v), B, S)
    o1, sel = nsa_cmp(q, k_cmp, v_cmp, zg, _overlap_matrix(S), B, S)
    o2 = nsa_sel(q, ks, vs, sel, zg, o1, B, S)
    o3 = nsa_win(q, kw, vw, zg, o2, B, S)
    return proj_residual([o3], [w_out.astype(BF16)], h)


def ffn_layer(h, g, wg, wu, wd):
    return ffn(h, g, wg.astype(BF16), wu.astype(BF16), wd.astype(BF16))


def kernel(x, ev_attn_norm, ev_w_in, ev_q_norm, ev_kv_norm, ev_w_uq, ev_w_ukv, ev_conv_w, ev_a_log, ev_dt_bias, ev_gdn_norm, ev_w_out, od_attn_norm, od_w_in, od_pe_k, od_w1_k, od_w2_k, od_pe_v, od_w1_v, od_w2_v, od_w_out, ffn_norm, ffn_w_gate, ffn_w_up, ffn_w_down, final_norm):
    B, S, D = x.shape
    depth = ffn_norm.shape[0]
    h = x.reshape(B * S, D)
    tables = _rope_tables(S)
    for layer in range(depth):
        i = layer // 2
        if layer % 2 == 0:
            h = even_mixer_layer(h, B, S, ev_attn_norm[i], ev_w_in[i], ev_q_norm[i], ev_kv_norm[i], ev_w_uq[i],
                                 ev_w_ukv[i], ev_conv_w[i], ev_a_log[i], ev_dt_bias[i], ev_gdn_norm[i],
                                 ev_w_out[i], tables)
        else:
            h = odd_mixer_layer(h, B, S, od_attn_norm[i], od_w_in[i], od_pe_k[i], od_w1_k[i], od_w2_k[i],
                                od_pe_v[i], od_w1_v[i], od_w2_v[i], od_w_out[i])
        h = ffn_layer(h, ffn_norm[layer], ffn_w_gate[layer], ffn_w_up[layer], ffn_w_down[layer])
    return rmsnorm_call(h, final_norm).reshape(B, S, D)
```

```python
import functools
import math

import jax
import jax.numpy as jnp
import numpy as np
from jax import lax
from jax.experimental import pallas as pl
from jax.experimental.pallas import tpu as pltpu

F32 = jnp.float32
BF16 = jnp.bfloat16

EPS = 1e-6
NEG = -1e30
BIG = 1e30
LANES = 128

MLA_HEADS = 8
MLA_Q_RANK = 384
MLA_KV_RANK = 256
MLA_NOPE = 64
MLA_ROPE = 32
MLA_V = 64
ROPE_BASE = 10000.0
GDN_HEADS = 4
GDN_DK = 128
GDN_DV = 128
GDN_CONV = 4
GDN_CHUNK = 64
NSA_HEADS = 16
NSA_GROUPS = 4
NSA_HPG = 4
NSA_DIM = 64
CMP_BLOCK = 32
CMP_STRIDE = 16
CMP_HIDDEN = 256
SEL_BLOCK = 64
SEL_TOPN = 16
WINDOW = 512

VMEM_LIMIT = 56 * 1024 * 1024
HIGHEST = lax.Precision.HIGHEST


def _cparams(sem):
    return pltpu.CompilerParams(dimension_semantics=sem, vmem_limit_bytes=VMEM_LIMIT)


def _dot(a, b):
    return jnp.dot(a.astype(BF16), b.astype(BF16), preferred_element_type=F32)


def _dot_nt(a, b):
    return lax.dot_general(a.astype(BF16), b.astype(BF16), (((1,), (1,)), ((), ())),
                           preferred_element_type=F32)


def _dot_tn(a, b):
    return lax.dot_general(a.astype(BF16), b.astype(BF16), (((0,), (0,)), ((), ())),
                           preferred_element_type=F32)


def _dot_f32(a, b):
    return jnp.dot(a, b, preferred_element_type=F32, precision=HIGHEST)


def _rms(x, g):
    var = jnp.mean(x * x, axis=-1, keepdims=True)
    return x * lax.rsqrt(var + EPS) * g


def _silu(x):
    return x * (1.0 / (1.0 + jnp.exp(-x)))


def _sigmoid(x):
    return 1.0 / (1.0 + jnp.exp(-x))


def _rms_matmul_kernel(x_ref, g_ref, w_ref, *out_refs, splits):
    xn = _rms(x_ref[...], g_ref[...])
    acc = _dot(xn, w_ref[...])
    off = 0
    for o_ref, n in zip(out_refs, splits):
        o_ref[...] = acc[:, off:off + n].astype(o_ref.dtype)
        off += n


def rms_matmul(x, g, w, splits, tm=256, dtypes=None):
    T, K = x.shape
    N = w.shape[1]
    assert sum(splits) == N and T % tm == 0
    dtypes = dtypes or (F32,) * len(splits)
    return pl.pallas_call(
        functools.partial(_rms_matmul_kernel, splits=splits),
        grid=(T // tm,),
        in_specs=[pl.BlockSpec((tm, K), lambda i: (i, 0)),
                  pl.BlockSpec((1, K), lambda i: (0, 0)),
                  pl.BlockSpec((K, N), lambda i: (0, 0))],
        out_specs=[pl.BlockSpec((tm, n), lambda i: (i, 0)) for n in splits],
        out_shape=[jax.ShapeDtypeStruct((T, n), dt) for n, dt in zip(splits, dtypes)],
        compiler_params=_cparams(("parallel",)),
        name="rms_matmul",
    )(x, g.reshape(1, K), w)


def _proj_residual_kernel(*refs, n_in):
    a_refs = refs[:n_in]
    w_refs = refs[n_in:2 * n_in]
    res_ref = refs[2 * n_in]
    o_ref = refs[2 * n_in + 1]
    acc = res_ref[...]
    for a_ref, w_ref in zip(a_refs, w_refs):
        acc = acc + _dot(a_ref[...], w_ref[...])
    o_ref[...] = acc


def proj_residual(a_list, w_list, res, tm=512):
    T, N = res.shape
    n_in = len(a_list)
    in_specs = [pl.BlockSpec((tm, a.shape[1]), lambda i: (i, 0)) for a in a_list]
    in_specs += [pl.BlockSpec(w.shape, lambda i: (0, 0)) for w in w_list]
    in_specs += [pl.BlockSpec((tm, N), lambda i: (i, 0))]
    return pl.pallas_call(
        functools.partial(_proj_residual_kernel, n_in=n_in),
        grid=(T // tm,),
        in_specs=in_specs,
        out_specs=pl.BlockSpec((tm, N), lambda i: (i, 0)),
        out_shape=jax.ShapeDtypeStruct((T, N), F32),
        compiler_params=_cparams(("parallel",)),
        name="proj_residual",
    )(*a_list, *w_list, res)


def _ffn_kernel(h_ref, g_ref, wg_ref, wu_ref, wd_ref, o_ref, *, chunks):
    h = h_ref[...]
    xn = _rms(h, g_ref[...]).astype(BF16)
    acc = h
    off = 0
    for n in chunks:
        gate = jnp.dot(xn, wg_ref[:, off:off + n], preferred_element_type=F32)
        up = jnp.dot(xn, wu_ref[:, off:off + n], preferred_element_type=F32)
        act = (_silu(gate) * up).astype(BF16)
        acc = acc + jnp.dot(act, wd_ref[off:off + n, :], preferred_element_type=F32)
        off += n
    o_ref[...] = acc


def ffn(h, g, wg, wu, wd, tm=512):
    T, D = h.shape
    Hd = wg.shape[1]
    nch = 2 if (Hd % 256 == 0) else 1
    chunks = (Hd // nch,) * nch
    single = pl.Buffered(1)
    return pl.pallas_call(
        functools.partial(_ffn_kernel, chunks=chunks),
        grid=(T // tm,),
        in_specs=[pl.BlockSpec((tm, D), lambda i: (i, 0)),
                  pl.BlockSpec((1, D), lambda i: (0, 0)),
                  pl.BlockSpec((D, Hd), lambda i: (0, 0), pipeline_mode=single),
                  pl.BlockSpec((D, Hd), lambda i: (0, 0), pipeline_mode=single),
                  pl.BlockSpec((Hd, D), lambda i: (0, 0), pipeline_mode=single)],
        out_specs=pl.BlockSpec((tm, D), lambda i: (i, 0)),
        out_shape=jax.ShapeDtypeStruct((T, D), F32),
        compiler_params=_cparams(("parallel",)),
        name="ffn",
    )(h, g.reshape(1, D), wg, wu, wd)


def _rmsnorm_kernel(x_ref, g_ref, o_ref):
    o_ref[...] = _rms(x_ref[...], g_ref[...])


def rmsnorm_call(x, g, tm=1024):
    T, D = x.shape
    return pl.pallas_call(
        _rmsnorm_kernel,
        grid=(T // tm,),
        in_specs=[pl.BlockSpec((tm, D), lambda i: (i, 0)),
                  pl.BlockSpec((1, D), lambda i: (0, 0))],
        out_specs=pl.BlockSpec((tm, D), lambda i: (i, 0)),
        out_shape=jax.ShapeDtypeStruct((T, D), F32),
        compiler_params=_cparams(("parallel",)),
        name="final_norm",
    )(x, g.reshape(1, D))


MLA_HB = 128


def _mla_prep_kernel(zm_ref, qn_ref, kvn_ref, wq_ref, wqr_ref, wk_ref, wv_ref, c_ref, s_ref,
                     q_out, k_out, v_out):
    zm = zm_ref[...]
    cq = zm[:, :MLA_Q_RANK]
    ckv = zm[:, MLA_Q_RANK:MLA_Q_RANK + MLA_KV_RANK]
    m1 = zm[:, 640:768]
    m2 = zm[:, 768:896]
    cqn = _rms(cq, qn_ref[...]).astype(BF16)
    ckvn = _rms(ckv, kvn_ref[...]).astype(BF16)
    q = jnp.dot(cqn, wq_ref[...], preferred_element_type=F32)
    qr = jnp.dot(cqn, wqr_ref[...], preferred_element_type=F32)
    kn = jnp.dot(ckvn, wk_ref[...], preferred_element_type=F32)
    cos = c_ref[...]
    sin = s_ref[...]
    lane = lax.broadcasted_iota(jnp.int32, cos.shape, 1)
    rope_lane = (lane >= MLA_NOPE) & (lane < MLA_NOPE + MLA_ROPE)
    krot = jnp.where(rope_lane, m1 * cos + m2 * sin, 0.0)
    scale = (MLA_NOPE + MLA_ROPE) ** -0.5
    for h in range(MLA_HEADS):
        sl = slice(h * MLA_HB, (h + 1) * MLA_HB)
        q_out[:, sl] = ((q[:, sl] * cos + qr[:, sl] * sin) * scale).astype(BF16)
        k_out[:, sl] = (kn[:, sl] + krot).astype(BF16)
    v_out[...] = jnp.dot(ckvn, wv_ref[...], preferred_element_type=F32).astype(BF16)


def mla_prep(zm, qn, kvn, wq, wqr, wk, wv, cos_t, sin_t, S, tm=512):
    T = zm.shape[0]
    nsb = S // tm
    HW = MLA_HEADS * MLA_HB
    full = lambda a: pl.BlockSpec(a.shape, lambda i: (0, 0))
    return pl.pallas_call(
        _mla_prep_kernel,
        grid=(T // tm,),
        in_specs=[pl.BlockSpec((tm, zm.shape[1]), lambda i: (i, 0)),
                  full(qn), full(kvn), full(wq), full(wqr), full(wk), full(wv),
                  pl.BlockSpec((tm, LANES), lambda i: (i % nsb, 0)),
                  pl.BlockSpec((tm, LANES), lambda i: (i % nsb, 0))],
        out_specs=[pl.BlockSpec((tm, HW), lambda i: (i, 0)),
                   pl.BlockSpec((tm, HW), lambda i: (i, 0)),
                   pl.BlockSpec((tm, MLA_HEADS * MLA_V), lambda i: (i, 0))],
        out_shape=[jax.ShapeDtypeStruct((T, HW), BF16),
                   jax.ShapeDtypeStruct((T, HW), BF16),
                   jax.ShapeDtypeStruct((T, MLA_HEADS * MLA_V), BF16)],
        compiler_params=_cparams(("parallel",)),
        name="mla_prep",
    )(zm, qn, kvn, wq, wqr, wk, wv, cos_t, sin_t)


def _mla_flash_kernel(qi_ref, kj_ref, q_ref, k_ref, v_ref, o_ref, m_scr, l_scr, acc_scr, *, tq, tk):
    n = pl.program_id(2)
    i = qi_ref[n]
    j = kj_ref[n]

    @pl.when(j == 0)
    def _():
        m_scr[...] = jnp.full(m_scr.shape, NEG, F32)
        l_scr[...] = jnp.zeros(l_scr.shape, F32)
        acc_scr[...] = jnp.zeros(acc_scr.shape, F32)

    qpos = i * tq + lax.broadcasted_iota(jnp.int32, (tq, tk), 0)
    kpos = j * tk + lax.broadcasted_iota(jnp.int32, (tq, tk), 1)
    mask = kpos <= qpos
    v = v_ref[...]
    for hh in range(2):
        sl = slice(hh * MLA_HB, (hh + 1) * MLA_HB)
        s = lax.dot_general(q_ref[:, sl], k_ref[:, sl], (((1,), (1,)), ((), ())),
                            preferred_element_type=F32)
        s = jnp.where(mask, s, NEG)
        m_prev = m_scr[hh]
        m_new = jnp.maximum(m_prev, jnp.max(s, axis=1, keepdims=True))
        alpha = jnp.exp(m_prev - m_new)
        p = jnp.exp(s - m_new[:, :1])
        l_scr[hh] = alpha * l_scr[hh] + jnp.sum(p, axis=1, keepdims=True)
        acc_scr[hh] = alpha * acc_scr[hh] + jnp.dot(p.astype(BF16), v, preferred_element_type=F32)
        m_scr[hh] = m_new

    @pl.when(j == i)
    def _():
        o0 = acc_scr[0] * (1.0 / l_scr[0])
        o1 = acc_scr[1] * (1.0 / l_scr[1])
        lane = lax.broadcasted_iota(jnp.int32, o0.shape, 1)
        o_ref[...] = jnp.where(lane < MLA_V, o0, o1)


def _tri_pairs(nq, ratio=1):
    qi, kj = [], []
    for i in range(nq):
        for j in range(i // ratio + 1):
            qi.append(i)
            kj.append(j)
    return jnp.asarray(qi, jnp.int32), jnp.asarray(kj, jnp.int32)


def mla_flash(q, k, v, B, S, tq=512):
    T = q.shape[0]
    tk = tq
    nq = S // tq
    qi, kj = _tri_pairs(nq)
    npairs = int(qi.shape[0])
    grid_spec = pltpu.PrefetchScalarGridSpec(
        num_scalar_prefetch=2,
        grid=(B, MLA_HEADS // 2, npairs),
        in_specs=[pl.BlockSpec((tq, 2 * MLA_HB), lambda b, p, n, qi, kj: (b * nq + qi[n], p)),
                  pl.BlockSpec((tk, 2 * MLA_HB), lambda b, p, n, qi, kj: (b * nq + kj[n], p)),
                  pl.BlockSpec((tk, 2 * MLA_V), lambda b, p, n, qi, kj: (b * nq + kj[n], p))],
        out_specs=pl.BlockSpec((tq, 2 * MLA_V), lambda b, p, n, qi, kj: (b * nq + qi[n], p)),
        scratch_shapes=[pltpu.VMEM((2, tq, LANES), F32),
                        pltpu.VMEM((2, tq, LANES), F32),
                        pltpu.VMEM((2, tq, LANES), F32)],
    )
    return pl.pallas_call(
        functools.partial(_mla_flash_kernel, tq=tq, tk=tk),
        grid_spec=grid_spec,
        out_shape=jax.ShapeDtypeStruct((T, MLA_HEADS * MLA_V), F32),
        compiler_params=_cparams(("parallel", "parallel", "arbitrary")),
        name="mla_flash",
    )(qi, kj, q, k, v)


GDN_QKV = 3 * GDN_HEADS * GDN_DK
BETA_LANE = 96
DECAY_LANE = 100
HALO = 8


def _gdn_prep_kernel(z_ref, halo_ref, m1_ref, cw_ref, alog_ref, dt_ref, qkv_out, gb_out, gbt_out,
                     *, tm, tiles_per_seq):
    i = pl.program_id(0)
    x = z_ref[...]
    halo = halo_ref[...]
    halo = jnp.where(i % tiles_per_seq == 0, jnp.zeros_like(halo), halo)
    xe = jnp.concatenate([halo, x], axis=0)
    cw = cw_ref[...]
    acc = x * cw[GDN_CONV - 1:GDN_CONV, :]
    for d in range(1, GDN_CONV):
        acc = acc + xe[HALO - d:HALO - d + tm, :] * cw[GDN_CONV - 1 - d:GDN_CONV - d, :]
    y = _silu(acc)
    nq = GDN_HEADS * GDN_DK
    for h in range(GDN_HEADS):
        sl = slice(h * GDN_DK, (h + 1) * GDN_DK)
        qh = y[:, sl]
        qkv_out[:, sl] = qh * lax.rsqrt(jnp.sum(qh * qh, axis=-1, keepdims=True) + EPS) * (GDN_DK ** -0.5)
        sl2 = slice(nq + h * GDN_DK, nq + (h + 1) * GDN_DK)
        kh = y[:, sl2]
        qkv_out[:, sl2] = kh * lax.rsqrt(jnp.sum(kh * kh, axis=-1, keepdims=True) + EPS)
    qkv_out[:, 2 * nq:] = y[:, 2 * nq:]
    m1 = m1_ref[...]
    lane = lax.broadcasted_iota(jnp.int32, m1.shape, 1)
    beta = _sigmoid(m1)
    xa = m1 + dt_ref[...]
    softplus = jnp.maximum(xa, 0.0) + jnp.log(1.0 + jnp.exp(-jnp.abs(xa)))
    decay = -jnp.exp(alog_ref[...]) * softplus
    ri = lax.broadcasted_iota(jnp.int32, (tm, tm), 0)
    ci = lax.broadcasted_iota(jnp.int32, (tm, tm), 1)
    ltri = jnp.where((ri >= ci) & (ri // GDN_CHUNK == ci // GDN_CHUNK), 1.0, 0.0).astype(BF16)
    d_hi = decay.astype(BF16)
    rem = decay - d_hi.astype(F32)
    d_mid = rem.astype(BF16)
    d_lo = (rem - d_mid.astype(F32)).astype(BF16)
    gcum = (jnp.dot(ltri, d_hi, preferred_element_type=F32) + jnp.dot(ltri, d_mid, preferred_element_type=F32)
            + jnp.dot(ltri, d_lo, preferred_element_type=F32))
    gb = jnp.where(lane < DECAY_LANE, beta, gcum)
    gb_out[...] = gb
    r = lax.broadcasted_iota(jnp.int32, (8, LANES), 0)
    c = lax.broadcasted_iota(jnp.int32, (8, LANES), 1)
    pick = (c == r + BETA_LANE).astype(F32)
    gbt_out[...] = lax.dot_general(pick, gb, (((1,), (1,)), ((), ())),
                                   preferred_element_type=F32, precision=HIGHEST)


def gdn_prep(zg, zm, conv_w, alog_v, dt_v, S, tm=512):
    T = zg.shape[0]
    tiles_per_seq = S // tm
    hb = tm // HALO
    m1_blk = 640 // LANES
    return pl.pallas_call(
        functools.partial(_gdn_prep_kernel, tm=tm, tiles_per_seq=tiles_per_seq),
        grid=(T // tm,),
        in_specs=[pl.BlockSpec((tm, GDN_QKV), lambda i: (i, 0)),
                  pl.BlockSpec((HALO, GDN_QKV), lambda i: (jnp.maximum(i * hb - 1, 0), 0)),
                  pl.BlockSpec((tm, LANES), lambda i: (i, m1_blk)),
                  pl.BlockSpec((GDN_CONV, GDN_QKV), lambda i: (0, 0)),
                  pl.BlockSpec((1, LANES), lambda i: (0, 0)),
                  pl.BlockSpec((1, LANES), lambda i: (0, 0))],
        out_specs=[pl.BlockSpec((tm, GDN_QKV), lambda i: (i, 0)),
                   pl.BlockSpec((tm, LANES), lambda i: (i, 0)),
                   pl.BlockSpec((8, tm), lambda i: (0, i))],
        out_shape=[jax.ShapeDtypeStruct((T, GDN_QKV), F32),
                   jax.ShapeDtypeStruct((T, LANES), F32),
                   jax.ShapeDtypeStruct((8, T), F32)],
        compiler_params=_cparams(("parallel",)),
        name="gdn_prep",
    )(zg, zg, zm, conv_w, alog_v, dt_v)


def _dot3(a, b):
    a_hi = a.astype(BF16)
    a_lo = (a - a_hi.astype(F32)).astype(BF16)
    b_hi = b.astype(BF16)
    b_lo = (b - b_hi.astype(F32)).astype(BF16)
    return (jnp.dot(a_hi, b_hi, preferred_element_type=F32) + jnp.dot(a_hi, b_lo, preferred_element_type=F32)
            + jnp.dot(a_lo, b_hi, preferred_element_type=F32))


def _tri_inverse_all(a_list, eye, diag_blocks):
    ad = [jnp.where(diag_blocks, a, 0.0) for a in a_list]
    ao = [a - d for a, d in zip(a_list, ad)]
    a2 = [_dot(d, d) for d in ad]
    a4 = [_dot(x, x) for x in a2]
    a8 = [_dot(x, x) for x in a4]
    t = [_dot(eye - d, eye + x) for d, x in zip(ad, a2)]
    t = [_dot(y, eye + x) for y, x in zip(t, a4)]
    dinv = [_dot(y, eye + x) for y, x in zip(t, a8)]
    n = [_dot(d, o) for d, o in zip(dinv, ao)]
    n2 = [_dot(x, x) for x in n]
    t = [_dot(eye - x, eye + y) for x, y in zip(n, n2)]
    x0 = [_dot(y, d) for y, d in zip(t, dinv)]
    res = [eye - x - _dot3(a, x) for a, x in zip(a_list, x0)]
    return [x + _dot(x, r) for x, r in zip(x0, res)]


def _gdn_chunk_kernel(qkv_ref, gb_ref, gbt_ref, zg_ref, norm_ref, o_ref, state_scr, *, lb):
    C = GDN_CHUNK
    DK = GDN_DK
    H = GDN_HEADS
    nq = H * DK
    n_chunks = lb // C

    @pl.when(pl.program_id(1) == 0)
    def _():
        state_scr[...] = jnp.zeros(state_scr.shape, F32)

    ii = lax.broadcasted_iota(jnp.int32, (C, C), 0)
    jj = lax.broadcasted_iota(jnp.int32, (C, C), 1)
    lower = ii >= jj
    strict = ii > jj
    eye = (ii == jj).astype(F32)
    diag_blocks = (ii // 16) == (jj // 16)
    norm_w = norm_ref[...]

    items = [(c, h) for c in range(n_chunks) for h in range(H)]
    rows = lambda c: slice(c * C, (c + 1) * C)
    q = [qkv_ref[rows(c), h * DK:(h + 1) * DK] for c, h in items]
    k = [qkv_ref[rows(c), nq + h * DK:nq + (h + 1) * DK] for c, h in items]
    v = [qkv_ref[rows(c), 2 * nq + h * DK:2 * nq + (h + 1) * DK] for c, h in items]
    beta = [jnp.broadcast_to(gb_ref[rows(c), BETA_LANE + h:BETA_LANE + h + 1], (C, DK)) for c, h in items]
    gc = [jnp.broadcast_to(gb_ref[rows(c), DECAY_LANE + h:DECAY_LANE + h + 1], (C, DK)) for c, h in items]
    gr = [jnp.broadcast_to(gbt_ref[4 + h:5 + h, rows(c)], (C, C)) for c, h in items]
    decay = [jnp.exp(jnp.where(lower, x[:, :C] - y, NEG)) for x, y in zip(gc, gr)]
    eg = [jnp.exp(x) for x in gc]
    kb = [x * b for x, b in zip(k, beta)]
    a = [jnp.where(strict, _dot_nt(x, y) * d, 0.0) for x, y, d in zip(kb, k, decay)]
    t_inv = _tri_inverse_all(a, eye, diag_blocks)
    uw = [_dot(t, jnp.concatenate([x * b, y * e], axis=1))
          for t, x, b, y, e in zip(t_inv, v, beta, kb, eg)]
    intra = [_dot_nt(x, y) * d for x, y, d in zip(q, k, decay)]
    g_last = [x[C - 1:C, :] for x in gc]
    k_dec = [x * jnp.exp(gl - g) for x, gl, g in zip(k, g_last, gc)]
    wq = [jnp.concatenate([x[:, DK:], y * e], axis=0) for x, y, e in zip(uw, q, eg)]

    states = [state_scr[h] for h in range(H)]
    for c in range(n_chunks):
        idx = [c * H + h for h in range(H)]
        ws = [_dot(wq[i], states[h]) for h, i in enumerate(idx)]
        v_new = [uw[i][:, :DK] - y[:C] for i, y in zip(idx, ws)]
        o = [y[C:] + _dot(intra[i], vn) for i, y, vn in zip(idx, ws, v_new)]
        states = [s * jnp.exp(g_last[i]) + _dot_tn(k_dec[i], vn) for s, i, vn in zip(states, idx, v_new)]
        for h in range(H):
            on = o[h] * lax.rsqrt(jnp.mean(o[h] * o[h], axis=-1, keepdims=True) + EPS) * norm_w
            o_ref[rows(c), h * DK:(h + 1) * DK] = on * _silu(zg_ref[rows(c), h * DK:(h + 1) * DK])
    for h in range(H):
        state_scr[h] = states[h]


def gdn_chunk(qkv, gb, gbt, zg, norm_w, B, S, lb=256):
    T = qkv.shape[0]
    nsb = S // lb
    VW = GDN_HEADS * GDN_DV
    zg_blk = GDN_QKV // VW
    return pl.pallas_call(
        functools.partial(_gdn_chunk_kernel, lb=lb),
        grid=(B, nsb),
        in_specs=[pl.BlockSpec((lb, GDN_QKV), lambda b, s: (b * nsb + s, 0)),
                  pl.BlockSpec((lb, LANES), lambda b, s: (b * nsb + s, 0)),
                  pl.BlockSpec((8, lb), lambda b, s: (0, b * nsb + s)),
                  pl.BlockSpec((lb, VW), lambda b, s: (b * nsb + s, zg_blk)),
                  pl.BlockSpec((1, GDN_DV), lambda b, s: (0, 0))],
        out_specs=pl.BlockSpec((lb, VW), lambda b, s: (b * nsb + s, 0)),
        out_shape=jax.ShapeDtypeStruct((T, VW), F32),
        scratch_shapes=[pltpu.VMEM((GDN_HEADS, GDN_DK, GDN_DV), F32)],
        compiler_params=_cparams(("parallel", "arbitrary")),
        name="gdn_chunk",
    )(qkv, gb, gbt, zg, norm_w)


def _compress_kernel(x_ref, pea_ref, peb_ref, w1a_ref, w1b_ref, w2_ref, o_ref, p1_scr, p2_scr):
    kc = pl.program_id(1)

    @pl.when(kc == 0)
    def _():
        p1_scr[...] = jnp.zeros(p1_scr.shape, F32)
        p2_scr[...] = jnp.zeros(p2_scr.shape, F32)

    x = x_ref[0]
    p1_scr[...] += _dot(x + pea_ref[...], w1a_ref[...])
    p2_scr[...] += _dot(x + peb_ref[...], w1b_ref[...])

    @pl.when(kc == pl.num_programs(1) - 1)
    def _():
        p2 = p2_scr[...]
        rows = p2.shape[0]
        hid = p1_scr[...] + pltpu.roll(p2, rows - 1, 0)
        c0 = math.sqrt(2.0 / math.pi)
        act = 0.5 * hid * (1.0 + jnp.tanh(c0 * (hid + 0.044715 * (hid * hid * hid))))
        o_ref[0] = _dot(act, w2_ref[...])


def nsa_compress(x, pe_a, pe_b, w1a, w1b, w2, B, S, kchunk=512):
    nr = S // CMP_STRIDE
    xr = x.reshape(B, nr, CMP_STRIDE * NSA_GROUPS * NSA_DIM)
    KW = xr.shape[2]
    HW = NSA_GROUPS * CMP_HIDDEN
    OW = NSA_GROUPS * NSA_DIM
    return pl.pallas_call(
        _compress_kernel,
        grid=(B, KW // kchunk),
        in_specs=[pl.BlockSpec((1, nr, kchunk), lambda b, k: (b, 0, k)),
                  pl.BlockSpec((1, kchunk), lambda b, k: (0, k)),
                  pl.BlockSpec((1, kchunk), lambda b, k: (0, k)),
                  pl.BlockSpec((kchunk, HW), lambda b, k: (k, 0)),
                  pl.BlockSpec((kchunk, HW), lambda b, k: (k, 0)),
                  pl.BlockSpec((HW, OW), lambda b, k: (0, 0))],
        out_specs=pl.BlockSpec((1, nr, OW), lambda b, k: (b, 0, 0)),
        out_shape=jax.ShapeDtypeStruct((B, nr, OW), F32),
        scratch_shapes=[pltpu.VMEM((nr, HW), F32), pltpu.VMEM((nr, HW), F32)],
        compiler_params=_cparams(("parallel", "arbitrary")),
        name="nsa_compress",
    )(xr, pe_a, pe_b, w1a, w1b, w2)


def _slope(h):
    return float(2.0 ** (-8.0 * (h + 1) / NSA_HEADS))


SEL_TK = 2 * SEL_BLOCK
SEL_NT = 4


def _nsa_cmp_kernel(q_ref, kc_ref, vc_ref, zg_ref, ov_ref, o_ref, sel_ref, flag_ref, *, tq, n_sel):
    i = pl.program_id(1)
    n_tiles = n_sel * SEL_BLOCK // SEL_TK
    blk_tile = (lax.broadcasted_iota(jnp.int32, (n_sel, n_tiles), 0) * SEL_BLOCK // SEL_TK
                == lax.broadcasted_iota(jnp.int32, (n_sel, n_tiles), 1))
    to_tile = jnp.where(blk_tile, 1.0, 0.0).astype(BF16)
    ncmp = kc_ref.shape[1]
    D = NSA_DIM
    scale = D ** -0.5
    qpos = i * tq + lax.broadcasted_iota(jnp.int32, (tq, 1), 0)
    nidx = lax.broadcasted_iota(jnp.int32, (1, ncmp), 1)
    valid = (nidx * CMP_STRIDE + (CMP_BLOCK - 1)) <= qpos
    any_valid = (qpos >= CMP_BLOCK - 1).astype(F32)
    dist = qpos.astype(F32) - (nidx.astype(F32) * CMP_STRIDE + 0.5 * (CMP_BLOCK - 1))
    gates = _sigmoid(zg_ref[...])
    kc = kc_ref[0].astype(BF16)
    vc = vc_ref[0].astype(BF16)
    jf = lax.broadcasted_iota(jnp.int32, (1, n_sel), 1).astype(F32)
    qblk = (qpos // SEL_BLOCK).astype(F32)
    forced = (jf == 0.0) | (jf == qblk) | (jf == qblk - 1.0)
    causal_blk = jf <= qblk
    for g in range(NSA_GROUPS):
        kg = kc[:, g * D:(g + 1) * D]
        vg = vc[:, g * D:(g + 1) * D]
        psum = jnp.zeros((tq, ncmp), F32)
        for r in range(NSA_HPG):
            h = g * NSA_HPG + r
            qh = q_ref[:, h * D:(h + 1) * D]
            s = _dot_nt(qh, kg) * scale - _slope(h) * dist
            s = jnp.where(valid, s, NEG)
            e = jnp.exp(s - jnp.max(s, axis=1, keepdims=True))
            p = e * (any_valid / jnp.sum(e, axis=1, keepdims=True))
            psum = psum + p
            o_ref[:, h * D:(h + 1) * D] = gates[:, 3 * h:3 * h + 1] * _dot(p, vg)
        imp = _dot_f32(psum, ov_ref[...])
        work = jnp.where(forced, BIG, jnp.where(causal_blk, imp, NEG))
        selm = jnp.zeros((tq, n_sel), F32)
        for _ in range(min(SEL_TOPN, n_sel)):
            mx = jnp.max(work, axis=1, keepdims=True)
            first = jnp.min(jnp.where(work == mx, jf, float(n_sel)), axis=1, keepdims=True)
            pick = jf == first
            selm = jnp.where(pick, 1.0, selm)
            work = jnp.where(pick, -jnp.inf, work)
        selb = selm.astype(BF16)
        sel_ref[:, g * n_sel:(g + 1) * n_sel] = selb
        hits = jnp.max(jnp.dot(selb, to_tile, preferred_element_type=F32), axis=0, keepdims=True)
        flag_ref[0, g:g + 1, :] = (hits > 0.5).astype(jnp.int32)


def nsa_cmp(q, k_cmp, v_cmp, zg, overlap, B, S, tq=256):
    T = q.shape[0]
    nq = S // tq
    n_sel = S // SEL_BLOCK
    ncmp = k_cmp.shape[1]
    QW = NSA_HEADS * NSA_DIM
    KW = NSA_GROUPS * NSA_DIM
    return pl.pallas_call(
        functools.partial(_nsa_cmp_kernel, tq=tq, n_sel=n_sel),
        grid=(B, nq),
        in_specs=[pl.BlockSpec((tq, QW), lambda b, i: (b * nq + i, 0)),
                  pl.BlockSpec((1, ncmp, KW), lambda b, i: (b, 0, 0)),
                  pl.BlockSpec((1, ncmp, KW), lambda b, i: (b, 0, 0)),
                  pl.BlockSpec((tq, LANES), lambda b, i: (b * nq + i, 0)),
                  pl.BlockSpec((ncmp, n_sel), lambda b, i: (0, 0))],
        out_specs=[pl.BlockSpec((tq, QW), lambda b, i: (b * nq + i, 0)),
                   pl.BlockSpec((tq, NSA_GROUPS * n_sel), lambda b, i: (b * nq + i, 0)),
                   pl.BlockSpec((1, NSA_GROUPS, S // SEL_TK), lambda b, i: (b * nq + i, 0, 0))],
        out_shape=[jax.ShapeDtypeStruct((T, QW), F32),
                   jax.ShapeDtypeStruct((T, NSA_GROUPS * n_sel), BF16),
                   jax.ShapeDtypeStruct((B * nq, NSA_GROUPS, S // SEL_TK), jnp.int32)],
        compiler_params=_cparams(("parallel", "parallel")),
        name="nsa_cmp",
    )(q, k_cmp, v_cmp, zg, overlap)


def _nsa_sel_kernel(flags_ref, q_ref, k_ref, v_ref, sel_ref, zg_ref, prev_ref, o_ref,
                    list_smem, m_scr, l_scr, acc_scr, *, tq, n_sel, nq):
    b = pl.program_id(0)
    i = pl.program_id(1)
    D = NSA_DIM
    scale = D ** -0.5
    n_tiles = n_sel * SEL_BLOCK // SEL_TK
    n_causal = (i * tq + tq - 1) // SEL_TK + 1
    ks_w = SEL_NT * SEL_TK
    qpos = i * tq + lax.broadcasted_iota(jnp.int32, (tq, 1), 0)
    lane_t = lax.broadcasted_iota(jnp.int32, (1, SEL_TK), 1)
    blk_iota = lax.broadcasted_iota(jnp.int32, (n_sel, ks_w), 0)
    gates = _sigmoid(zg_ref[...])

    for g in range(NSA_GROUPS):
        base = ((b * nq + i) * NSA_GROUPS + g) * n_tiles

        def scan(j, n, base=base):
            list_smem[n] = j
            return n + (flags_ref[base + j] != 0).astype(jnp.int32)

        count = lax.fori_loop(0, n_causal, scan, jnp.int32(0))

        for r in range(NSA_HPG):
            m_scr[r] = jnp.full(m_scr.shape[1:], NEG, F32)
            l_scr[r] = jnp.zeros(l_scr.shape[1:], F32)
            acc_scr[r] = jnp.zeros(acc_scr.shape[1:], F32)
        qs = [(q_ref[:, (g * NSA_HPG + r) * D:(g * NSA_HPG + r + 1) * D] * scale).astype(BF16)
              for r in range(NSA_HPG)]
        sel_g = sel_ref[:, g * n_sel:(g + 1) * n_sel]

        def step(st, carry, g=g, count=count, qs=qs, sel_g=sel_g):
            k_parts, v_parts, kpos_parts, kblk_parts = [], [], [], []
            for s in range(SEL_NT):
                idx = st * SEL_NT + s
                j = list_smem[jnp.minimum(idx, count - 1)]
                start = pl.multiple_of(j * SEL_TK, SEL_TK)
                k_parts.append(k_ref[pl.ds(start, SEL_TK), g * D:(g + 1) * D])
                v_parts.append(v_ref[pl.ds(start, SEL_TK), g * D:(g + 1) * D])
                tid = jnp.where(idx < count, j, -1)
                kpos_parts.append(tid * SEL_TK + lane_t)
                kblk_parts.append(tid * (SEL_TK // SEL_BLOCK) + lane_t // SEL_BLOCK)
            k = jnp.concatenate(k_parts, axis=0)
            v = jnp.concatenate(v_parts, axis=0)
            kpos = jnp.concatenate(kpos_parts, axis=1)
            kblk = jnp.concatenate(kblk_parts, axis=1)
            expand = jnp.where(blk_iota == kblk, 1.0, 0.0).astype(BF16)
            picked = jnp.dot(sel_g, expand, preferred_element_type=F32)
            dist_i = qpos - kpos
            allowed = jnp.where(dist_i >= 0, picked, 0.0) > 0.5
            dist = dist_i.astype(F32)
            for r in range(NSA_HPG):
                s_ = _dot_nt(qs[r], k) - _slope(g * NSA_HPG + r) * dist
                s_ = jnp.where(allowed, s_, NEG)
                m_prev = m_scr[r]
                m_new = jnp.maximum(m_prev, jnp.max(s_, axis=1, keepdims=True))
                alpha = jnp.exp(m_prev - m_new)
                p = jnp.where(allowed, jnp.exp(s_ - m_new[:, :1]), 0.0)
                l_scr[r] = alpha * l_scr[r] + jnp.sum(p, axis=1, keepdims=True)
                acc_scr[r] = alpha[:, :D] * acc_scr[r] + _dot(p, v)
                m_scr[r] = m_new
            return carry

        lax.fori_loop(0, (count + SEL_NT - 1) // SEL_NT, step, jnp.int32(0))

        for r in range(NSA_HPG):
            h = g * NSA_HPG + r
            o_h = acc_scr[r] * (1.0 / l_scr[r][:, :D])
            o_ref[:, h * D:(h + 1) * D] = (prev_ref[:, h * D:(h + 1) * D]
                                           + gates[:, 3 * h + 1:3 * h + 2] * o_h)


def nsa_sel(q, ks, vs, sel, flags, zg, prev, B, S, tq=256):
    T = q.shape[0]
    nq = S // tq
    n_sel = S // SEL_BLOCK
    QW = NSA_HEADS * NSA_DIM
    KW = NSA_GROUPS * NSA_DIM
    qmap = lambda b, i, fl: (b * nq + i, 0)
    kmap = lambda b, i, fl: (b, 0)
    grid_spec = pltpu.PrefetchScalarGridSpec(
        num_scalar_prefetch=1,
        grid=(B, nq),
        in_specs=[pl.BlockSpec((tq, QW), qmap),
                  pl.BlockSpec((S, KW), kmap),
                  pl.BlockSpec((S, KW), kmap),
                  pl.BlockSpec((tq, NSA_GROUPS * n_sel), qmap),
                  pl.BlockSpec((tq, LANES), qmap),
                  pl.BlockSpec((tq, QW), qmap)],
        out_specs=pl.BlockSpec((tq, QW), qmap),
        scratch_shapes=[pltpu.SMEM((S // SEL_TK,), jnp.int32),
                        pltpu.VMEM((NSA_HPG, tq, LANES), F32),
                        pltpu.VMEM((NSA_HPG, tq, LANES), F32),
                        pltpu.VMEM((NSA_HPG, tq, NSA_DIM), F32)],
    )
    return pl.pallas_call(
        functools.partial(_nsa_sel_kernel, tq=tq, n_sel=n_sel, nq=nq),
        grid_spec=grid_spec,
        out_shape=jax.ShapeDtypeStruct((T, QW), F32),
        compiler_params=_cparams(("parallel", "arbitrary")),
        name="nsa_sel",
    )(flags.reshape(-1), q, ks, vs, sel, zg, prev)


def _nsa_win_kernel(q_ref, k0_ref, k1_ref, k2_ref, v0_ref, v1_ref, v2_ref, zg_ref, prev_ref, o_ref, *, tq):
    i = pl.program_id(1)
    D = NSA_DIM
    scale = D ** -0.5
    nback = WINDOW // tq
    tkw = (nback + 1) * tq
    k = jnp.concatenate([k0_ref[...], k1_ref[...], k2_ref[...]], axis=0).astype(BF16)
    v = jnp.concatenate([v0_ref[...], v1_ref[...], v2_ref[...]], axis=0).astype(BF16)
    qpos = i * tq + lax.broadcasted_iota(jnp.int32, (tq, tkw), 0)
    kpos = (i - nback) * tq + lax.broadcasted_iota(jnp.int32, (tq, tkw), 1)
    dw = qpos - kpos
    wvalid = (dw >= 0) & (dw < WINDOW) & (kpos >= 0)
    dist = dw.astype(F32)
    gates = _sigmoid(zg_ref[...])
    for g in range(NSA_GROUPS):
        kg = k[:, g * D:(g + 1) * D]
        vg = v[:, g * D:(g + 1) * D]
        for r in range(NSA_HPG):
            h = g * NSA_HPG + r
            qh = q_ref[:, h * D:(h + 1) * D]
            s = _dot_nt(qh, kg) * scale - _slope(h) * dist
            s = jnp.where(wvalid, s, NEG)
            e = jnp.exp(s - jnp.max(s, axis=1, keepdims=True))
            p = e * (1.0 / jnp.sum(e, axis=1, keepdims=True))
            o_ref[:, h * D:(h + 1) * D] = (prev_ref[:, h * D:(h + 1) * D]
                                           + gates[:, 3 * h + 2:3 * h + 3] * _dot(p, vg))


def nsa_win(q, kw, vw, zg, prev, B, S, tq=256):
    T = q.shape[0]
    assert WINDOW % tq == 0 and WINDOW // tq == 2
    nq = S // tq
    QW = NSA_HEADS * NSA_DIM
    KW = NSA_GROUPS * NSA_DIM
    qmap = lambda b, i: (b * nq + i, 0)
    back = lambda d: (lambda b, i: (b * nq + jnp.maximum(i - d, 0), 0))
    kspecs = [pl.BlockSpec((tq, KW), back(2)), pl.BlockSpec((tq, KW), back(1)), pl.BlockSpec((tq, KW), back(0))]
    return pl.pallas_call(
        functools.partial(_nsa_win_kernel, tq=tq),
        grid=(B, nq),
        in_specs=[pl.BlockSpec((tq, QW), qmap)] + kspecs + kspecs
                 + [pl.BlockSpec((tq, LANES), qmap), pl.BlockSpec((tq, QW), qmap)],
        out_specs=pl.BlockSpec((tq, QW), qmap),
        out_shape=jax.ShapeDtypeStruct((T, QW), F32),
        compiler_params=_cparams(("parallel", "parallel")),
        name="nsa_win",
    )(q, kw, kw, kw, vw, vw, vw, zg, prev)


def _rot_half_cols(w):
    half = w.shape[-1] // 2
    return jnp.concatenate([-w[..., half:], w[..., :half]], axis=-1)


def _even_weights(w_in, w_uq, w_ukv):
    D = w_in.shape[0]
    o = 0
    cuts = {}
    for name, n in (("cq", MLA_Q_RANK), ("ckv", MLA_KV_RANK), ("kr", MLA_ROPE), ("zq", 512), ("zk", 512),
                    ("zv", 512), ("zg", 512), ("zb", GDN_HEADS), ("za", GDN_HEADS)):
        cuts[name] = w_in[:, o:o + n]
        o += n
    z = lambda n: jnp.zeros((D, n), F32)
    misc1 = jnp.concatenate([z(MLA_NOPE), cuts["kr"], cuts["zb"], cuts["za"],
                             z(LANES - MLA_NOPE - MLA_ROPE - 2 * GDN_HEADS)], axis=1)
    misc2 = jnp.concatenate([z(MLA_NOPE), _rot_half_cols(cuts["kr"]), z(LANES - MLA_NOPE - MLA_ROPE)], axis=1)
    w_even = jnp.concatenate([cuts["cq"], cuts["ckv"], misc1, misc2,
                              cuts["zq"], cuts["zk"], cuts["zv"], cuts["zg"]], axis=1).astype(BF16)
    qd = MLA_NOPE + MLA_ROPE
    wq3 = w_uq.reshape(MLA_Q_RANK, MLA_HEADS, qd)
    zq = jnp.zeros((MLA_Q_RANK, MLA_HEADS, MLA_HB - qd), F32)
    wq = jnp.concatenate([wq3, zq], axis=2).reshape(MLA_Q_RANK, MLA_HEADS * MLA_HB).astype(BF16)
    wqr = jnp.concatenate([jnp.zeros((MLA_Q_RANK, MLA_HEADS, MLA_NOPE), F32),
                           _rot_half_cols(wq3[:, :, MLA_NOPE:]), zq], axis=2)
    wqr = wqr.reshape(MLA_Q_RANK, MLA_HEADS * MLA_HB).astype(BF16)
    wkv3 = w_ukv.reshape(MLA_KV_RANK, MLA_HEADS, MLA_NOPE + MLA_V)
    wk = jnp.concatenate([wkv3[:, :, :MLA_NOPE], jnp.zeros((MLA_KV_RANK, MLA_HEADS, MLA_HB - MLA_NOPE), F32)],
                         axis=2).reshape(MLA_KV_RANK, MLA_HEADS * MLA_HB).astype(BF16)
    wv = wkv3[:, :, MLA_NOPE:].reshape(MLA_KV_RANK, MLA_HEADS * MLA_V).astype(BF16)
    return w_even, wq, wqr, wk, wv


def _rope_tables(S):
    half = MLA_ROPE // 2
    inv = ROPE_BASE ** (-jnp.arange(half, dtype=F32) / half)
    ang = jnp.arange(S, dtype=F32)[:, None] * inv[None, :]
    cos = jnp.cos(ang)
    sin = jnp.sin(ang)
    pad = jnp.zeros((S, LANES - MLA_NOPE - MLA_ROPE), F32)
    cos_t = jnp.concatenate([jnp.ones((S, MLA_NOPE), F32), cos, cos, pad], axis=1)
    sin_t = jnp.concatenate([jnp.zeros((S, MLA_NOPE), F32), sin, sin, pad], axis=1)
    return cos_t, sin_t


def _lane_vec(vals, start):
    return jnp.zeros((1, LANES), F32).at[0, start:start + vals.shape[0]].set(vals)


def even_mixer_layer(h, B, S, attn_norm, w_in, q_norm, kv_norm, w_uq, w_ukv, conv_w, a_log, dt_bias,
                     gdn_norm, w_out, tables):
    w_even, wq, wqr, wk, wv = _even_weights(w_in, w_uq, w_ukv)
    zm, zg = rms_matmul(h, attn_norm, w_even, (896, 2048))
    cos_t, sin_t = tables
    q, k, v = mla_prep(zm, q_norm.reshape(1, -1), kv_norm.reshape(1, -1), wq, wqr, wk, wv, cos_t, sin_t, S)
    o_mla = mla_flash(q, k, v, B, S, tq=min(512, S))
    qkv, gb, gbt = gdn_prep(zg, zm, conv_w, _lane_vec(a_log, DECAY_LANE), _lane_vec(dt_bias, DECAY_LANE), S)
    o_gdn = gdn_chunk(qkv, gb, gbt, zg, gdn_norm.reshape(1, -1), B, S)
    nm = MLA_HEADS * MLA_V
    return proj_residual([o_mla, o_gdn], [w_out[:nm].astype(BF16), w_out[nm:].astype(BF16)], h)


def _compress_weights(pe, w1, w2):
    G, D = NSA_GROUPS, NSA_DIM
    eye = jnp.eye(G, dtype=F32)
    w1r = w1.reshape(CMP_BLOCK, D, CMP_HIDDEN)

    def expand(wpart):
        return jnp.einsum('ldh,gk->lgdkh', wpart, eye).reshape(CMP_STRIDE * G * D, G * CMP_HIDDEN).astype(BF16)

    def pe_vec(p):
        return jnp.broadcast_to(p[:, None, :], (CMP_STRIDE, G, D)).reshape(1, CMP_STRIDE * G * D)

    w2e = jnp.einsum('hd,gk->ghkd', w2, eye).reshape(G * CMP_HIDDEN, G * D).astype(BF16)
    return (pe_vec(pe[:CMP_STRIDE]), pe_vec(pe[CMP_STRIDE:]), expand(w1r[:CMP_STRIDE]),
            expand(w1r[CMP_STRIDE:]), w2e)


def _overlap_matrix(S):
    nr = S // CMP_STRIDE
    n_sel = S // SEL_BLOCK
    n = np.arange(nr)[:, None]
    j = np.arange(n_sel)[None, :]
    start = n * CMP_STRIDE
    ov = (start <= j * SEL_BLOCK + SEL_BLOCK - 1) & (start + CMP_BLOCK - 1 >= j * SEL_BLOCK)
    ov = ov & (n < nr - 1)
    return jnp.asarray(ov.astype(np.float32))


def odd_mixer_layer(h, B, S, attn_norm, w_in, pe_k, w1_k, w2_k, pe_v, w1_v, w2_v, w_out):
    D = w_in.shape[0]
    n_g = 3 * NSA_HEADS
    w_odd = jnp.concatenate([w_in, jnp.zeros((D, LANES - n_g), F32)], axis=1).astype(BF16)
    kvw = NSA_GROUPS * NSA_DIM
    q, kc, vc, ks, vs, kw, vw, zg = rms_matmul(
        h, attn_norm, w_odd, (NSA_HEADS * NSA_DIM,) + (kvw,) * 6 + (LANES,),
        dtypes=(F32, F32, F32, BF16, BF16, F32, F32, F32))
    k_cmp = nsa_compress(kc, *_compress_weights(pe_k, w1_k, w2_k), B, S)
    v_cmp = nsa_compress(vc, *_compress_weights(pe_v, w1_v, w2_v), B, S)
    o1, sel, flags = nsa_cmp(q, k_cmp, v_cmp, zg, _overlap_matrix(S), B, S)
    o2 = nsa_sel(q, ks, vs, sel, flags, zg, o1, B, S)
    o3 = nsa_win(q, kw, vw, zg, o2, B, S)
    return proj_residual([o3], [w_out.astype(BF16)], h)


def ffn_layer(h, g, wg, wu, wd):
    return ffn(h, g, wg.astype(BF16), wu.astype(BF16), wd.astype(BF16))


def kernel(x, ev_attn_norm, ev_w_in, ev_q_norm, ev_kv_norm, ev_w_uq, ev_w_ukv, ev_conv_w, ev_a_log, ev_dt_bias, ev_gdn_norm, ev_w_out, od_attn_norm, od_w_in, od_pe_k, od_w1_k, od_w2_k, od_pe_v, od_w1_v, od_w2_v, od_w_out, ffn_norm, ffn_w_gate, ffn_w_up, ffn_w_down, final_norm):
    B, S, D = x.shape
    depth = ffn_norm.shape[0]
    h = x.reshape(B * S, D)
    tables = _rope_tables(S)
    for layer in range(depth):
        i = layer // 2
        if layer % 2 == 0:
            h = even_mixer_layer(h, B, S, ev_attn_norm[i], ev_w_in[i], ev_q_norm[i], ev_kv_norm[i], ev_w_uq[i],
                                 ev_w_ukv[i], ev_conv_w[i], ev_a_log[i], ev_dt_bias[i], ev_gdn_norm[i],
                                 ev_w_out[i], tables)
        else:
            h = odd_mixer_layer(h, B, S, od_attn_norm[i], od_w_in[i], od_pe_k[i], od_w1_k[i], od_w2_k[i],
                                od_pe_v[i], od_w1_v[i], od_w2_v[i], od_w_out[i])
        h = ffn_layer(h, ffn_norm[layer], ffn_w_gate[layer], ffn_w_up[layer], ffn_w_down[layer])
    return rmsnorm_call(h, final_norm).reshape(B, S, D)
```

```python
import functools
import math

import jax
import jax.numpy as jnp
import numpy as np
from jax import lax
from jax.experimental import pallas as pl
from jax.experimental.pallas import tpu as pltpu

F32 = jnp.float32
BF16 = jnp.bfloat16

EPS = 1e-6
NEG = -1e30
BIG = 1e30
LANES = 128

MLA_HEADS = 8
MLA_Q_RANK = 384
MLA_KV_RANK = 256
MLA_NOPE = 64
MLA_ROPE = 32
MLA_V = 64
ROPE_BASE = 10000.0
GDN_HEADS = 4
GDN_DK = 128
GDN_DV = 128
GDN_CONV = 4
GDN_CHUNK = 64
NSA_HEADS = 16
NSA_GROUPS = 4
NSA_HPG = 4
NSA_DIM = 64
CMP_BLOCK = 32
CMP_STRIDE = 16
CMP_HIDDEN = 256
SEL_BLOCK = 64
SEL_TOPN = 16
WINDOW = 512

LOG2E = math.log2(math.e)
VMEM_LIMIT = 56 * 1024 * 1024
HIGHEST = lax.Precision.HIGHEST


def _cparams(sem):
    return pltpu.CompilerParams(dimension_semantics=sem, vmem_limit_bytes=VMEM_LIMIT)


def _dot(a, b):
    return jnp.dot(a.astype(BF16), b.astype(BF16), preferred_element_type=F32)


def _dot_nt(a, b):
    return lax.dot_general(a.astype(BF16), b.astype(BF16), (((1,), (1,)), ((), ())),
                           preferred_element_type=F32)


def _dot_tn(a, b):
    return lax.dot_general(a.astype(BF16), b.astype(BF16), (((0,), (0,)), ((), ())),
                           preferred_element_type=F32)


def _dot_f32(a, b):
    return jnp.dot(a, b, preferred_element_type=F32, precision=HIGHEST)


def _rms(x, g):
    var = jnp.mean(x * x, axis=-1, keepdims=True)
    return x * lax.rsqrt(var + EPS) * g


def _silu(x):
    return x * (1.0 / (1.0 + jnp.exp(-x)))


def _sigmoid(x):
    return 1.0 / (1.0 + jnp.exp(-x))


def _rms_matmul_kernel(x_ref, g_ref, w_ref, *out_refs, splits):
    xn = _rms(x_ref[...], g_ref[...])
    acc = _dot(xn, w_ref[...])
    off = 0
    for o_ref, n in zip(out_refs, splits):
        o_ref[...] = acc[:, off:off + n].astype(o_ref.dtype)
        off += n


def rms_matmul(x, g, w, splits, tm=256, dtypes=None):
    T, K = x.shape
    N = w.shape[1]
    assert sum(splits) == N and T % tm == 0
    dtypes = dtypes or (F32,) * len(splits)
    return pl.pallas_call(
        functools.partial(_rms_matmul_kernel, splits=splits),
        grid=(T // tm,),
        in_specs=[pl.BlockSpec((tm, K), lambda i: (i, 0)),
                  pl.BlockSpec((1, K), lambda i: (0, 0)),
                  pl.BlockSpec((K, N), lambda i: (0, 0))],
        out_specs=[pl.BlockSpec((tm, n), lambda i: (i, 0)) for n in splits],
        out_shape=[jax.ShapeDtypeStruct((T, n), dt) for n, dt in zip(splits, dtypes)],
        compiler_params=_cparams(("parallel",)),
        name="rms_matmul",
    )(x, g.reshape(1, K), w)


def _proj_residual_kernel(*refs, n_in):
    a_refs = refs[:n_in]
    w_refs = refs[n_in:2 * n_in]
    res_ref = refs[2 * n_in]
    o_ref = refs[2 * n_in + 1]
    acc = res_ref[...]
    for a_ref, w_ref in zip(a_refs, w_refs):
        acc = acc + _dot(a_ref[...], w_ref[...])
    o_ref[...] = acc


def proj_residual(a_list, w_list, res, tm=512):
    T, N = res.shape
    n_in = len(a_list)
    in_specs = [pl.BlockSpec((tm, a.shape[1]), lambda i: (i, 0)) for a in a_list]
    in_specs += [pl.BlockSpec(w.shape, lambda i: (0, 0)) for w in w_list]
    in_specs += [pl.BlockSpec((tm, N), lambda i: (i, 0))]
    return pl.pallas_call(
        functools.partial(_proj_residual_kernel, n_in=n_in),
        grid=(T // tm,),
        in_specs=in_specs,
        out_specs=pl.BlockSpec((tm, N), lambda i: (i, 0)),
        out_shape=jax.ShapeDtypeStruct((T, N), F32),
        compiler_params=_cparams(("parallel",)),
        name="proj_residual",
    )(*a_list, *w_list, res)


def _ffn_kernel(h_ref, g_ref, wg_ref, wu_ref, wd_ref, o_ref, *, chunks):
    h = h_ref[...]
    xn = _rms(h, g_ref[...]).astype(BF16)
    acc = h
    off = 0
    for n in chunks:
        gate = jnp.dot(xn, wg_ref[:, off:off + n], preferred_element_type=F32)
        up = jnp.dot(xn, wu_ref[:, off:off + n], preferred_element_type=F32)
        act = (_silu(gate) * up).astype(BF16)
        acc = acc + jnp.dot(act, wd_ref[off:off + n, :], preferred_element_type=F32)
        off += n
    o_ref[...] = acc


def ffn(h, g, wg, wu, wd, tm=512):
    T, D = h.shape
    Hd = wg.shape[1]
    nch = 2 if (Hd % 256 == 0) else 1
    chunks = (Hd // nch,) * nch
    single = pl.Buffered(1)
    return pl.pallas_call(
        functools.partial(_ffn_kernel, chunks=chunks),
        grid=(T // tm,),
        in_specs=[pl.BlockSpec((tm, D), lambda i: (i, 0)),
                  pl.BlockSpec((1, D), lambda i: (0, 0)),
                  pl.BlockSpec((D, Hd), lambda i: (0, 0), pipeline_mode=single),
                  pl.BlockSpec((D, Hd), lambda i: (0, 0), pipeline_mode=single),
                  pl.BlockSpec((Hd, D), lambda i: (0, 0), pipeline_mode=single)],
        out_specs=pl.BlockSpec((tm, D), lambda i: (i, 0)),
        out_shape=jax.ShapeDtypeStruct((T, D), F32),
        compiler_params=_cparams(("parallel",)),
        name="ffn",
    )(h, g.reshape(1, D), wg, wu, wd)


def _rmsnorm_kernel(x_ref, g_ref, o_ref):
    o_ref[...] = _rms(x_ref[...], g_ref[...])


def rmsnorm_call(x, g, tm=1024):
    T, D = x.shape
    return pl.pallas_call(
        _rmsnorm_kernel,
        grid=(T // tm,),
        in_specs=[pl.BlockSpec((tm, D), lambda i: (i, 0)),
                  pl.BlockSpec((1, D), lambda i: (0, 0))],
        out_specs=pl.BlockSpec((tm, D), lambda i: (i, 0)),
        out_shape=jax.ShapeDtypeStruct((T, D), F32),
        compiler_params=_cparams(("parallel",)),
        name="final_norm",
    )(x, g.reshape(1, D))


MLA_HB = 128
MLA_FLASH_HEADS = 4


def _mla_prep_kernel(zm_ref, qn_ref, kvn_ref, wq_ref, wqr_ref, wk_ref, wv_ref, c_ref, s_ref,
                     q_out, k_out, v_out):
    zm = zm_ref[...]
    cq = zm[:, :MLA_Q_RANK]
    ckv = zm[:, MLA_Q_RANK:MLA_Q_RANK + MLA_KV_RANK]
    m1 = zm[:, 640:768]
    m2 = zm[:, 768:896]
    cqn = _rms(cq, qn_ref[...]).astype(BF16)
    ckvn = _rms(ckv, kvn_ref[...]).astype(BF16)
    q = jnp.dot(cqn, wq_ref[...], preferred_element_type=F32)
    qr = jnp.dot(cqn, wqr_ref[...], preferred_element_type=F32)
    kn = jnp.dot(ckvn, wk_ref[...], preferred_element_type=F32)
    cos = c_ref[...]
    sin = s_ref[...]
    lane = lax.broadcasted_iota(jnp.int32, cos.shape, 1)
    rope_lane = (lane >= MLA_NOPE) & (lane < MLA_NOPE + MLA_ROPE)
    krot = jnp.where(rope_lane, m1 * cos + m2 * sin, 0.0)
    scale = (MLA_NOPE + MLA_ROPE) ** -0.5 * LOG2E
    for h in range(MLA_HEADS):
        sl = slice(h * MLA_HB, (h + 1) * MLA_HB)
        q_out[:, sl] = ((q[:, sl] * cos + qr[:, sl] * sin) * scale).astype(BF16)
        k_out[:, sl] = (kn[:, sl] + krot).astype(BF16)
    vlane = lax.broadcasted_iota(jnp.int32, (1, MLA_HEADS * LANES), 1)
    ones_half = ((vlane // LANES) % 2 == 0) == ((vlane % LANES) >= MLA_V)
    v = jnp.dot(ckvn, wv_ref[...], preferred_element_type=F32)
    v_out[...] = jnp.where(ones_half, 1.0, v).astype(BF16)


def mla_prep(zm, qn, kvn, wq, wqr, wk, wv, cos_t, sin_t, S, tm=512):
    T = zm.shape[0]
    nsb = S // tm
    HW = MLA_HEADS * MLA_HB
    full = lambda a: pl.BlockSpec(a.shape, lambda i: (0, 0))
    return pl.pallas_call(
        _mla_prep_kernel,
        grid=(T // tm,),
        in_specs=[pl.BlockSpec((tm, zm.shape[1]), lambda i: (i, 0)),
                  full(qn), full(kvn), full(wq), full(wqr), full(wk), full(wv),
                  pl.BlockSpec((tm, LANES), lambda i: (i % nsb, 0)),
                  pl.BlockSpec((tm, LANES), lambda i: (i % nsb, 0))],
        out_specs=[pl.BlockSpec((tm, HW), lambda i: (i, 0)),
                   pl.BlockSpec((tm, HW), lambda i: (i, 0)),
                   pl.BlockSpec((tm, HW), lambda i: (i, 0))],
        out_shape=[jax.ShapeDtypeStruct((T, HW), BF16),
                   jax.ShapeDtypeStruct((T, HW), BF16),
                   jax.ShapeDtypeStruct((T, HW), BF16)],
        compiler_params=_cparams(("parallel",)),
        name="mla_prep",
    )(zm, qn, kvn, wq, wqr, wk, wv, cos_t, sin_t)


def _mla_flash_kernel(qi_ref, kj_ref, q_ref, k_ref, v_ref, o_ref, m_scr, acc_scr, *, tq, tk):
    n = pl.program_id(2)
    i = qi_ref[n]
    j = kj_ref[n]

    @pl.when(j == 0)
    def _():
        m_scr[...] = jnp.full(m_scr.shape, NEG, F32)
        acc_scr[...] = jnp.zeros(acc_scr.shape, F32)

    heads = range(MLA_FLASH_HEADS)
    lane_tiles = range(tk // LANES)

    def update(masked):
        s = [lax.dot_general(q_ref[:, h * MLA_HB:(h + 1) * MLA_HB], k_ref[:, h * MLA_HB:(h + 1) * MLA_HB],
                             (((1,), (1,)), ((), ())), preferred_element_type=F32) for h in heads]
        if masked:
            qpos = lax.broadcasted_iota(jnp.int32, (tq, tk), 0)
            kpos = lax.broadcasted_iota(jnp.int32, (tq, tk), 1)
            mask = kpos <= qpos
            s = [jnp.where(mask, x, NEG) for x in s]
        new_m = []
        pv = []
        for h in heads:
            tiles = [s[h][:, c * LANES:(c + 1) * LANES] for c in lane_tiles]
            tile_max = functools.reduce(jnp.maximum, tiles)
            m_new = jnp.maximum(m_scr[h], jnp.max(tile_max, axis=1, keepdims=True))
            p = jnp.concatenate([jnp.exp2(t - m_new) for t in tiles], axis=1).astype(BF16)
            pv.append(jnp.dot(p, v_ref[:, h * LANES:(h + 1) * LANES], preferred_element_type=F32))
            new_m.append(m_new)
        for h in heads:
            acc_scr[h] = jnp.exp2(m_scr[h] - new_m[h]) * acc_scr[h] + pv[h]
            m_scr[h] = new_m[h]

    @pl.when(j < i)
    def _():
        update(False)

    @pl.when(j == i)
    def _():
        update(True)
        lane = lax.broadcasted_iota(jnp.int32, (tq, LANES), 1)
        for pr in range(MLA_FLASH_HEADS // 2):
            a0 = acc_scr[2 * pr]
            a1 = acc_scr[2 * pr + 1]
            o0 = a0 * (1.0 / pltpu.roll(a0, MLA_V, 1))
            o1 = a1 * (1.0 / pltpu.roll(a1, MLA_V, 1))
            o_ref[:, pr * LANES:(pr + 1) * LANES] = jnp.where(lane < MLA_V, o0, o1)


def _tri_pairs(nq, ratio=1):
    qi, kj = [], []
    for i in range(nq):
        for j in range(i // ratio + 1):
            qi.append(i)
            kj.append(j)
    return jnp.asarray(qi, jnp.int32), jnp.asarray(kj, jnp.int32)


def mla_flash(q, k, v, B, S, tq=512):
    T = q.shape[0]
    tk = tq
    nq = S // tq
    qi, kj = _tri_pairs(nq)
    npairs = int(qi.shape[0])
    nh = MLA_FLASH_HEADS
    grid_spec = pltpu.PrefetchScalarGridSpec(
        num_scalar_prefetch=2,
        grid=(B, MLA_HEADS // nh, npairs),
        in_specs=[pl.BlockSpec((tq, nh * MLA_HB), lambda b, p, n, qi, kj: (b * nq + qi[n], p)),
                  pl.BlockSpec((tk, nh * MLA_HB), lambda b, p, n, qi, kj: (b * nq + kj[n], p)),
                  pl.BlockSpec((tk, nh * LANES), lambda b, p, n, qi, kj: (b * nq + kj[n], p))],
        out_specs=pl.BlockSpec((tq, nh * MLA_V), lambda b, p, n, qi, kj: (b * nq + qi[n], p)),
        scratch_shapes=[pltpu.VMEM((nh, tq, LANES), F32),
                        pltpu.VMEM((nh, tq, LANES), F32)],
    )
    return pl.pallas_call(
        functools.partial(_mla_flash_kernel, tq=tq, tk=tk),
        grid_spec=grid_spec,
        out_shape=jax.ShapeDtypeStruct((T, MLA_HEADS * MLA_V), F32),
        compiler_params=_cparams(("parallel", "parallel", "arbitrary")),
        name="mla_flash",
    )(qi, kj, q, k, v)


GDN_QKV = 3 * GDN_HEADS * GDN_DK
BETA_LANE = 96
DECAY_LANE = 100
HALO = 8


def _gdn_prep_kernel(z_ref, halo_ref, m1_ref, cw_ref, alog_ref, dt_ref, qkv_out, gb_out, gbt_out,
                     *, tm, tiles_per_seq):
    i = pl.program_id(0)
    x = z_ref[...]
    halo = halo_ref[...]
    halo = jnp.where(i % tiles_per_seq == 0, jnp.zeros_like(halo), halo)
    xe = jnp.concatenate([halo, x], axis=0)
    cw = cw_ref[...]
    acc = x * cw[GDN_CONV - 1:GDN_CONV, :]
    for d in range(1, GDN_CONV):
        acc = acc + xe[HALO - d:HALO - d + tm, :] * cw[GDN_CONV - 1 - d:GDN_CONV - d, :]
    y = _silu(acc)
    nq = GDN_HEADS * GDN_DK
    for h in range(GDN_HEADS):
        sl = slice(h * GDN_DK, (h + 1) * GDN_DK)
        qh = y[:, sl]
        qkv_out[:, sl] = qh * lax.rsqrt(jnp.sum(qh * qh, axis=-1, keepdims=True) + EPS) * (GDN_DK ** -0.5)
        sl2 = slice(nq + h * GDN_DK, nq + (h + 1) * GDN_DK)
        kh = y[:, sl2]
        qkv_out[:, sl2] = kh * lax.rsqrt(jnp.sum(kh * kh, axis=-1, keepdims=True) + EPS)
    qkv_out[:, 2 * nq:] = y[:, 2 * nq:]
    m1 = m1_ref[...]
    lane = lax.broadcasted_iota(jnp.int32, m1.shape, 1)
    beta = _sigmoid(m1)
    xa = m1 + dt_ref[...]
    softplus = jnp.maximum(xa, 0.0) + jnp.log(1.0 + jnp.exp(-jnp.abs(xa)))
    decay = -jnp.exp(alog_ref[...]) * softplus
    ri = lax.broadcasted_iota(jnp.int32, (tm, tm), 0)
    ci = lax.broadcasted_iota(jnp.int32, (tm, tm), 1)
    ltri = jnp.where((ri >= ci) & (ri // GDN_CHUNK == ci // GDN_CHUNK), 1.0, 0.0).astype(BF16)
    d_hi = decay.astype(BF16)
    rem = decay - d_hi.astype(F32)
    d_mid = rem.astype(BF16)
    d_lo = (rem - d_mid.astype(F32)).astype(BF16)
    gcum = (jnp.dot(ltri, d_hi, preferred_element_type=F32) + jnp.dot(ltri, d_mid, preferred_element_type=F32)
            + jnp.dot(ltri, d_lo, preferred_element_type=F32))
    gb = jnp.where(lane < DECAY_LANE, beta, gcum)
    gb_out[...] = gb
    r = lax.broadcasted_iota(jnp.int32, (8, LANES), 0)
    c = lax.broadcasted_iota(jnp.int32, (8, LANES), 1)
    pick = (c == r + BETA_LANE).astype(F32)
    gbt_out[...] = lax.dot_general(pick, gb, (((1,), (1,)), ((), ())),
                                   preferred_element_type=F32, precision=HIGHEST)


def gdn_prep(zg, zm, conv_w, alog_v, dt_v, S, tm=512):
    T = zg.shape[0]
    tiles_per_seq = S // tm
    hb = tm // HALO
    m1_blk = 640 // LANES
    return pl.pallas_call(
        functools.partial(_gdn_prep_kernel, tm=tm, tiles_per_seq=tiles_per_seq),
        grid=(T // tm,),
        in_specs=[pl.BlockSpec((tm, GDN_QKV), lambda i: (i, 0)),
                  pl.BlockSpec((HALO, GDN_QKV), lambda i: (jnp.maximum(i * hb - 1, 0), 0)),
                  pl.BlockSpec((tm, LANES), lambda i: (i, m1_blk)),
                  pl.BlockSpec((GDN_CONV, GDN_QKV), lambda i: (0, 0)),
                  pl.BlockSpec((1, LANES), lambda i: (0, 0)),
                  pl.BlockSpec((1, LANES), lambda i: (0, 0))],
        out_specs=[pl.BlockSpec((tm, GDN_QKV), lambda i: (i, 0)),
                   pl.BlockSpec((tm, LANES), lambda i: (i, 0)),
                   pl.BlockSpec((8, tm), lambda i: (0, i))],
        out_shape=[jax.ShapeDtypeStruct((T, GDN_QKV), F32),
                   jax.ShapeDtypeStruct((T, LANES), F32),
                   jax.ShapeDtypeStruct((8, T), F32)],
        compiler_params=_cparams(("parallel",)),
        name="gdn_prep",
    )(zg, zg, zm, conv_w, alog_v, dt_v)


def _dot3(a, b):
    a_hi = a.astype(BF16)
    a_lo = (a - a_hi.astype(F32)).astype(BF16)
    b_hi = b.astype(BF16)
    b_lo = (b - b_hi.astype(F32)).astype(BF16)
    return (jnp.dot(a_hi, b_hi, preferred_element_type=F32) + jnp.dot(a_hi, b_lo, preferred_element_type=F32)
            + jnp.dot(a_lo, b_hi, preferred_element_type=F32))


def _tri_inverse_all(a_list, eye, diag_blocks):
    ad = [jnp.where(diag_blocks, a, 0.0) for a in a_list]
    ao = [a - d for a, d in zip(a_list, ad)]
    a2 = [_dot(d, d) for d in ad]
    a4 = [_dot(x, x) for x in a2]
    a8 = [_dot(x, x) for x in a4]
    t = [_dot(eye - d, eye + x) for d, x in zip(ad, a2)]
    t = [_dot(y, eye + x) for y, x in zip(t, a4)]
    dinv = [_dot(y, eye + x) for y, x in zip(t, a8)]
    n = [_dot(d, o) for d, o in zip(dinv, ao)]
    n2 = [_dot(x, x) for x in n]
    t = [_dot(eye - x, eye + y) for x, y in zip(n, n2)]
    x0 = [_dot(y, d) for y, d in zip(t, dinv)]
    res = [eye - x - _dot3(a, x) for a, x in zip(a_list, x0)]
    return [x + _dot(x, r) for x, r in zip(x0, res)]


def _gdn_chunk_kernel(qkv_ref, gb_ref, gbt_ref, zg_ref, norm_ref, o_ref, state_scr, *, lb):
    C = GDN_CHUNK
    DK = GDN_DK
    H = GDN_HEADS
    nq = H * DK
    n_chunks = lb // C

    @pl.when(pl.program_id(1) == 0)
    def _():
        state_scr[...] = jnp.zeros(state_scr.shape, F32)

    ii = lax.broadcasted_iota(jnp.int32, (C, C), 0)
    jj = lax.broadcasted_iota(jnp.int32, (C, C), 1)
    lower = ii >= jj
    strict = ii > jj
    eye = (ii == jj).astype(F32)
    diag_blocks = (ii // 16) == (jj // 16)
    norm_w = norm_ref[...]

    items = [(c, h) for c in range(n_chunks) for h in range(H)]
    rows = lambda c: slice(c * C, (c + 1) * C)
    q = [qkv_ref[rows(c), h * DK:(h + 1) * DK] for c, h in items]
    k = [qkv_ref[rows(c), nq + h * DK:nq + (h + 1) * DK] for c, h in items]
    v = [qkv_ref[rows(c), 2 * nq + h * DK:2 * nq + (h + 1) * DK] for c, h in items]
    beta = [jnp.broadcast_to(gb_ref[rows(c), BETA_LANE + h:BETA_LANE + h + 1], (C, DK)) for c, h in items]
    gc = [jnp.broadcast_to(gb_ref[rows(c), DECAY_LANE + h:DECAY_LANE + h + 1], (C, DK)) for c, h in items]
    gr = [jnp.broadcast_to(gbt_ref[4 + h:5 + h, rows(c)], (C, C)) for c, h in items]
    decay = [jnp.exp(jnp.where(lower, x[:, :C] - y, NEG)) for x, y in zip(gc, gr)]
    eg = [jnp.exp(x) for x in gc]
    kb = [x * b for x, b in zip(k, beta)]
    a = [jnp.where(strict, _dot_nt(x, y) * d, 0.0) for x, y, d in zip(kb, k, decay)]
    t_inv = _tri_inverse_all(a, eye, diag_blocks)
    uw = [_dot(t, jnp.concatenate([x * b, y * e], axis=1))
          for t, x, b, y, e in zip(t_inv, v, beta, kb, eg)]
    intra = [_dot_nt(x, y) * d for x, y, d in zip(q, k, decay)]
    g_last = [x[C - 1:C, :] for x in gc]
    k_dec = [x * jnp.exp(gl - g) for x, gl, g in zip(k, g_last, gc)]
    wq = [jnp.concatenate([x[:, DK:], y * e], axis=0) for x, y, e in zip(uw, q, eg)]

    states = [state_scr[h] for h in range(H)]
    for c in range(n_chunks):
        idx = [c * H + h for h in range(H)]
        ws = [_dot(wq[i], states[h]) for h, i in enumerate(idx)]
        v_new = [uw[i][:, :DK] - y[:C] for i, y in zip(idx, ws)]
        o = [y[C:] + _dot(intra[i], vn) for i, y, vn in zip(idx, ws, v_new)]
        states = [s * jnp.exp(g_last[i]) + _dot_tn(k_dec[i], vn) for s, i, vn in zip(states, idx, v_new)]
        for h in range(H):
            on = o[h] * lax.rsqrt(jnp.mean(o[h] * o[h], axis=-1, keepdims=True) + EPS) * norm_w
            o_ref[rows(c), h * DK:(h + 1) * DK] = on * _silu(zg_ref[rows(c), h * DK:(h + 1) * DK])
    for h in range(H):
        state_scr[h] = states[h]


def gdn_chunk(qkv, gb, gbt, zg, norm_w, B, S, lb=256):
    T = qkv.shape[0]
    nsb = S // lb
    VW = GDN_HEADS * GDN_DV
    zg_blk = GDN_QKV // VW
    return pl.pallas_call(
        functools.partial(_gdn_chunk_kernel, lb=lb),
        grid=(B, nsb),
        in_specs=[pl.BlockSpec((lb, GDN_QKV), lambda b, s: (b * nsb + s, 0)),
                  pl.BlockSpec((lb, LANES), lambda b, s: (b * nsb + s, 0)),
                  pl.BlockSpec((8, lb), lambda b, s: (0, b * nsb + s)),
                  pl.BlockSpec((lb, VW), lambda b, s: (b * nsb + s, zg_blk)),
                  pl.BlockSpec((1, GDN_DV), lambda b, s: (0, 0))],
        out_specs=pl.BlockSpec((lb, VW), lambda b, s: (b * nsb + s, 0)),
        out_shape=jax.ShapeDtypeStruct((T, VW), F32),
        scratch_shapes=[pltpu.VMEM((GDN_HEADS, GDN_DK, GDN_DV), F32)],
        compiler_params=_cparams(("parallel", "arbitrary")),
        name="gdn_chunk",
    )(qkv, gb, gbt, zg, norm_w)


def _compress_kernel(x_ref, pea_ref, peb_ref, w1a_ref, w1b_ref, w2_ref, o_ref, p1_scr, p2_scr):
    kc = pl.program_id(1)

    @pl.when(kc == 0)
    def _():
        p1_scr[...] = jnp.zeros(p1_scr.shape, F32)
        p2_scr[...] = jnp.zeros(p2_scr.shape, F32)

    x = x_ref[0]
    p1_scr[...] += _dot(x + pea_ref[...], w1a_ref[...])
    p2_scr[...] += _dot(x + peb_ref[...], w1b_ref[...])

    @pl.when(kc == pl.num_programs(1) - 1)
    def _():
        p2 = p2_scr[...]
        rows = p2.shape[0]
        hid = p1_scr[...] + pltpu.roll(p2, rows - 1, 0)
        c0 = math.sqrt(2.0 / math.pi)
        act = 0.5 * hid * (1.0 + jnp.tanh(c0 * (hid + 0.044715 * (hid * hid * hid))))
        o_ref[0] = _dot(act, w2_ref[...])


def nsa_compress(x, pe_a, pe_b, w1a, w1b, w2, B, S, kchunk=512):
    nr = S // CMP_STRIDE
    xr = x.reshape(B, nr, CMP_STRIDE * NSA_GROUPS * NSA_DIM)
    KW = xr.shape[2]
    HW = NSA_GROUPS * CMP_HIDDEN
    OW = NSA_GROUPS * NSA_DIM
    return pl.pallas_call(
        _compress_kernel,
        grid=(B, KW // kchunk),
        in_specs=[pl.BlockSpec((1, nr, kchunk), lambda b, k: (b, 0, k)),
                  pl.BlockSpec((1, kchunk), lambda b, k: (0, k)),
                  pl.BlockSpec((1, kchunk), lambda b, k: (0, k)),
                  pl.BlockSpec((kchunk, HW), lambda b, k: (k, 0)),
                  pl.BlockSpec((kchunk, HW), lambda b, k: (k, 0)),
                  pl.BlockSpec((HW, OW), lambda b, k: (0, 0))],
        out_specs=pl.BlockSpec((1, nr, OW), lambda b, k: (b, 0, 0)),
        out_shape=jax.ShapeDtypeStruct((B, nr, OW), F32),
        scratch_shapes=[pltpu.VMEM((nr, HW), F32), pltpu.VMEM((nr, HW), F32)],
        compiler_params=_cparams(("parallel", "arbitrary")),
        name="nsa_compress",
    )(xr, pe_a, pe_b, w1a, w1b, w2)


def _slope(h):
    return float(2.0 ** (-8.0 * (h + 1) / NSA_HEADS))


SEL_TK = 2 * SEL_BLOCK
SEL_NT = 4


def _nsa_cmp_kernel(q_ref, kc_ref, vc_ref, zg_ref, ov_ref, o_ref, sel_ref, flag_ref, *, tq, n_sel):
    i = pl.program_id(1)
    n_tiles = n_sel * SEL_BLOCK // SEL_TK
    blk_tile = (lax.broadcasted_iota(jnp.int32, (n_sel, n_tiles), 0) * SEL_BLOCK // SEL_TK
                == lax.broadcasted_iota(jnp.int32, (n_sel, n_tiles), 1))
    to_tile = jnp.where(blk_tile, 1.0, 0.0).astype(BF16)
    ncmp = kc_ref.shape[1]
    D = NSA_DIM
    scale = D ** -0.5 * LOG2E
    qpos = i * tq + lax.broadcasted_iota(jnp.int32, (tq, 1), 0)
    nidx = lax.broadcasted_iota(jnp.int32, (1, ncmp), 1)
    valid = (nidx * CMP_STRIDE + (CMP_BLOCK - 1)) <= qpos
    any_valid = (qpos >= CMP_BLOCK - 1).astype(F32)
    centre_rel = (nidx * CMP_STRIDE - i * tq).astype(F32) + 0.5 * (CMP_BLOCK - 1)
    gates = _sigmoid(zg_ref[...])
    kc = kc_ref[0].astype(BF16)
    vc = vc_ref[0].astype(BF16)
    ov_t = ov_ref[...]
    jf = lax.broadcasted_iota(jnp.int32, (n_sel, 1), 0).astype(F32)
    qblk = ((i * tq + lax.broadcasted_iota(jnp.int32, (1, tq), 1)) // SEL_BLOCK).astype(F32)
    forced = (jf == 0.0) | (jf == qblk) | (jf == qblk - 1.0)
    causal_blk = jf <= qblk
    heads = range(NSA_HPG)
    lane_half = lax.broadcasted_iota(jnp.int32, (tq, LANES), 1) < D
    assert ncmp % LANES == 0
    work = []
    for g in range(NSA_GROUPS):
        kg = kc[:, g * D:(g + 1) * D]
        vg = vc[:, g * D:(g + 1) * D]
        hs = [g * NSA_HPG + r for r in heads]
        s = [_dot_nt((q_ref[:, h * D:(h + 1) * D] * scale).astype(BF16), kg)
             + (_slope(h) * LOG2E) * centre_rel for h in hs]
        s = [jnp.where(valid, x, NEG) for x in s]
        v1 = jnp.concatenate([vg, jnp.ones_like(vg)], axis=1)
        p = []
        for r, h in enumerate(hs):
            tiles = [s[r][:, c * LANES:(c + 1) * LANES] for c in range(ncmp // LANES)]
            tile_max = functools.reduce(jnp.maximum, tiles)
            m = jnp.maximum(jnp.full((tq, LANES), NEG, F32), jnp.max(tile_max, axis=1, keepdims=True))
            e_tiles = [jnp.exp2(t - m) for t in tiles]
            pv = jnp.dot(jnp.concatenate(e_tiles, axis=1).astype(BF16), v1, preferred_element_type=F32)
            row_sum = jnp.where(lane_half, pltpu.roll(pv, D, 1), pv)
            norm = any_valid / row_sum
            o_ref[:, h * D:(h + 1) * D] = gates[:, 3 * h:3 * h + 1] * (pv * norm)[:, :D]
            p.append([t * norm for t in e_tiles])
        psum = jnp.concatenate([(a + b) + (c + d) for a, b, c, d in zip(*p)], axis=1)
        p_hi = psum.astype(BF16)
        rem = psum - p_hi.astype(F32)
        p_mid = rem.astype(BF16)
        p_lo = (rem - p_mid.astype(F32)).astype(BF16)
        imp_t = (_dot_nt(ov_t, p_hi) + _dot_nt(ov_t, p_mid)) + _dot_nt(ov_t, p_lo)
        work.append(jnp.where(forced, BIG, jnp.where(causal_blk, imp_t, NEG)))
    groups = range(NSA_GROUPS)
    selm = [jnp.zeros((n_sel, tq), F32) for _ in groups]
    for _ in range(min(SEL_TOPN, n_sel)):
        mx = [jnp.max(w, axis=0, keepdims=True) for w in work]
        first = [jnp.min(jnp.where(w == m, jf, float(n_sel)), axis=0, keepdims=True) for w, m in zip(work, mx)]
        pick = [jf == f for f in first]
        selm = [jnp.where(pk, 1.0, sm) for pk, sm in zip(pick, selm)]
        work = [jnp.where(pk, -jnp.inf, w) for pk, w in zip(pick, work)]
    for g in groups:
        selb = selm[g].T.astype(BF16)
        sel_ref[:, g * n_sel:(g + 1) * n_sel] = selb
        hits = jnp.max(jnp.dot(selb, to_tile, preferred_element_type=F32), axis=0, keepdims=True)
        flag_ref[0, g:g + 1, :] = (hits > 0.5).astype(jnp.int32)


def nsa_cmp(q, k_cmp, v_cmp, zg, overlap, B, S, tq=256):
    T = q.shape[0]
    nq = S // tq
    n_sel = S // SEL_BLOCK
    ncmp = k_cmp.shape[1]
    QW = NSA_HEADS * NSA_DIM
    KW = NSA_GROUPS * NSA_DIM
    return pl.pallas_call(
        functools.partial(_nsa_cmp_kernel, tq=tq, n_sel=n_sel),
        grid=(B, nq),
        in_specs=[pl.BlockSpec((tq, QW), lambda b, i: (b * nq + i, 0)),
                  pl.BlockSpec((1, ncmp, KW), lambda b, i: (b, 0, 0)),
                  pl.BlockSpec((1, ncmp, KW), lambda b, i: (b, 0, 0)),
                  pl.BlockSpec((tq, LANES), lambda b, i: (b * nq + i, 0)),
                  pl.BlockSpec((n_sel, ncmp), lambda b, i: (0, 0))],
        out_specs=[pl.BlockSpec((tq, QW), lambda b, i: (b * nq + i, 0)),
                   pl.BlockSpec((tq, NSA_GROUPS * n_sel), lambda b, i: (b * nq + i, 0)),
                   pl.BlockSpec((1, NSA_GROUPS, S // SEL_TK), lambda b, i: (b * nq + i, 0, 0))],
        out_shape=[jax.ShapeDtypeStruct((T, QW), F32),
                   jax.ShapeDtypeStruct((T, NSA_GROUPS * n_sel), BF16),
                   jax.ShapeDtypeStruct((B * nq, NSA_GROUPS, S // SEL_TK), jnp.int32)],
        compiler_params=_cparams(("parallel", "parallel")),
        name="nsa_cmp",
    )(q, k_cmp, v_cmp, zg, overlap)


def _nsa_sel_kernel(flags_ref, q_ref, k_ref, v_ref, sel_ref, zg_ref, prev_ref, o_ref,
                    list_smem, m_scr, acc_scr, *, tq, n_sel, nq):
    b = pl.program_id(0)
    i = pl.program_id(1)
    D = NSA_DIM
    scale = D ** -0.5 * LOG2E
    n_tiles = n_sel * SEL_BLOCK // SEL_TK
    n_causal = (i * tq + tq - 1) // SEL_TK + 1
    ks_w = SEL_NT * SEL_TK
    qpos = i * tq + lax.broadcasted_iota(jnp.int32, (tq, 1), 0)
    lane_t = lax.broadcasted_iota(jnp.int32, (1, SEL_TK), 1)
    blk_iota = lax.broadcasted_iota(jnp.int32, (n_sel, ks_w), 0)
    gates = _sigmoid(zg_ref[...])

    for g in range(NSA_GROUPS):
        base = ((b * nq + i) * NSA_GROUPS + g) * n_tiles

        def scan(j, n, base=base):
            list_smem[n] = j
            return n + (flags_ref[base + j] != 0).astype(jnp.int32)

        count = lax.fori_loop(0, n_causal, scan, jnp.int32(0))

        for r in range(NSA_HPG):
            m_scr[r] = jnp.full(m_scr.shape[1:], NEG, F32)
            acc_scr[r] = jnp.zeros(acc_scr.shape[1:], F32)
        qs = [(q_ref[:, (g * NSA_HPG + r) * D:(g * NSA_HPG + r + 1) * D] * scale).astype(BF16)
              for r in range(NSA_HPG)]
        sel_g = sel_ref[:, g * n_sel:(g + 1) * n_sel]

        def step(st, carry, g=g, count=count, qs=qs, sel_g=sel_g):
            k_parts, v_parts, kpos_parts, kblk_parts = [], [], [], []
            for s in range(SEL_NT):
                idx = st * SEL_NT + s
                j = list_smem[jnp.minimum(idx, count - 1)]
                start = pl.multiple_of(j * SEL_TK, SEL_TK)
                k_parts.append(k_ref[pl.ds(start, SEL_TK), g * D:(g + 1) * D])
                v_parts.append(v_ref[pl.ds(start, SEL_TK), g * D:(g + 1) * D])
                tid = jnp.where(idx < count, j, -1)
                kpos_parts.append(tid * SEL_TK + lane_t)
                kblk_parts.append(tid * (SEL_TK // SEL_BLOCK) + lane_t // SEL_BLOCK)
            k = jnp.concatenate(k_parts, axis=0)
            v = jnp.concatenate(v_parts, axis=0)
            kpos = jnp.concatenate(kpos_parts, axis=1)
            kblk = jnp.concatenate(kblk_parts, axis=1)
            expand = jnp.where(blk_iota == kblk, 1.0, 0.0).astype(BF16)
            picked = jnp.dot(sel_g, expand, preferred_element_type=F32)
            allowed = jnp.where(kpos <= qpos, picked, 0.0) > 0.5
            krel = (kpos - i * tq).astype(F32)
            hs = range(NSA_HPG)
            s_ = [_dot_nt(qs[r], k) + (_slope(g * NSA_HPG + r) * LOG2E) * krel for r in hs]
            s_ = [jnp.where(allowed, x, NEG) for x in s_]
            v1 = jnp.concatenate([v, jnp.ones_like(v)], axis=1)
            new_m, pv = [], []
            for r in hs:
                tiles = [s_[r][:, c * LANES:(c + 1) * LANES] for c in range(ks_w // LANES)]
                tile_max = functools.reduce(jnp.maximum, tiles)
                m_new = jnp.maximum(m_scr[r], jnp.max(tile_max, axis=1, keepdims=True))
                p = jnp.concatenate([jnp.exp2(t - m_new) for t in tiles], axis=1).astype(BF16)
                pv.append(jnp.dot(p, v1, preferred_element_type=F32))
                new_m.append(m_new)
            for r in hs:
                acc_scr[r] = jnp.exp2(m_scr[r] - new_m[r]) * acc_scr[r] + pv[r]
                m_scr[r] = new_m[r]
            return carry

        lax.fori_loop(0, (count + SEL_NT - 1) // SEL_NT, step, jnp.int32(0))

        for r in range(NSA_HPG):
            h = g * NSA_HPG + r
            acc = acc_scr[r]
            o_h = (acc * (1.0 / pltpu.roll(acc, D, 1)))[:, :D]
            o_ref[:, h * D:(h + 1) * D] = (prev_ref[:, h * D:(h + 1) * D]
                                           + gates[:, 3 * h + 1:3 * h + 2] * o_h)


def nsa_sel(q, ks, vs, sel, flags, zg, prev, B, S, tq=256):
    T = q.shape[0]
    nq = S // tq
    n_sel = S // SEL_BLOCK
    QW = NSA_HEADS * NSA_DIM
    KW = NSA_GROUPS * NSA_DIM
    qmap = lambda b, i, fl: (b * nq + i, 0)
    kmap = lambda b, i, fl: (b, 0)
    grid_spec = pltpu.PrefetchScalarGridSpec(
        num_scalar_prefetch=1,
        grid=(B, nq),
        in_specs=[pl.BlockSpec((tq, QW), qmap),
                  pl.BlockSpec((S, KW), kmap),
                  pl.BlockSpec((S, KW), kmap),
                  pl.BlockSpec((tq, NSA_GROUPS * n_sel), qmap),
                  pl.BlockSpec((tq, LANES), qmap),
                  pl.BlockSpec((tq, QW), qmap)],
        out_specs=pl.BlockSpec((tq, QW), qmap),
        scratch_shapes=[pltpu.SMEM((S // SEL_TK,), jnp.int32),
                        pltpu.VMEM((NSA_HPG, tq, LANES), F32),
                        pltpu.VMEM((NSA_HPG, tq, LANES), F32)],
    )
    return pl.pallas_call(
        functools.partial(_nsa_sel_kernel, tq=tq, n_sel=n_sel, nq=nq),
        grid_spec=grid_spec,
        out_shape=jax.ShapeDtypeStruct((T, QW), F32),
        compiler_params=_cparams(("parallel", "arbitrary")),
        name="nsa_sel",
    )(flags.reshape(-1), q, ks, vs, sel, zg, prev)


def _nsa_win_kernel(q_ref, k0_ref, k1_ref, k2_ref, v0_ref, v1_ref, v2_ref, zg_ref, prev_ref, o_ref, *, tq):
    i = pl.program_id(1)
    D = NSA_DIM
    scale = D ** -0.5 * LOG2E
    nback = WINDOW // tq
    tkw = (nback + 1) * tq
    k = jnp.concatenate([k0_ref[...], k1_ref[...], k2_ref[...]], axis=0).astype(BF16)
    v = jnp.concatenate([v0_ref[...], v1_ref[...], v2_ref[...]], axis=0).astype(BF16)
    qrel = lax.broadcasted_iota(jnp.int32, (tq, tkw), 0)
    krel = lax.broadcasted_iota(jnp.int32, (tq, tkw), 1) - nback * tq
    dw = qrel - krel
    wvalid = jnp.where(dw >= 0, jnp.where(dw < WINDOW, krel + i * tq, -1), -1) >= 0
    krow = (lax.broadcasted_iota(jnp.int32, (1, tkw), 1) - nback * tq).astype(F32)
    gates = _sigmoid(zg_ref[...])
    heads = range(NSA_HPG)
    for g in range(NSA_GROUPS):
        kg = k[:, g * D:(g + 1) * D]
        vg = v[:, g * D:(g + 1) * D]
        hs = [g * NSA_HPG + r for r in heads]
        s = [_dot_nt((q_ref[:, h * D:(h + 1) * D] * scale).astype(BF16), kg)
             + (_slope(h) * LOG2E) * krow for h in hs]
        s = [jnp.where(wvalid, x, NEG) for x in s]
        v1 = jnp.concatenate([vg, jnp.ones_like(vg)], axis=1)
        pv = []
        for x in s:
            tiles = [x[:, c * LANES:(c + 1) * LANES] for c in range(tkw // LANES)]
            tile_max = functools.reduce(jnp.maximum, tiles)
            m = jnp.maximum(jnp.full((tq, LANES), NEG, F32), jnp.max(tile_max, axis=1, keepdims=True))
            e = jnp.concatenate([jnp.exp2(t - m) for t in tiles], axis=1).astype(BF16)
            pv.append(jnp.dot(e, v1, preferred_element_type=F32))
        for r, h in enumerate(hs):
            o_h = (pv[r] * (1.0 / pltpu.roll(pv[r], D, 1)))[:, :D]
            o_ref[:, h * D:(h + 1) * D] = (prev_ref[:, h * D:(h + 1) * D]
                                           + gates[:, 3 * h + 2:3 * h + 3] * o_h)


def nsa_win(q, kw, vw, zg, prev, B, S, tq=256):
    T = q.shape[0]
    assert WINDOW % tq == 0 and WINDOW // tq == 2
    nq = S // tq
    QW = NSA_HEADS * NSA_DIM
    KW = NSA_GROUPS * NSA_DIM
    qmap = lambda b, i: (b * nq + i, 0)
    back = lambda d: (lambda b, i: (b * nq + jnp.maximum(i - d, 0), 0))
    kspecs = [pl.BlockSpec((tq, KW), back(2)), pl.BlockSpec((tq, KW), back(1)), pl.BlockSpec((tq, KW), back(0))]
    return pl.pallas_call(
        functools.partial(_nsa_win_kernel, tq=tq),
        grid=(B, nq),
        in_specs=[pl.BlockSpec((tq, QW), qmap)] + kspecs + kspecs
                 + [pl.BlockSpec((tq, LANES), qmap), pl.BlockSpec((tq, QW), qmap)],
        out_specs=pl.BlockSpec((tq, QW), qmap),
        out_shape=jax.ShapeDtypeStruct((T, QW), F32),
        compiler_params=_cparams(("parallel", "parallel")),
        name="nsa_win",
    )(q, kw, kw, kw, vw, vw, vw, zg, prev)


def _rot_half_cols(w):
    half = w.shape[-1] // 2
    return jnp.concatenate([-w[..., half:], w[..., :half]], axis=-1)


def _even_weights(w_in, w_uq, w_ukv):
    D = w_in.shape[0]
    o = 0
    cuts = {}
    for name, n in (("cq", MLA_Q_RANK), ("ckv", MLA_KV_RANK), ("kr", MLA_ROPE), ("zq", 512), ("zk", 512),
                    ("zv", 512), ("zg", 512), ("zb", GDN_HEADS), ("za", GDN_HEADS)):
        cuts[name] = w_in[:, o:o + n]
        o += n
    z = lambda n: jnp.zeros((D, n), F32)
    misc1 = jnp.concatenate([z(MLA_NOPE), cuts["kr"], cuts["zb"], cuts["za"],
                             z(LANES - MLA_NOPE - MLA_ROPE - 2 * GDN_HEADS)], axis=1)
    misc2 = jnp.concatenate([z(MLA_NOPE), _rot_half_cols(cuts["kr"]), z(LANES - MLA_NOPE - MLA_ROPE)], axis=1)
    w_even = jnp.concatenate([cuts["cq"], cuts["ckv"], misc1, misc2,
                              cuts["zq"], cuts["zk"], cuts["zv"], cuts["zg"]], axis=1).astype(BF16)
    qd = MLA_NOPE + MLA_ROPE
    wq3 = w_uq.reshape(MLA_Q_RANK, MLA_HEADS, qd)
    zq = jnp.zeros((MLA_Q_RANK, MLA_HEADS, MLA_HB - qd), F32)
    wq = jnp.concatenate([wq3, zq], axis=2).reshape(MLA_Q_RANK, MLA_HEADS * MLA_HB).astype(BF16)
    wqr = jnp.concatenate([jnp.zeros((MLA_Q_RANK, MLA_HEADS, MLA_NOPE), F32),
                           _rot_half_cols(wq3[:, :, MLA_NOPE:]), zq], axis=2)
    wqr = wqr.reshape(MLA_Q_RANK, MLA_HEADS * MLA_HB).astype(BF16)
    wkv3 = w_ukv.reshape(MLA_KV_RANK, MLA_HEADS, MLA_NOPE + MLA_V)
    wk = jnp.concatenate([wkv3[:, :, :MLA_NOPE], jnp.zeros((MLA_KV_RANK, MLA_HEADS, MLA_HB - MLA_NOPE), F32)],
                         axis=2).reshape(MLA_KV_RANK, MLA_HEADS * MLA_HB).astype(BF16)
    wv4 = wkv3[:, :, MLA_NOPE:].reshape(MLA_KV_RANK, MLA_HEADS // 2, 2, MLA_V)
    zv = jnp.zeros((MLA_KV_RANK, MLA_HEADS // 2, MLA_V), F32)
    wv = jnp.stack([jnp.concatenate([wv4[:, :, 0], zv], axis=2), jnp.concatenate([zv, wv4[:, :, 1]], axis=2)],
                   axis=2).reshape(MLA_KV_RANK, MLA_HEADS * MLA_HB).astype(BF16)
    return w_even, wq, wqr, wk, wv


def _rope_tables(S):
    half = MLA_ROPE // 2
    inv = ROPE_BASE ** (-jnp.arange(half, dtype=F32) / half)
    ang = jnp.arange(S, dtype=F32)[:, None] * inv[None, :]
    cos = jnp.cos(ang)
    sin = jnp.sin(ang)
    pad = jnp.zeros((S, LANES - MLA_NOPE - MLA_ROPE), F32)
    cos_t = jnp.concatenate([jnp.ones((S, MLA_NOPE), F32), cos, cos, pad], axis=1)
    sin_t = jnp.concatenate([jnp.zeros((S, MLA_NOPE), F32), sin, sin, pad], axis=1)
    return cos_t, sin_t


def _lane_vec(vals, start):
    return jnp.zeros((1, LANES), F32).at[0, start:start + vals.shape[0]].set(vals)


def even_mixer_layer(h, B, S, attn_norm, w_in, q_norm, kv_norm, w_uq, w_ukv, conv_w, a_log, dt_bias,
                     gdn_norm, w_out, tables):
    w_even, wq, wqr, wk, wv = _even_weights(w_in, w_uq, w_ukv)
    zm, zg = rms_matmul(h, attn_norm, w_even, (896, 2048))
    cos_t, sin_t = tables
    q, k, v = mla_prep(zm, q_norm.reshape(1, -1), kv_norm.reshape(1, -1), wq, wqr, wk, wv, cos_t, sin_t, S)
    o_mla = mla_flash(q, k, v, B, S, tq=min(512, S))
    qkv, gb, gbt = gdn_prep(zg, zm, conv_w, _lane_vec(a_log, DECAY_LANE), _lane_vec(dt_bias, DECAY_LANE), S)
    o_gdn = gdn_chunk(qkv, gb, gbt, zg, gdn_norm.reshape(1, -1), B, S)
    nm = MLA_HEADS * MLA_V
    return proj_residual([o_mla, o_gdn], [w_out[:nm].astype(BF16), w_out[nm:].astype(BF16)], h)


def _compress_weights(pe, w1, w2):
    G, D = NSA_GROUPS, NSA_DIM
    eye = jnp.eye(G, dtype=F32)
    w1r = w1.reshape(CMP_BLOCK, D, CMP_HIDDEN)

    def expand(wpart):
        return jnp.einsum('ldh,gk->lgdkh', wpart, eye).reshape(CMP_STRIDE * G * D, G * CMP_HIDDEN).astype(BF16)

    def pe_vec(p):
        return jnp.broadcast_to(p[:, None, :], (CMP_STRIDE, G, D)).reshape(1, CMP_STRIDE * G * D)

    w2e = jnp.einsum('hd,gk->ghkd', w2, eye).reshape(G * CMP_HIDDEN, G * D).astype(BF16)
    return (pe_vec(pe[:CMP_STRIDE]), pe_vec(pe[CMP_STRIDE:]), expand(w1r[:CMP_STRIDE]),
            expand(w1r[CMP_STRIDE:]), w2e)


def _overlap_matrix(S):
    nr = S // CMP_STRIDE
    n_sel = S // SEL_BLOCK
    n = np.arange(nr)[:, None]
    j = np.arange(n_sel)[None, :]
    start = n * CMP_STRIDE
    ov = (start <= j * SEL_BLOCK + SEL_BLOCK - 1) & (start + CMP_BLOCK - 1 >= j * SEL_BLOCK)
    ov = ov & (n < nr - 1)
    return jnp.asarray(ov.T.astype(np.float32)).astype(BF16)


def odd_mixer_layer(h, B, S, attn_norm, w_in, pe_k, w1_k, w2_k, pe_v, w1_v, w2_v, w_out):
    D = w_in.shape[0]
    n_g = 3 * NSA_HEADS
    w_odd = jnp.concatenate([w_in, jnp.zeros((D, LANES - n_g), F32)], axis=1).astype(BF16)
    kvw = NSA_GROUPS * NSA_DIM
    q, kc, vc, ks, vs, kw, vw, zg = rms_matmul(
        h, attn_norm, w_odd, (NSA_HEADS * NSA_DIM,) + (kvw,) * 6 + (LANES,),
        dtypes=(F32, F32, F32, BF16, BF16, F32, F32, F32))
    k_cmp = nsa_compress(kc, *_compress_weights(pe_k, w1_k, w2_k), B, S)
    v_cmp = nsa_compress(vc, *_compress_weights(pe_v, w1_v, w2_v), B, S)
    o1, sel, flags = nsa_cmp(q, k_cmp, v_cmp, zg, _overlap_matrix(S), B, S)
    o2 = nsa_sel(q, ks, vs, sel, flags, zg, o1, B, S)
    o3 = nsa_win(q, kw, vw, zg, o2, B, S)
    return proj_residual([o3], [w_out.astype(BF16)], h)


def ffn_layer(h, g, wg, wu, wd):
    return ffn(h, g, wg.astype(BF16), wu.astype(BF16), wd.astype(BF16))


def kernel(x, ev_attn_norm, ev_w_in, ev_q_norm, ev_kv_norm, ev_w_uq, ev_w_ukv, ev_conv_w, ev_a_log, ev_dt_bias, ev_gdn_norm, ev_w_out, od_attn_norm, od_w_in, od_pe_k, od_w1_k, od_w2_k, od_pe_v, od_w1_v, od_w2_v, od_w_out, ffn_norm, ffn_w_gate, ffn_w_up, ffn_w_down, final_norm):
    B, S, D = x.shape
    depth = ffn_norm.shape[0]
    h = x.reshape(B * S, D)
    tables = _rope_tables(S)
    for layer in range(depth):
        i = layer // 2
        if layer % 2 == 0:
            h = even_mixer_layer(h, B, S, ev_attn_norm[i], ev_w_in[i], ev_q_norm[i], ev_kv_norm[i], ev_w_uq[i],
                                 ev_w_ukv[i], ev_conv_w[i], ev_a_log[i], ev_dt_bias[i], ev_gdn_norm[i],
                                 ev_w_out[i], tables)
        else:
            h = odd_mixer_layer(h, B, S, od_attn_norm[i], od_w_in[i], od_pe_k[i], od_w1_k[i], od_w2_k[i],
                                od_pe_v[i], od_w1_v[i], od_w2_v[i], od_w_out[i])
        h = ffn_layer(h, ffn_norm[layer], ffn_w_gate[layer], ffn_w_up[layer], ffn_w_down[layer])
    return rmsnorm_call(h, final_norm).reshape(B, S, D)
```

```python
import functools
import math

import jax
import jax.numpy as jnp
import numpy as np
from jax import lax
from jax.experimental import pallas as pl
from jax.experimental.pallas import tpu as pltpu

F32 = jnp.float32
BF16 = jnp.bfloat16

EPS = 1e-6
NEG = -1e30
BIG = 1e30
LANES = 128

MLA_HEADS = 8
MLA_Q_RANK = 384
MLA_KV_RANK = 256
MLA_NOPE = 64
MLA_ROPE = 32
MLA_V = 64
ROPE_BASE = 10000.0
GDN_HEADS = 4
GDN_DK = 128
GDN_DV = 128
GDN_CONV = 4
GDN_CHUNK = 64
NSA_HEADS = 16
NSA_GROUPS = 4
NSA_HPG = 4
NSA_DIM = 64
CMP_BLOCK = 32
CMP_STRIDE = 16
CMP_HIDDEN = 256
SEL_BLOCK = 64
SEL_TOPN = 16
WINDOW = 512

LOG2E = math.log2(math.e)
VMEM_LIMIT = 56 * 1024 * 1024
HIGHEST = lax.Precision.HIGHEST


def _cparams(sem):
    return pltpu.CompilerParams(dimension_semantics=sem, vmem_limit_bytes=VMEM_LIMIT)


def _dot(a, b):
    return jnp.dot(a.astype(BF16), b.astype(BF16), preferred_element_type=F32)


def _dot_nt(a, b):
    return lax.dot_general(a.astype(BF16), b.astype(BF16), (((1,), (1,)), ((), ())),
                           preferred_element_type=F32)


def _dot_tn(a, b):
    return lax.dot_general(a.astype(BF16), b.astype(BF16), (((0,), (0,)), ((), ())),
                           preferred_element_type=F32)


def _dot_f32(a, b):
    return jnp.dot(a, b, preferred_element_type=F32, precision=HIGHEST)


def _rms(x, g):
    var = jnp.mean(x * x, axis=-1, keepdims=True)
    return x * lax.rsqrt(var + EPS) * g


def _silu(x):
    return x * (1.0 / (1.0 + jnp.exp(-x)))


def _sigmoid(x):
    return 1.0 / (1.0 + jnp.exp(-x))


def _rms_matmul_kernel(x_ref, g_ref, w_ref, *out_refs, splits):
    xn = _rms(x_ref[...], g_ref[...])
    acc = _dot(xn, w_ref[...])
    off = 0
    for o_ref, n in zip(out_refs, splits):
        o_ref[...] = acc[:, off:off + n].astype(o_ref.dtype)
        off += n


def rms_matmul(x, g, w, splits, tm=256, dtypes=None):
    T, K = x.shape
    N = w.shape[1]
    assert sum(splits) == N and T % tm == 0
    dtypes = dtypes or (F32,) * len(splits)
    return pl.pallas_call(
        functools.partial(_rms_matmul_kernel, splits=splits),
        grid=(T // tm,),
        in_specs=[pl.BlockSpec((tm, K), lambda i: (i, 0)),
                  pl.BlockSpec((1, K), lambda i: (0, 0)),
                  pl.BlockSpec((K, N), lambda i: (0, 0))],
        out_specs=[pl.BlockSpec((tm, n), lambda i: (i, 0)) for n in splits],
        out_shape=[jax.ShapeDtypeStruct((T, n), dt) for n, dt in zip(splits, dtypes)],
        compiler_params=_cparams(("parallel",)),
        name="rms_matmul",
    )(x, g.reshape(1, K), w)


def _proj_residual_kernel(*refs, n_in):
    a_refs = refs[:n_in]
    w_refs = refs[n_in:2 * n_in]
    res_ref = refs[2 * n_in]
    o_ref = refs[2 * n_in + 1]
    acc = res_ref[...]
    for a_ref, w_ref in zip(a_refs, w_refs):
        acc = acc + _dot(a_ref[...], w_ref[...])
    o_ref[...] = acc


def proj_residual(a_list, w_list, res, tm=512):
    T, N = res.shape
    n_in = len(a_list)
    in_specs = [pl.BlockSpec((tm, a.shape[1]), lambda i: (i, 0)) for a in a_list]
    in_specs += [pl.BlockSpec(w.shape, lambda i: (0, 0)) for w in w_list]
    in_specs += [pl.BlockSpec((tm, N), lambda i: (i, 0))]
    return pl.pallas_call(
        functools.partial(_proj_residual_kernel, n_in=n_in),
        grid=(T // tm,),
        in_specs=in_specs,
        out_specs=pl.BlockSpec((tm, N), lambda i: (i, 0)),
        out_shape=jax.ShapeDtypeStruct((T, N), F32),
        compiler_params=_cparams(("parallel",)),
        name="proj_residual",
    )(*a_list, *w_list, res)


def _ffn_kernel(h_ref, g_ref, wg_ref, wu_ref, wd_ref, o_ref, *, chunks):
    h = h_ref[...]
    xn = _rms(h, g_ref[...]).astype(BF16)
    acc = h
    off = 0
    for n in chunks:
        gate = jnp.dot(xn, wg_ref[:, off:off + n], preferred_element_type=F32)
        up = jnp.dot(xn, wu_ref[:, off:off + n], preferred_element_type=F32)
        act = (_silu(gate) * up).astype(BF16)
        acc = acc + jnp.dot(act, wd_ref[off:off + n, :], preferred_element_type=F32)
        off += n
    o_ref[...] = acc


def ffn(h, g, wg, wu, wd, tm=512):
    T, D = h.shape
    Hd = wg.shape[1]
    nch = 2 if (Hd % 256 == 0) else 1
    chunks = (Hd // nch,) * nch
    single = pl.Buffered(1)
    return pl.pallas_call(
        functools.partial(_ffn_kernel, chunks=chunks),
        grid=(T // tm,),
        in_specs=[pl.BlockSpec((tm, D), lambda i: (i, 0)),
                  pl.BlockSpec((1, D), lambda i: (0, 0)),
                  pl.BlockSpec((D, Hd), lambda i: (0, 0), pipeline_mode=single),
                  pl.BlockSpec((D, Hd), lambda i: (0, 0), pipeline_mode=single),
                  pl.BlockSpec((Hd, D), lambda i: (0, 0), pipeline_mode=single)],
        out_specs=pl.BlockSpec((tm, D), lambda i: (i, 0)),
        out_shape=jax.ShapeDtypeStruct((T, D), F32),
        compiler_params=_cparams(("parallel",)),
        name="ffn",
    )(h, g.reshape(1, D), wg, wu, wd)


def _rmsnorm_kernel(x_ref, g_ref, o_ref):
    o_ref[...] = _rms(x_ref[...], g_ref[...])


def rmsnorm_call(x, g, tm=1024):
    T, D = x.shape
    return pl.pallas_call(
        _rmsnorm_kernel,
        grid=(T // tm,),
        in_specs=[pl.BlockSpec((tm, D), lambda i: (i, 0)),
                  pl.BlockSpec((1, D), lambda i: (0, 0))],
        out_specs=pl.BlockSpec((tm, D), lambda i: (i, 0)),
        out_shape=jax.ShapeDtypeStruct((T, D), F32),
        compiler_params=_cparams(("parallel",)),
        name="final_norm",
    )(x, g.reshape(1, D))


MLA_HB = 128
MLA_FLASH_HEADS = 4


def _mla_prep_kernel(zm_ref, qn_ref, kvn_ref, wq_ref, wqr_ref, wk_ref, wv_ref, c_ref, s_ref,
                     q_out, k_out, v_out):
    zm = zm_ref[...]
    cq = zm[:, :MLA_Q_RANK]
    ckv = zm[:, MLA_Q_RANK:MLA_Q_RANK + MLA_KV_RANK]
    m1 = zm[:, 640:768]
    m2 = zm[:, 768:896]
    cqn = _rms(cq, qn_ref[...]).astype(BF16)
    ckvn = _rms(ckv, kvn_ref[...]).astype(BF16)
    q = jnp.dot(cqn, wq_ref[...], preferred_element_type=F32)
    qr = jnp.dot(cqn, wqr_ref[...], preferred_element_type=F32)
    kn = jnp.dot(ckvn, wk_ref[...], preferred_element_type=F32)
    cos = c_ref[...]
    sin = s_ref[...]
    lane = lax.broadcasted_iota(jnp.int32, cos.shape, 1)
    rope_lane = (lane >= MLA_NOPE) & (lane < MLA_NOPE + MLA_ROPE)
    krot = jnp.where(rope_lane, m1 * cos + m2 * sin, 0.0)
    scale = (MLA_NOPE + MLA_ROPE) ** -0.5 * LOG2E
    for h in range(MLA_HEADS):
        sl = slice(h * MLA_HB, (h + 1) * MLA_HB)
        q_out[:, sl] = ((q[:, sl] * cos + qr[:, sl] * sin) * scale).astype(BF16)
        k_out[:, sl] = (kn[:, sl] + krot).astype(BF16)
    vlane = lax.broadcasted_iota(jnp.int32, (1, MLA_HEADS * LANES), 1)
    ones_half = ((vlane // LANES) % 2 == 0) == ((vlane % LANES) >= MLA_V)
    v = jnp.dot(ckvn, wv_ref[...], preferred_element_type=F32)
    v_out[...] = jnp.where(ones_half, 1.0, v).astype(BF16)


def mla_prep(zm, qn, kvn, wq, wqr, wk, wv, cos_t, sin_t, S, tm=512):
    T = zm.shape[0]
    nsb = S // tm
    HW = MLA_HEADS * MLA_HB
    full = lambda a: pl.BlockSpec(a.shape, lambda i: (0, 0))
    return pl.pallas_call(
        _mla_prep_kernel,
        grid=(T // tm,),
        in_specs=[pl.BlockSpec((tm, zm.shape[1]), lambda i: (i, 0)),
                  full(qn), full(kvn), full(wq), full(wqr), full(wk), full(wv),
                  pl.BlockSpec((tm, LANES), lambda i: (i % nsb, 0)),
                  pl.BlockSpec((tm, LANES), lambda i: (i % nsb, 0))],
        out_specs=[pl.BlockSpec((tm, HW), lambda i: (i, 0)),
                   pl.BlockSpec((tm, HW), lambda i: (i, 0)),
                   pl.BlockSpec((tm, HW), lambda i: (i, 0))],
        out_shape=[jax.ShapeDtypeStruct((T, HW), BF16),
                   jax.ShapeDtypeStruct((T, HW), BF16),
                   jax.ShapeDtypeStruct((T, HW), BF16)],
        compiler_params=_cparams(("parallel",)),
        name="mla_prep",
    )(zm, qn, kvn, wq, wqr, wk, wv, cos_t, sin_t)


def _mla_flash_kernel(q_ref, k_ref, v_ref, o_ref, m_scr, acc_scr, *, tq):
    i = pl.program_id(2)
    tk = tq
    m_scr[...] = jnp.full(m_scr.shape, NEG, F32)
    acc_scr[...] = jnp.zeros(acc_scr.shape, F32)
    heads = range(MLA_FLASH_HEADS)
    lane_tiles = range(tk // LANES)

    def update(j, masked):
        rows = pl.ds(pl.multiple_of(j * tk, tk), tk)
        s = [lax.dot_general(q_ref[:, h * MLA_HB:(h + 1) * MLA_HB], k_ref[rows, h * MLA_HB:(h + 1) * MLA_HB],
                             (((1,), (1,)), ((), ())), preferred_element_type=F32) for h in heads]
        if masked:
            qpos = lax.broadcasted_iota(jnp.int32, (tq, tk), 0)
            kpos = lax.broadcasted_iota(jnp.int32, (tq, tk), 1)
            mask = kpos <= qpos
            s = [jnp.where(mask, x, NEG) for x in s]
        new_m = []
        pv = []
        for h in heads:
            tiles = [s[h][:, c * LANES:(c + 1) * LANES] for c in lane_tiles]
            tile_max = functools.reduce(jnp.maximum, tiles)
            m_new = jnp.maximum(m_scr[h], jnp.max(tile_max, axis=1, keepdims=True))
            p = jnp.concatenate([jnp.exp2(t - m_new) for t in tiles], axis=1).astype(BF16)
            pv.append(jnp.dot(p, v_ref[rows, h * LANES:(h + 1) * LANES], preferred_element_type=F32))
            new_m.append(m_new)
        for h in heads:
            acc_scr[h] = jnp.exp2(m_scr[h] - new_m[h]) * acc_scr[h] + pv[h]
            m_scr[h] = new_m[h]

    def body(j, carry):
        update(j, False)
        return carry

    lax.fori_loop(0, i, body, jnp.int32(0))
    update(i, True)
    lane = lax.broadcasted_iota(jnp.int32, (tq, LANES), 1)
    for pr in range(MLA_FLASH_HEADS // 2):
        a0 = acc_scr[2 * pr]
        a1 = acc_scr[2 * pr + 1]
        o0 = a0 * (1.0 / pltpu.roll(a0, MLA_V, 1))
        o1 = a1 * (1.0 / pltpu.roll(a1, MLA_V, 1))
        o_ref[:, pr * LANES:(pr + 1) * LANES] = jnp.where(lane < MLA_V, o0, o1)


def mla_flash(q, k, v, B, S, tq=512):
    T = q.shape[0]
    nq = S // tq
    nh = MLA_FLASH_HEADS
    return pl.pallas_call(
        functools.partial(_mla_flash_kernel, tq=tq),
        grid=(B, MLA_HEADS // nh, nq),
        in_specs=[pl.BlockSpec((tq, nh * MLA_HB), lambda b, p, i: (b * nq + i, p)),
                  pl.BlockSpec((S, nh * MLA_HB), lambda b, p, i: (b, p)),
                  pl.BlockSpec((S, nh * LANES), lambda b, p, i: (b, p))],
        out_specs=pl.BlockSpec((tq, nh * MLA_V), lambda b, p, i: (b * nq + i, p)),
        out_shape=jax.ShapeDtypeStruct((T, MLA_HEADS * MLA_V), F32),
        scratch_shapes=[pltpu.VMEM((nh, tq, LANES), F32),
                        pltpu.VMEM((nh, tq, LANES), F32)],
        compiler_params=_cparams(("parallel", "parallel", "arbitrary")),
        name="mla_flash",
    )(q, k, v)


GDN_QKV = 3 * GDN_HEADS * GDN_DK
BETA_LANE = 96
DECAY_LANE = 100
HALO = 8


def _gdn_prep_kernel(z_ref, halo_ref, m1_ref, cw_ref, alog_ref, dt_ref, qkv_out, gb_out, gbt_out,
                     *, tm, tiles_per_seq):
    i = pl.program_id(0)
    x = z_ref[...]
    halo = halo_ref[...]
    halo = jnp.where(i % tiles_per_seq == 0, jnp.zeros_like(halo), halo)
    xe = jnp.concatenate([halo, x], axis=0)
    cw = cw_ref[...]
    acc = x * cw[GDN_CONV - 1:GDN_CONV, :]
    for d in range(1, GDN_CONV):
        acc = acc + xe[HALO - d:HALO - d + tm, :] * cw[GDN_CONV - 1 - d:GDN_CONV - d, :]
    y = _silu(acc)
    nq = GDN_HEADS * GDN_DK
    for h in range(GDN_HEADS):
        sl = slice(h * GDN_DK, (h + 1) * GDN_DK)
        qh = y[:, sl]
        qkv_out[:, sl] = qh * lax.rsqrt(jnp.sum(qh * qh, axis=-1, keepdims=True) + EPS) * (GDN_DK ** -0.5)
        sl2 = slice(nq + h * GDN_DK, nq + (h + 1) * GDN_DK)
        kh = y[:, sl2]
        qkv_out[:, sl2] = kh * lax.rsqrt(jnp.sum(kh * kh, axis=-1, keepdims=True) + EPS)
    qkv_out[:, 2 * nq:] = y[:, 2 * nq:]
    m1 = m1_ref[...]
    lane = lax.broadcasted_iota(jnp.int32, m1.shape, 1)
    beta = _sigmoid(m1)
    xa = m1 + dt_ref[...]
    softplus = jnp.maximum(xa, 0.0) + jnp.log(1.0 + jnp.exp(-jnp.abs(xa)))
    decay = -jnp.exp(alog_ref[...]) * softplus
    ri = lax.broadcasted_iota(jnp.int32, (tm, tm), 0)
    ci = lax.broadcasted_iota(jnp.int32, (tm, tm), 1)
    ltri = jnp.where((ri >= ci) & (ri // GDN_CHUNK == ci // GDN_CHUNK), 1.0, 0.0).astype(BF16)
    d_hi = decay.astype(BF16)
    rem = decay - d_hi.astype(F32)
    d_mid = rem.astype(BF16)
    d_lo = (rem - d_mid.astype(F32)).astype(BF16)
    gcum = (jnp.dot(ltri, d_hi, preferred_element_type=F32) + jnp.dot(ltri, d_mid, preferred_element_type=F32)
            + jnp.dot(ltri, d_lo, preferred_element_type=F32))
    gb = jnp.where(lane < DECAY_LANE, beta, gcum)
    gb_out[...] = gb
    r = lax.broadcasted_iota(jnp.int32, (8, LANES), 0)
    c = lax.broadcasted_iota(jnp.int32, (8, LANES), 1)
    pick = (c == r + BETA_LANE).astype(F32)
    gbt_out[...] = lax.dot_general(pick, gb, (((1,), (1,)), ((), ())),
                                   preferred_element_type=F32, precision=HIGHEST)


def gdn_prep(zg, zm, conv_w, alog_v, dt_v, S, tm=512):
    T = zg.shape[0]
    tiles_per_seq = S // tm
    hb = tm // HALO
    m1_blk = 640 // LANES
    return pl.pallas_call(
        functools.partial(_gdn_prep_kernel, tm=tm, tiles_per_seq=tiles_per_seq),
        grid=(T // tm,),
        in_specs=[pl.BlockSpec((tm, GDN_QKV), lambda i: (i, 0)),
                  pl.BlockSpec((HALO, GDN_QKV), lambda i: (jnp.maximum(i * hb - 1, 0), 0)),
                  pl.BlockSpec((tm, LANES), lambda i: (i, m1_blk)),
                  pl.BlockSpec((GDN_CONV, GDN_QKV), lambda i: (0, 0)),
                  pl.BlockSpec((1, LANES), lambda i: (0, 0)),
                  pl.BlockSpec((1, LANES), lambda i: (0, 0))],
        out_specs=[pl.BlockSpec((tm, GDN_QKV), lambda i: (i, 0)),
                   pl.BlockSpec((tm, LANES), lambda i: (i, 0)),
                   pl.BlockSpec((8, tm), lambda i: (0, i))],
        out_shape=[jax.ShapeDtypeStruct((T, GDN_QKV), F32),
                   jax.ShapeDtypeStruct((T, LANES), F32),
                   jax.ShapeDtypeStruct((8, T), F32)],
        compiler_params=_cparams(("parallel",)),
        name="gdn_prep",
    )(zg, zg, zm, conv_w, alog_v, dt_v)


def _dot3(a, b):
    a_hi = a.astype(BF16)
    a_lo = (a - a_hi.astype(F32)).astype(BF16)
    b_hi = b.astype(BF16)
    b_lo = (b - b_hi.astype(F32)).astype(BF16)
    return (jnp.dot(a_hi, b_hi, preferred_element_type=F32) + jnp.dot(a_hi, b_lo, preferred_element_type=F32)
            + jnp.dot(a_lo, b_hi, preferred_element_type=F32))


def _tri_inverse_all(a_list, eye, diag_blocks):
    ad = [jnp.where(diag_blocks, a, 0.0) for a in a_list]
    ao = [a - d for a, d in zip(a_list, ad)]
    a2 = [_dot(d, d) for d in ad]
    a4 = [_dot(x, x) for x in a2]
    a8 = [_dot(x, x) for x in a4]
    t = [_dot(eye - d, eye + x) for d, x in zip(ad, a2)]
    t = [_dot(y, eye + x) for y, x in zip(t, a4)]
    dinv = [_dot(y, eye + x) for y, x in zip(t, a8)]
    n = [_dot(d, o) for d, o in zip(dinv, ao)]
    n2 = [_dot(x, x) for x in n]
    t = [_dot(eye - x, eye + y) for x, y in zip(n, n2)]
    x0 = [_dot(y, d) for y, d in zip(t, dinv)]
    res = [eye - x - _dot3(a, x) for a, x in zip(a_list, x0)]
    return [x + _dot(x, r) for x, r in zip(x0, res)]


def _gdn_chunk_kernel(qkv_ref, gb_ref, gbt_ref, zg_ref, norm_ref, o_ref, state_scr, *, lb):
    C = GDN_CHUNK
    DK = GDN_DK
    H = GDN_HEADS
    nq = H * DK
    n_chunks = lb // C

    @pl.when(pl.program_id(1) == 0)
    def _():
        state_scr[...] = jnp.zeros(state_scr.shape, F32)

    ii = lax.broadcasted_iota(jnp.int32, (C, C), 0)
    jj = lax.broadcasted_iota(jnp.int32, (C, C), 1)
    lower = ii >= jj
    strict = ii > jj
    eye = (ii == jj).astype(F32)
    diag_blocks = (ii // 16) == (jj // 16)
    norm_w = norm_ref[...]

    items = [(c, h) for c in range(n_chunks) for h in range(H)]
    rows = lambda c: slice(c * C, (c + 1) * C)
    q = [qkv_ref[rows(c), h * DK:(h + 1) * DK] for c, h in items]
    k = [qkv_ref[rows(c), nq + h * DK:nq + (h + 1) * DK] for c, h in items]
    v = [qkv_ref[rows(c), 2 * nq + h * DK:2 * nq + (h + 1) * DK] for c, h in items]
    beta = [jnp.broadcast_to(gb_ref[rows(c), BETA_LANE + h:BETA_LANE + h + 1], (C, DK)) for c, h in items]
    gc = [jnp.broadcast_to(gb_ref[rows(c), DECAY_LANE + h:DECAY_LANE + h + 1], (C, DK)) for c, h in items]
    gr = [jnp.broadcast_to(gbt_ref[4 + h:5 + h, rows(c)], (C, C)) for c, h in items]
    decay = [jnp.exp(jnp.where(lower, x[:, :C] - y, NEG)) for x, y in zip(gc, gr)]
    eg = [jnp.exp(x) for x in gc]
    kb = [x * b for x, b in zip(k, beta)]
    a = [jnp.where(strict, _dot_nt(x, y) * d, 0.0) for x, y, d in zip(kb, k, decay)]
    t_inv = _tri_inverse_all(a, eye, diag_blocks)
    uw = [_dot(t, jnp.concatenate([x * b, y * e], axis=1))
          for t, x, b, y, e in zip(t_inv, v, beta, kb, eg)]
    intra = [_dot_nt(x, y) * d for x, y, d in zip(q, k, decay)]
    g_last = [x[C - 1:C, :] for x in gc]
    k_dec = [x * jnp.exp(gl - g) for x, gl, g in zip(k, g_last, gc)]
    wq = [jnp.concatenate([x[:, DK:], y * e], axis=0) for x, y, e in zip(uw, q, eg)]

    states = [state_scr[h] for h in range(H)]
    for c in range(n_chunks):
        idx = [c * H + h for h in range(H)]
        ws = [_dot(wq[i], states[h]) for h, i in enumerate(idx)]
        v_new = [uw[i][:, :DK] - y[:C] for i, y in zip(idx, ws)]
        o = [y[C:] + _dot(intra[i], vn) for i, y, vn in zip(idx, ws, v_new)]
        states = [s * jnp.exp(g_last[i]) + _dot_tn(k_dec[i], vn) for s, i, vn in zip(states, idx, v_new)]
        for h in range(H):
            on = o[h] * lax.rsqrt(jnp.mean(o[h] * o[h], axis=-1, keepdims=True) + EPS) * norm_w
            o_ref[rows(c), h * DK:(h + 1) * DK] = on * _silu(zg_ref[rows(c), h * DK:(h + 1) * DK])
    for h in range(H):
        state_scr[h] = states[h]


def gdn_chunk(qkv, gb, gbt, zg, norm_w, B, S, lb=512):
    T = qkv.shape[0]
    nsb = S // lb
    VW = GDN_HEADS * GDN_DV
    zg_blk = GDN_QKV // VW
    return pl.pallas_call(
        functools.partial(_gdn_chunk_kernel, lb=lb),
        grid=(B, nsb),
        in_specs=[pl.BlockSpec((lb, GDN_QKV), lambda b, s: (b * nsb + s, 0)),
                  pl.BlockSpec((lb, LANES), lambda b, s: (b * nsb + s, 0)),
                  pl.BlockSpec((8, lb), lambda b, s: (0, b * nsb + s)),
                  pl.BlockSpec((lb, VW), lambda b, s: (b * nsb + s, zg_blk)),
                  pl.BlockSpec((1, GDN_DV), lambda b, s: (0, 0))],
        out_specs=pl.BlockSpec((lb, VW), lambda b, s: (b * nsb + s, 0)),
        out_shape=jax.ShapeDtypeStruct((T, VW), F32),
        scratch_shapes=[pltpu.VMEM((GDN_HEADS, GDN_DK, GDN_DV), F32)],
        compiler_params=_cparams(("parallel", "arbitrary")),
        name="gdn_chunk",
    )(qkv, gb, gbt, zg, norm_w)


def _compress_kernel(x_ref, pea_ref, peb_ref, w1a_ref, w1b_ref, w2_ref, o_ref, p1_scr, p2_scr):
    kc = pl.program_id(1)

    @pl.when(kc == 0)
    def _():
        p1_scr[...] = jnp.zeros(p1_scr.shape, F32)
        p2_scr[...] = jnp.zeros(p2_scr.shape, F32)

    x = x_ref[0]
    p1_scr[...] += _dot(x + pea_ref[...], w1a_ref[...])
    p2_scr[...] += _dot(x + peb_ref[...], w1b_ref[...])

    @pl.when(kc == pl.num_programs(1) - 1)
    def _():
        p2 = p2_scr[...]
        rows = p2.shape[0]
        hid = p1_scr[...] + pltpu.roll(p2, rows - 1, 0)
        c0 = math.sqrt(2.0 / math.pi)
        act = 0.5 * hid * (1.0 + jnp.tanh(c0 * (hid + 0.044715 * (hid * hid * hid))))
        o_ref[0] = _dot(act, w2_ref[...])


def nsa_compress(x, pe_a, pe_b, w1a, w1b, w2, B, S, kchunk=512):
    nr = S // CMP_STRIDE
    xr = x.reshape(B, nr, CMP_STRIDE * NSA_GROUPS * NSA_DIM)
    KW = xr.shape[2]
    HW = NSA_GROUPS * CMP_HIDDEN
    OW = NSA_GROUPS * NSA_DIM
    return pl.pallas_call(
        _compress_kernel,
        grid=(B, KW // kchunk),
        in_specs=[pl.BlockSpec((1, nr, kchunk), lambda b, k: (b, 0, k)),
                  pl.BlockSpec((1, kchunk), lambda b, k: (0, k)),
                  pl.BlockSpec((1, kchunk), lambda b, k: (0, k)),
                  pl.BlockSpec((kchunk, HW), lambda b, k: (k, 0)),
                  pl.BlockSpec((kchunk, HW), lambda b, k: (k, 0)),
                  pl.BlockSpec((HW, OW), lambda b, k: (0, 0))],
        out_specs=pl.BlockSpec((1, nr, OW), lambda b, k: (b, 0, 0)),
        out_shape=jax.ShapeDtypeStruct((B, nr, OW), F32),
        scratch_shapes=[pltpu.VMEM((nr, HW), F32), pltpu.VMEM((nr, HW), F32)],
        compiler_params=_cparams(("parallel", "arbitrary")),
        name="nsa_compress",
    )(xr, pe_a, pe_b, w1a, w1b, w2)


def _slope(h):
    return float(2.0 ** (-8.0 * (h + 1) / NSA_HEADS))


SEL_TK = 2 * SEL_BLOCK
SEL_NT = 4


def _gate_expand(branch):
    e = np.zeros((LANES, NSA_HEADS * NSA_DIM), np.float32)
    for h in range(NSA_HEADS):
        e[3 * h + branch, h * NSA_DIM:(h + 1) * NSA_DIM] = 1.0
    return jnp.asarray(e).astype(BF16)


def _gate_matrix(zg, expand):
    g = _sigmoid(zg)
    g_hi = g.astype(BF16)
    rem = g - g_hi.astype(F32)
    g_mid = rem.astype(BF16)
    g_lo = (rem - g_mid.astype(F32)).astype(BF16)
    dot = lambda a: jnp.dot(a, expand, preferred_element_type=F32)
    return (dot(g_hi) + dot(g_mid)) + dot(g_lo)


def _value_blocks(vg):
    ones = jnp.ones_like(vg)
    return jnp.concatenate([vg, ones], axis=1), jnp.concatenate([ones, vg], axis=1)


def _nsa_cmp_kernel(q_ref, kc_ref, vc_ref, zg_ref, ge_ref, ov_ref, o_ref, sel_ref, flag_ref, work_scr,
                    *, tq, n_sel):
    i = pl.program_id(1)
    n_tiles = n_sel * SEL_BLOCK // SEL_TK
    blk_tile = (lax.broadcasted_iota(jnp.int32, (n_sel, n_tiles), 0) * SEL_BLOCK // SEL_TK
                == lax.broadcasted_iota(jnp.int32, (n_sel, n_tiles), 1))
    to_tile = jnp.where(blk_tile, 1.0, 0.0).astype(BF16)
    ncmp = kc_ref.shape[1]
    D = NSA_DIM
    qpos = i * tq + lax.broadcasted_iota(jnp.int32, (tq, 1), 0)
    nidx = lax.broadcasted_iota(jnp.int32, (1, ncmp), 1)
    valid = (nidx * CMP_STRIDE + (CMP_BLOCK - 1)) <= qpos
    any_valid = (qpos >= CMP_BLOCK - 1).astype(F32)
    centre_rel = (nidx * CMP_STRIDE - i * tq).astype(F32) + 0.5 * (CMP_BLOCK - 1)
    gate_mat = _gate_matrix(zg_ref[...], ge_ref[...])
    kc = kc_ref[0].astype(BF16)
    vc = vc_ref[0].astype(BF16)
    ov_t = ov_ref[...]
    jf = lax.broadcasted_iota(jnp.int32, (n_sel, 1), 0).astype(F32)
    qblk = ((i * tq + lax.broadcasted_iota(jnp.int32, (1, tq), 1)) // SEL_BLOCK).astype(F32)
    forced = (jf == 0.0) | (jf == qblk) | (jf == qblk - 1.0)
    causal_blk = jf <= qblk
    heads = range(NSA_HPG)
    groups = range(NSA_GROUPS)
    lane_half = lax.broadcasted_iota(jnp.int32, (tq, LANES), 1) < D
    assert ncmp % LANES == 0

    def scores(width):
        for g in groups:
            kg = kc[:width, g * D:(g + 1) * D]
            vg = vc[:width, g * D:(g + 1) * D]
            hs = [g * NSA_HPG + r for r in heads]
            s = [_dot_nt(q_ref[:, h * D:(h + 1) * D], kg) + (_slope(h) * LOG2E) * centre_rel[:, :width] for h in hs]
            s = [jnp.where(valid[:, :width], x, NEG) for x in s]
            v_even, v_odd = _value_blocks(vg)
            p, o = [], []
            for r, h in enumerate(hs):
                tiles = [s[r][:, c * LANES:(c + 1) * LANES] for c in range(width // LANES)]
                tile_max = functools.reduce(jnp.maximum, tiles)
                m = jnp.maximum(jnp.full((tq, LANES), NEG, F32), jnp.max(tile_max, axis=1, keepdims=True))
                e_tiles = [jnp.exp2(t - m) for t in tiles]
                pv = jnp.dot(jnp.concatenate(e_tiles, axis=1).astype(BF16), v_odd if r % 2 else v_even,
                             preferred_element_type=F32)
                rolled = pltpu.roll(pv, D, 1)
                row_sum = jnp.where(lane_half, pv, rolled) if r % 2 else jnp.where(lane_half, rolled, pv)
                norm = any_valid / row_sum
                o.append(pv * norm)
                p.append([t * norm for t in e_tiles])
            for t in range(NSA_HPG // 2):
                blk = slice((g * NSA_HPG // 2 + t) * LANES, (g * NSA_HPG // 2 + t + 1) * LANES)
                o_ref[:, blk] = gate_mat[:, blk] * jnp.where(lane_half, o[2 * t], o[2 * t + 1])
            psum = jnp.concatenate([(a + b) + (c + d) for a, b, c, d in zip(*p)], axis=1)
            p_hi = psum.astype(BF16)
            rem = psum - p_hi.astype(F32)
            p_mid = rem.astype(BF16)
            p_lo = (rem - p_mid.astype(F32)).astype(BF16)
            ov_w = ov_t[:, :width]
            imp_t = (_dot_nt(ov_w, p_hi) + _dot_nt(ov_w, p_mid)) + _dot_nt(ov_w, p_lo)
            work_scr[g] = jnp.where(forced, BIG, jnp.where(causal_blk, imp_t, NEG))

    n_valid = jnp.maximum((i * tq + tq - CMP_BLOCK) // CMP_STRIDE + 1, 1)
    tiles_needed = jnp.minimum((n_valid + LANES - 1) // LANES, ncmp // LANES)
    for nt in range(1, ncmp // LANES + 1):
        pl.when(tiles_needed == nt)(functools.partial(scores, nt * LANES))
    work = [work_scr[g] for g in groups]
    selm = [jnp.zeros((n_sel, tq), F32) for _ in groups]
    for _ in range(min(SEL_TOPN, n_sel)):
        mx = [jnp.max(w, axis=0, keepdims=True) for w in work]
        first = [jnp.min(jnp.where(w == m, jf, float(n_sel)), axis=0, keepdims=True) for w, m in zip(work, mx)]
        pick = [jf == f for f in first]
        selm = [jnp.where(pk, 1.0, sm) for pk, sm in zip(pick, selm)]
        work = [jnp.where(pk, -jnp.inf, w) for pk, w in zip(pick, work)]
    for g in groups:
        selb = selm[g].T.astype(BF16)
        sel_ref[:, g * n_sel:(g + 1) * n_sel] = selb
        hits = jnp.max(jnp.dot(selb, to_tile, preferred_element_type=F32), axis=0, keepdims=True)
        flag_ref[0, g:g + 1, :] = (hits > 0.5).astype(jnp.int32)


def nsa_cmp(q, k_cmp, v_cmp, zg, overlap, B, S, tq=256):
    T = q.shape[0]
    nq = S // tq
    n_sel = S // SEL_BLOCK
    ncmp = k_cmp.shape[1]
    QW = NSA_HEADS * NSA_DIM
    KW = NSA_GROUPS * NSA_DIM
    return pl.pallas_call(
        functools.partial(_nsa_cmp_kernel, tq=tq, n_sel=n_sel),
        grid=(B, nq),
        in_specs=[pl.BlockSpec((tq, QW), lambda b, i: (b * nq + i, 0)),
                  pl.BlockSpec((1, ncmp, KW), lambda b, i: (b, 0, 0)),
                  pl.BlockSpec((1, ncmp, KW), lambda b, i: (b, 0, 0)),
                  pl.BlockSpec((tq, LANES), lambda b, i: (b * nq + i, 0)),
                  pl.BlockSpec((LANES, QW), lambda b, i: (0, 0)),
                  pl.BlockSpec((n_sel, ncmp), lambda b, i: (0, 0))],
        out_specs=[pl.BlockSpec((tq, QW), lambda b, i: (b * nq + i, 0)),
                   pl.BlockSpec((tq, NSA_GROUPS * n_sel), lambda b, i: (b * nq + i, 0)),
                   pl.BlockSpec((1, NSA_GROUPS, S // SEL_TK), lambda b, i: (b * nq + i, 0, 0))],
        out_shape=[jax.ShapeDtypeStruct((T, QW), F32),
                   jax.ShapeDtypeStruct((T, NSA_GROUPS * n_sel), BF16),
                   jax.ShapeDtypeStruct((B * nq, NSA_GROUPS, S // SEL_TK), jnp.int32)],
        scratch_shapes=[pltpu.VMEM((NSA_GROUPS, n_sel, tq), F32)],
        compiler_params=_cparams(("parallel", "parallel")),
        name="nsa_cmp",
    )(q, k_cmp, v_cmp, zg, _gate_expand(0), overlap)


def _nsa_sel_kernel(flags_ref, q_ref, k_ref, v_ref, sel_ref, zg_ref, ge_ref, prev_ref, o_ref,
                    list_smem, m_scr, acc_scr, *, tq, n_sel, nq):
    b = pl.program_id(0)
    i = pl.program_id(1)
    D = NSA_DIM
    n_tiles = n_sel * SEL_BLOCK // SEL_TK
    n_causal = (i * tq + tq - 1) // SEL_TK + 1
    ks_w = SEL_NT * SEL_TK
    qpos = i * tq + lax.broadcasted_iota(jnp.int32, (tq, 1), 0)
    lane_t = lax.broadcasted_iota(jnp.int32, (1, SEL_TK), 1)
    blk_iota = lax.broadcasted_iota(jnp.int32, (n_sel, ks_w), 0)
    gate_mat = _gate_matrix(zg_ref[...], ge_ref[...])
    lane_half = lax.broadcasted_iota(jnp.int32, (tq, LANES), 1) < D

    for g in range(NSA_GROUPS):
        base = ((b * nq + i) * NSA_GROUPS + g) * n_tiles

        def scan(j, n, base=base):
            list_smem[n] = j
            return n + (flags_ref[base + j] != 0).astype(jnp.int32)

        count = lax.fori_loop(0, n_causal, scan, jnp.int32(0))

        for r in range(NSA_HPG):
            m_scr[r] = jnp.full(m_scr.shape[1:], NEG, F32)
            acc_scr[r] = jnp.zeros(acc_scr.shape[1:], F32)
        qs = [q_ref[:, (g * NSA_HPG + r) * D:(g * NSA_HPG + r + 1) * D] for r in range(NSA_HPG)]
        sel_g = sel_ref[:, g * n_sel:(g + 1) * n_sel]

        def step(st, carry, g=g, count=count, qs=qs, sel_g=sel_g):
            k_parts, v_parts, kpos_parts, kblk_parts = [], [], [], []
            for s in range(SEL_NT):
                idx = st * SEL_NT + s
                j = list_smem[jnp.minimum(idx, count - 1)]
                start = pl.multiple_of(j * SEL_TK, SEL_TK)
                k_parts.append(k_ref[pl.ds(start, SEL_TK), g * D:(g + 1) * D])
                v_parts.append(v_ref[pl.ds(start, SEL_TK), g * D:(g + 1) * D])
                tid = jnp.where(idx < count, j, -1)
                kpos_parts.append(tid * SEL_TK + lane_t)
                kblk_parts.append(tid * (SEL_TK // SEL_BLOCK) + lane_t // SEL_BLOCK)
            k = jnp.concatenate(k_parts, axis=0)
            v = jnp.concatenate(v_parts, axis=0)
            kpos = jnp.concatenate(kpos_parts, axis=1)
            kblk = jnp.concatenate(kblk_parts, axis=1)
            expand = jnp.where(blk_iota == kblk, 1.0, 0.0).astype(BF16)
            picked = jnp.dot(sel_g, expand, preferred_element_type=F32)
            allowed = jnp.where(kpos <= qpos, picked, 0.0) > 0.5
            krel = (kpos - i * tq).astype(F32)
            hs = range(NSA_HPG)
            s_ = [_dot_nt(qs[r], k) + (_slope(g * NSA_HPG + r) * LOG2E) * krel for r in hs]
            s_ = [jnp.where(allowed, x, NEG) for x in s_]
            v_even, v_odd = _value_blocks(v)
            new_m, pv = [], []
            for r in hs:
                tiles = [s_[r][:, c * LANES:(c + 1) * LANES] for c in range(ks_w // LANES)]
                tile_max = functools.reduce(jnp.maximum, tiles)
                m_new = jnp.maximum(m_scr[r], jnp.max(tile_max, axis=1, keepdims=True))
                p = jnp.concatenate([jnp.exp2(t - m_new) for t in tiles], axis=1).astype(BF16)
                pv.append(jnp.dot(p, v_odd if r % 2 else v_even, preferred_element_type=F32))
                new_m.append(m_new)
            for r in hs:
                acc_scr[r] = jnp.exp2(m_scr[r] - new_m[r]) * acc_scr[r] + pv[r]
                m_scr[r] = new_m[r]
            return carry

        lax.fori_loop(0, (count + SEL_NT - 1) // SEL_NT, step, jnp.int32(0))

        o = [acc_scr[r] * (1.0 / pltpu.roll(acc_scr[r], D, 1)) for r in range(NSA_HPG)]
        for t in range(NSA_HPG // 2):
            blk = slice((g * NSA_HPG // 2 + t) * LANES, (g * NSA_HPG // 2 + t + 1) * LANES)
            o_ref[:, blk] = prev_ref[:, blk] + gate_mat[:, blk] * jnp.where(lane_half, o[2 * t], o[2 * t + 1])


def nsa_sel(q, ks, vs, sel, flags, zg, prev, B, S, tq=256):
    T = q.shape[0]
    nq = S // tq
    n_sel = S // SEL_BLOCK
    QW = NSA_HEADS * NSA_DIM
    KW = NSA_GROUPS * NSA_DIM
    qmap = lambda b, i, fl: (b * nq + i, 0)
    kmap = lambda b, i, fl: (b, 0)
    grid_spec = pltpu.PrefetchScalarGridSpec(
        num_scalar_prefetch=1,
        grid=(B, nq),
        in_specs=[pl.BlockSpec((tq, QW), qmap),
                  pl.BlockSpec((S, KW), kmap),
                  pl.BlockSpec((S, KW), kmap),
                  pl.BlockSpec((tq, NSA_GROUPS * n_sel), qmap),
                  pl.BlockSpec((tq, LANES), qmap),
                  pl.BlockSpec((LANES, QW), lambda b, i, fl: (0, 0)),
                  pl.BlockSpec((tq, QW), qmap)],
        out_specs=pl.BlockSpec((tq, QW), qmap),
        scratch_shapes=[pltpu.SMEM((S // SEL_TK,), jnp.int32),
                        pltpu.VMEM((NSA_HPG, tq, LANES), F32),
                        pltpu.VMEM((NSA_HPG, tq, LANES), F32)],
    )
    return pl.pallas_call(
        functools.partial(_nsa_sel_kernel, tq=tq, n_sel=n_sel, nq=nq),
        grid_spec=grid_spec,
        out_shape=jax.ShapeDtypeStruct((T, QW), F32),
        compiler_params=_cparams(("parallel", "arbitrary")),
        name="nsa_sel",
    )(flags.reshape(-1), q, ks, vs, sel, zg, _gate_expand(1), prev)


def _nsa_win_kernel(q_ref, k0_ref, k1_ref, k2_ref, v0_ref, v1_ref, v2_ref, zg_ref, ge_ref, prev_ref, o_ref,
                    *, tq):
    i = pl.program_id(1)
    D = NSA_DIM
    nback = WINDOW // tq
    tkw = (nback + 1) * tq
    k = jnp.concatenate([k0_ref[...], k1_ref[...], k2_ref[...]], axis=0).astype(BF16)
    v = jnp.concatenate([v0_ref[...], v1_ref[...], v2_ref[...]], axis=0).astype(BF16)
    qrel = lax.broadcasted_iota(jnp.int32, (tq, tkw), 0)
    krel = lax.broadcasted_iota(jnp.int32, (tq, tkw), 1) - nback * tq
    dw = qrel - krel
    wvalid = jnp.where(dw >= 0, jnp.where(dw < WINDOW, krel + i * tq, -1), -1) >= 0
    krow = (lax.broadcasted_iota(jnp.int32, (1, tkw), 1) - nback * tq).astype(F32)
    gate_mat = _gate_matrix(zg_ref[...], ge_ref[...])
    lane_half = lax.broadcasted_iota(jnp.int32, (tq, LANES), 1) < D
    heads = range(NSA_HPG)
    for g in range(NSA_GROUPS):
        kg = k[:, g * D:(g + 1) * D]
        vg = v[:, g * D:(g + 1) * D]
        hs = [g * NSA_HPG + r for r in heads]
        s = [_dot_nt(q_ref[:, h * D:(h + 1) * D], kg) + (_slope(h) * LOG2E) * krow for h in hs]
        s = [jnp.where(wvalid, x, NEG) for x in s]
        v_even, v_odd = _value_blocks(vg)
        o = []
        for r, x in enumerate(s):
            tiles = [x[:, c * LANES:(c + 1) * LANES] for c in range(tkw // LANES)]
            tile_max = functools.reduce(jnp.maximum, tiles)
            m = jnp.maximum(jnp.full((tq, LANES), NEG, F32), jnp.max(tile_max, axis=1, keepdims=True))
            e = jnp.concatenate([jnp.exp2(t - m) for t in tiles], axis=1).astype(BF16)
            pv = jnp.dot(e, v_odd if r % 2 else v_even, preferred_element_type=F32)
            o.append(pv * (1.0 / pltpu.roll(pv, D, 1)))
        for t in range(NSA_HPG // 2):
            blk = slice((g * NSA_HPG // 2 + t) * LANES, (g * NSA_HPG // 2 + t + 1) * LANES)
            o_ref[:, blk] = prev_ref[:, blk] + gate_mat[:, blk] * jnp.where(lane_half, o[2 * t], o[2 * t + 1])


def nsa_win(q, kw, vw, zg, prev, B, S, tq=256):
    T = q.shape[0]
    assert WINDOW % tq == 0 and WINDOW // tq == 2
    nq = S // tq
    QW = NSA_HEADS * NSA_DIM
    KW = NSA_GROUPS * NSA_DIM
    qmap = lambda b, i: (b * nq + i, 0)
    back = lambda d: (lambda b, i: (b * nq + jnp.maximum(i - d, 0), 0))
    kspecs = [pl.BlockSpec((tq, KW), back(2)), pl.BlockSpec((tq, KW), back(1)), pl.BlockSpec((tq, KW), back(0))]
    return pl.pallas_call(
        functools.partial(_nsa_win_kernel, tq=tq),
        grid=(B, nq),
        in_specs=[pl.BlockSpec((tq, QW), qmap)] + kspecs + kspecs
                 + [pl.BlockSpec((tq, LANES), qmap), pl.BlockSpec((LANES, QW), lambda b, i: (0, 0)),
                    pl.BlockSpec((tq, QW), qmap)],
        out_specs=pl.BlockSpec((tq, QW), qmap),
        out_shape=jax.ShapeDtypeStruct((T, QW), F32),
        compiler_params=_cparams(("parallel", "parallel")),
        name="nsa_win",
    )(q, kw, kw, kw, vw, vw, vw, zg, _gate_expand(2), prev)


def _rot_half_cols(w):
    half = w.shape[-1] // 2
    return jnp.concatenate([-w[..., half:], w[..., :half]], axis=-1)


def _even_weights(w_in, w_uq, w_ukv):
    D = w_in.shape[0]
    o = 0
    cuts = {}
    for name, n in (("cq", MLA_Q_RANK), ("ckv", MLA_KV_RANK), ("kr", MLA_ROPE), ("zq", 512), ("zk", 512),
                    ("zv", 512), ("zg", 512), ("zb", GDN_HEADS), ("za", GDN_HEADS)):
        cuts[name] = w_in[:, o:o + n]
        o += n
    z = lambda n: jnp.zeros((D, n), F32)
    misc1 = jnp.concatenate([z(MLA_NOPE), cuts["kr"], cuts["zb"], cuts["za"],
                             z(LANES - MLA_NOPE - MLA_ROPE - 2 * GDN_HEADS)], axis=1)
    misc2 = jnp.concatenate([z(MLA_NOPE), _rot_half_cols(cuts["kr"]), z(LANES - MLA_NOPE - MLA_ROPE)], axis=1)
    w_even = jnp.concatenate([cuts["cq"], cuts["ckv"], misc1, misc2,
                              cuts["zq"], cuts["zk"], cuts["zv"], cuts["zg"]], axis=1).astype(BF16)
    qd = MLA_NOPE + MLA_ROPE
    wq3 = w_uq.reshape(MLA_Q_RANK, MLA_HEADS, qd)
    zq = jnp.zeros((MLA_Q_RANK, MLA_HEADS, MLA_HB - qd), F32)
    wq = jnp.concatenate([wq3, zq], axis=2).reshape(MLA_Q_RANK, MLA_HEADS * MLA_HB).astype(BF16)
    wqr = jnp.concatenate([jnp.zeros((MLA_Q_RANK, MLA_HEADS, MLA_NOPE), F32),
                           _rot_half_cols(wq3[:, :, MLA_NOPE:]), zq], axis=2)
    wqr = wqr.reshape(MLA_Q_RANK, MLA_HEADS * MLA_HB).astype(BF16)
    wkv3 = w_ukv.reshape(MLA_KV_RANK, MLA_HEADS, MLA_NOPE + MLA_V)
    wk = jnp.concatenate([wkv3[:, :, :MLA_NOPE], jnp.zeros((MLA_KV_RANK, MLA_HEADS, MLA_HB - MLA_NOPE), F32)],
                         axis=2).reshape(MLA_KV_RANK, MLA_HEADS * MLA_HB).astype(BF16)
    wv4 = wkv3[:, :, MLA_NOPE:].reshape(MLA_KV_RANK, MLA_HEADS // 2, 2, MLA_V)
    zv = jnp.zeros((MLA_KV_RANK, MLA_HEADS // 2, MLA_V), F32)
    wv = jnp.stack([jnp.concatenate([wv4[:, :, 0], zv], axis=2), jnp.concatenate([zv, wv4[:, :, 1]], axis=2)],
                   axis=2).reshape(MLA_KV_RANK, MLA_HEADS * MLA_HB).astype(BF16)
    return w_even, wq, wqr, wk, wv


def _rope_tables(S):
    half = MLA_ROPE // 2
    inv = ROPE_BASE ** (-jnp.arange(half, dtype=F32) / half)
    ang = jnp.arange(S, dtype=F32)[:, None] * inv[None, :]
    cos = jnp.cos(ang)
    sin = jnp.sin(ang)
    pad = jnp.zeros((S, LANES - MLA_NOPE - MLA_ROPE), F32)
    cos_t = jnp.concatenate([jnp.ones((S, MLA_NOPE), F32), cos, cos, pad], axis=1)
    sin_t = jnp.concatenate([jnp.zeros((S, MLA_NOPE), F32), sin, sin, pad], axis=1)
    return cos_t, sin_t


def _lane_vec(vals, start):
    return jnp.zeros((1, LANES), F32).at[0, start:start + vals.shape[0]].set(vals)


def even_mixer_layer(h, B, S, attn_norm, w_in, q_norm, kv_norm, w_uq, w_ukv, conv_w, a_log, dt_bias,
                     gdn_norm, w_out, tables):
    w_even, wq, wqr, wk, wv = _even_weights(w_in, w_uq, w_ukv)
    zm, zg = rms_matmul(h, attn_norm, w_even, (896, 2048))
    cos_t, sin_t = tables
    q, k, v = mla_prep(zm, q_norm.reshape(1, -1), kv_norm.reshape(1, -1), wq, wqr, wk, wv, cos_t, sin_t, S)
    o_mla = mla_flash(q, k, v, B, S, tq=min(512, S))
    qkv, gb, gbt = gdn_prep(zg, zm, conv_w, _lane_vec(a_log, DECAY_LANE), _lane_vec(dt_bias, DECAY_LANE), S)
    o_gdn = gdn_chunk(qkv, gb, gbt, zg, gdn_norm.reshape(1, -1), B, S)
    nm = MLA_HEADS * MLA_V
    return proj_residual([o_mla, o_gdn], [w_out[:nm].astype(BF16), w_out[nm:].astype(BF16)], h)


def _compress_weights(pe, w1, w2):
    G, D = NSA_GROUPS, NSA_DIM
    eye = jnp.eye(G, dtype=F32)
    w1r = w1.reshape(CMP_BLOCK, D, CMP_HIDDEN)

    def expand(wpart):
        return jnp.einsum('ldh,gk->lgdkh', wpart, eye).reshape(CMP_STRIDE * G * D, G * CMP_HIDDEN).astype(BF16)

    def pe_vec(p):
        return jnp.broadcast_to(p[:, None, :], (CMP_STRIDE, G, D)).reshape(1, CMP_STRIDE * G * D)

    w2e = jnp.einsum('hd,gk->ghkd', w2, eye).reshape(G * CMP_HIDDEN, G * D).astype(BF16)
    return (pe_vec(pe[:CMP_STRIDE]), pe_vec(pe[CMP_STRIDE:]), expand(w1r[:CMP_STRIDE]),
            expand(w1r[CMP_STRIDE:]), w2e)


def _overlap_matrix(S):
    nr = S // CMP_STRIDE
    n_sel = S // SEL_BLOCK
    n = np.arange(nr)[:, None]
    j = np.arange(n_sel)[None, :]
    start = n * CMP_STRIDE
    ov = (start <= j * SEL_BLOCK + SEL_BLOCK - 1) & (start + CMP_BLOCK - 1 >= j * SEL_BLOCK)
    ov = ov & (n < nr - 1)
    return jnp.asarray(ov.T.astype(np.float32)).astype(BF16)


def odd_mixer_layer(h, B, S, attn_norm, w_in, pe_k, w1_k, w2_k, pe_v, w1_v, w2_v, w_out):
    D = w_in.shape[0]
    n_g = 3 * NSA_HEADS
    qw = NSA_HEADS * NSA_DIM
    w_odd = jnp.concatenate([w_in[:, :qw] * (NSA_DIM ** -0.5 * LOG2E), w_in[:, qw:],
                             jnp.zeros((D, LANES - n_g), F32)], axis=1).astype(BF16)
    kvw = NSA_GROUPS * NSA_DIM
    q, kc, vc, ks, vs, kw, vw, zg = rms_matmul(
        h, attn_norm, w_odd, (qw,) + (kvw,) * 6 + (LANES,),
        dtypes=(BF16, F32, F32, BF16, BF16, BF16, BF16, F32))
    k_cmp = nsa_compress(kc, *_compress_weights(pe_k, w1_k, w2_k), B, S)
    v_cmp = nsa_compress(vc, *_compress_weights(pe_v, w1_v, w2_v), B, S)
    o1, sel, flags = nsa_cmp(q, k_cmp, v_cmp, zg, _overlap_matrix(S), B, S)
    o2 = nsa_sel(q, ks, vs, sel, flags, zg, o1, B, S)
    o3 = nsa_win(q, kw, vw, zg, o2, B, S)
    return proj_residual([o3], [w_out.astype(BF16)], h)


def ffn_layer(h, g, wg, wu, wd):
    return ffn(h, g, wg.astype(BF16), wu.astype(BF16), wd.astype(BF16))


def kernel(x, ev_attn_norm, ev_w_in, ev_q_norm, ev_kv_norm, ev_w_uq, ev_w_ukv, ev_conv_w, ev_a_log, ev_dt_bias, ev_gdn_norm, ev_w_out, od_attn_norm, od_w_in, od_pe_k, od_w1_k, od_w2_k, od_pe_v, od_w1_v, od_w2_v, od_w_out, ffn_norm, ffn_w_gate, ffn_w_up, ffn_w_down, final_norm):
    B, S, D = x.shape
    depth = ffn_norm.shape[0]
    h = x.reshape(B * S, D)
    tables = _rope_tables(S)
    for layer in range(depth):
        i = layer // 2
        if layer % 2 == 0:
            h = even_mixer_layer(h, B, S, ev_attn_norm[i], ev_w_in[i], ev_q_norm[i], ev_kv_norm[i], ev_w_uq[i],
                                 ev_w_ukv[i], ev_conv_w[i], ev_a_log[i], ev_dt_bias[i], ev_gdn_norm[i],
                                 ev_w_out[i], tables)
        else:
            h = odd_mixer_layer(h, B, S, od_attn_norm[i], od_w_in[i], od_pe_k[i], od_w1_k[i], od_w2_k[i],
                                od_pe_v[i], od_w1_v[i], od_w2_v[i], od_w_out[i])
        h = ffn_layer(h, ffn_norm[layer], ffn_w_gate[layer], ffn_w_up[layer], ffn_w_down[layer])
    return rmsnorm_call(h, final_norm).reshape(B, S, D)
```

```python
import functools
import math

import jax
import jax.numpy as jnp
import numpy as np
from jax import lax
from jax.experimental import pallas as pl
from jax.experimental.pallas import tpu as pltpu

F32 = jnp.float32
BF16 = jnp.bfloat16

EPS = 1e-6
NEG = -1e30
BIG = 1e30
LANES = 128

MLA_HEADS = 8
MLA_Q_RANK = 384
MLA_KV_RANK = 256
MLA_NOPE = 64
MLA_ROPE = 32
MLA_V = 64
ROPE_BASE = 10000.0
GDN_HEADS = 4
GDN_DK = 128
GDN_DV = 128
GDN_CONV = 4
GDN_CHUNK = 64
NSA_HEADS = 16
NSA_GROUPS = 4
NSA_HPG = 4
NSA_DIM = 64
CMP_BLOCK = 32
CMP_STRIDE = 16
CMP_HIDDEN = 256
SEL_BLOCK = 64
SEL_TOPN = 16
WINDOW = 512

LOG2E = math.log2(math.e)
VMEM_LIMIT = 56 * 1024 * 1024
HIGHEST = lax.Precision.HIGHEST


def _cparams(sem):
    return pltpu.CompilerParams(dimension_semantics=sem, vmem_limit_bytes=VMEM_LIMIT)


def _dot(a, b):
    return jnp.dot(a.astype(BF16), b.astype(BF16), preferred_element_type=F32)


def _dot_nt(a, b):
    return lax.dot_general(a.astype(BF16), b.astype(BF16), (((1,), (1,)), ((), ())),
                           preferred_element_type=F32)


def _dot_tn(a, b):
    return lax.dot_general(a.astype(BF16), b.astype(BF16), (((0,), (0,)), ((), ())),
                           preferred_element_type=F32)


def _dot_f32(a, b):
    return jnp.dot(a, b, preferred_element_type=F32, precision=HIGHEST)


def _rms(x, g):
    var = jnp.mean(x * x, axis=-1, keepdims=True)
    return x * lax.rsqrt(var + EPS) * g


def _silu(x):
    return x * (1.0 / (1.0 + jnp.exp(-x)))


def _sigmoid(x):
    return 1.0 / (1.0 + jnp.exp(-x))


def _rms_matmul_kernel(x_ref, g_ref, w_ref, *out_refs, splits):
    xn = _rms(x_ref[...], g_ref[...])
    acc = _dot(xn, w_ref[...])
    off = 0
    for o_ref, n in zip(out_refs, splits):
        o_ref[...] = acc[:, off:off + n].astype(o_ref.dtype)
        off += n


def rms_matmul(x, g, w, splits, tm=512, dtypes=None):
    T, K = x.shape
    N = w.shape[1]
    assert sum(splits) == N and T % tm == 0
    dtypes = dtypes or (F32,) * len(splits)
    return pl.pallas_call(
        functools.partial(_rms_matmul_kernel, splits=splits),
        grid=(T // tm,),
        in_specs=[pl.BlockSpec((tm, K), lambda i: (i, 0)),
                  pl.BlockSpec((1, K), lambda i: (0, 0)),
                  pl.BlockSpec((K, N), lambda i: (0, 0))],
        out_specs=[pl.BlockSpec((tm, n), lambda i: (i, 0)) for n in splits],
        out_shape=[jax.ShapeDtypeStruct((T, n), dt) for n, dt in zip(splits, dtypes)],
        compiler_params=_cparams(("parallel",)),
        name="rms_matmul",
    )(x, g.reshape(1, K), w)


def _proj_ffn_kernel(*refs, n_in, chunks, final_norm):
    a_refs = refs[:n_in]
    w_refs = refs[n_in:2 * n_in]
    res_ref, g_ref, wg_ref, wu_ref, wd_ref = refs[2 * n_in:2 * n_in + 5]
    fg_ref = refs[2 * n_in + 5] if final_norm else None
    o_ref = refs[-1]
    proj = functools.reduce(lambda x, y: x + y, [jnp.dot(a_ref[...], w_ref[...], preferred_element_type=F32)
                                                 for a_ref, w_ref in zip(a_refs, w_refs)])
    h = res_ref[...] + proj
    xn = _rms(h, g_ref[...]).astype(BF16)
    acc = h
    off = 0
    for n in chunks:
        gate = jnp.dot(xn, wg_ref[:, off:off + n], preferred_element_type=F32)
        up = jnp.dot(xn, wu_ref[:, off:off + n], preferred_element_type=F32)
        act = (_silu(gate) * up).astype(BF16)
        acc = acc + jnp.dot(act, wd_ref[off:off + n, :], preferred_element_type=F32)
        off += n
    o_ref[...] = _rms(acc, fg_ref[...]) if final_norm else acc


def proj_ffn(a_list, w_list, res, g, wg, wu, wd, final_g=None, tm=512):
    T, D = res.shape
    Hd = wg.shape[1]
    n_in = len(a_list)
    nch = 2 if (Hd % 256 == 0) else 1
    chunks = (Hd // nch,) * nch
    single = pl.Buffered(1)
    row = lambda n: pl.BlockSpec((tm, n), lambda i: (i, 0))
    const = lambda a: pl.BlockSpec(a.shape, lambda i: (0, 0))
    in_specs = [row(a.shape[1]) for a in a_list] + [const(w) for w in w_list]
    in_specs += [row(D), pl.BlockSpec((1, D), lambda i: (0, 0)),
                 pl.BlockSpec((D, Hd), lambda i: (0, 0), pipeline_mode=single),
                 pl.BlockSpec((D, Hd), lambda i: (0, 0), pipeline_mode=single),
                 pl.BlockSpec((Hd, D), lambda i: (0, 0), pipeline_mode=single)]
    args = [*a_list, *w_list, res, g.reshape(1, D), wg, wu, wd]
    if final_g is not None:
        in_specs.append(pl.BlockSpec((1, D), lambda i: (0, 0)))
        args.append(final_g.reshape(1, D))
    return pl.pallas_call(
        functools.partial(_proj_ffn_kernel, n_in=n_in, chunks=chunks, final_norm=final_g is not None),
        grid=(T // tm,),
        in_specs=in_specs,
        out_specs=row(D),
        out_shape=jax.ShapeDtypeStruct((T, D), F32),
        compiler_params=_cparams(("parallel",)),
        name="proj_ffn",
    )(*args)


MLA_HB = 128
MLA_FLASH_HEADS = 4


def _mla_prep_kernel(zm_ref, qn_ref, kvn_ref, wq_ref, wqr_ref, wk_ref, wv_ref, c_ref, s_ref,
                     q_out, k_out, v_out):
    zm = zm_ref[...]
    cq = zm[:, :MLA_Q_RANK]
    ckv = zm[:, MLA_Q_RANK:MLA_Q_RANK + MLA_KV_RANK]
    m1 = zm[:, 640:768]
    m2 = zm[:, 768:896]
    cqn = _rms(cq, qn_ref[...]).astype(BF16)
    ckvn = _rms(ckv, kvn_ref[...]).astype(BF16)
    q = jnp.dot(cqn, wq_ref[...], preferred_element_type=F32)
    qr = jnp.dot(cqn, wqr_ref[...], preferred_element_type=F32)
    kn = jnp.dot(ckvn, wk_ref[...], preferred_element_type=F32)
    cos = c_ref[...]
    sin = s_ref[...]
    lane = lax.broadcasted_iota(jnp.int32, cos.shape, 1)
    rope_lane = (lane >= MLA_NOPE) & (lane < MLA_NOPE + MLA_ROPE)
    krot = jnp.where(rope_lane, m1 * cos + m2 * sin, 0.0)
    scale = (MLA_NOPE + MLA_ROPE) ** -0.5 * LOG2E
    for h in range(MLA_HEADS):
        sl = slice(h * MLA_HB, (h + 1) * MLA_HB)
        q_out[:, sl] = ((q[:, sl] * cos + qr[:, sl] * sin) * scale).astype(BF16)
        k_out[:, sl] = (kn[:, sl] + krot).astype(BF16)
    vlane = lax.broadcasted_iota(jnp.int32, (1, MLA_HEADS * LANES), 1)
    ones_half = ((vlane // LANES) % 2 == 0) == ((vlane % LANES) >= MLA_V)
    v = jnp.dot(ckvn, wv_ref[...], preferred_element_type=F32)
    v_out[...] = jnp.where(ones_half, 1.0, v).astype(BF16)


def mla_prep(zm, qn, kvn, wq, wqr, wk, wv, cos_t, sin_t, S, tm=512):
    T = zm.shape[0]
    nsb = S // tm
    HW = MLA_HEADS * MLA_HB
    full = lambda a: pl.BlockSpec(a.shape, lambda i: (0, 0))
    return pl.pallas_call(
        _mla_prep_kernel,
        grid=(T // tm,),
        in_specs=[pl.BlockSpec((tm, zm.shape[1]), lambda i: (i, 0)),
                  full(qn), full(kvn), full(wq), full(wqr), full(wk), full(wv),
                  pl.BlockSpec((tm, LANES), lambda i: (i % nsb, 0)),
                  pl.BlockSpec((tm, LANES), lambda i: (i % nsb, 0))],
        out_specs=[pl.BlockSpec((tm, HW), lambda i: (i, 0)),
                   pl.BlockSpec((tm, HW), lambda i: (i, 0)),
                   pl.BlockSpec((tm, HW), lambda i: (i, 0))],
        out_shape=[jax.ShapeDtypeStruct((T, HW), BF16),
                   jax.ShapeDtypeStruct((T, HW), BF16),
                   jax.ShapeDtypeStruct((T, HW), BF16)],
        compiler_params=_cparams(("parallel",)),
        name="mla_prep",
    )(zm, qn, kvn, wq, wqr, wk, wv, cos_t, sin_t)


def _mla_flash_kernel(q_ref, k_ref, v_ref, o_ref, m_scr, acc_scr, *, tq):
    i = pl.program_id(2)
    tk = tq
    m_scr[...] = jnp.full(m_scr.shape, NEG, F32)
    acc_scr[...] = jnp.zeros(acc_scr.shape, F32)
    heads = range(MLA_FLASH_HEADS)
    lane_tiles = range(tk // LANES)

    def update(j, masked):
        rows = pl.ds(pl.multiple_of(j * tk, tk), tk)
        s = [lax.dot_general(q_ref[:, h * MLA_HB:(h + 1) * MLA_HB], k_ref[rows, h * MLA_HB:(h + 1) * MLA_HB],
                             (((1,), (1,)), ((), ())), preferred_element_type=F32) for h in heads]
        if masked:
            qpos = lax.broadcasted_iota(jnp.int32, (tq, tk), 0)
            kpos = lax.broadcasted_iota(jnp.int32, (tq, tk), 1)
            mask = kpos <= qpos
            s = [jnp.where(mask, x, NEG) for x in s]
        new_m = []
        pv = []
        for h in heads:
            tiles = [s[h][:, c * LANES:(c + 1) * LANES] for c in lane_tiles]
            tile_max = functools.reduce(jnp.maximum, tiles)
            m_new = jnp.maximum(m_scr[h], jnp.max(tile_max, axis=1, keepdims=True))
            p = jnp.concatenate([jnp.exp2(t - m_new) for t in tiles], axis=1).astype(BF16)
            pv.append(jnp.dot(p, v_ref[rows, h * LANES:(h + 1) * LANES], preferred_element_type=F32))
            new_m.append(m_new)
        for h in heads:
            acc_scr[h] = jnp.exp2(m_scr[h] - new_m[h]) * acc_scr[h] + pv[h]
            m_scr[h] = new_m[h]

    def body(j, carry):
        update(j, False)
        return carry

    lax.fori_loop(0, i, body, jnp.int32(0))
    update(i, True)
    lane = lax.broadcasted_iota(jnp.int32, (tq, LANES), 1)
    for pr in range(MLA_FLASH_HEADS // 2):
        a0 = acc_scr[2 * pr]
        a1 = acc_scr[2 * pr + 1]
        o0 = a0 * (1.0 / pltpu.roll(a0, MLA_V, 1))
        o1 = a1 * (1.0 / pltpu.roll(a1, MLA_V, 1))
        o_ref[:, pr * LANES:(pr + 1) * LANES] = jnp.where(lane < MLA_V, o0, o1).astype(o_ref.dtype)


def mla_flash(q, k, v, B, S, tq=512):
    T = q.shape[0]
    nq = S // tq
    nh = MLA_FLASH_HEADS
    return pl.pallas_call(
        functools.partial(_mla_flash_kernel, tq=tq),
        grid=(B, MLA_HEADS // nh, nq),
        in_specs=[pl.BlockSpec((tq, nh * MLA_HB), lambda b, p, i: (b * nq + i, p)),
                  pl.BlockSpec((S, nh * MLA_HB), lambda b, p, i: (b, p)),
                  pl.BlockSpec((S, nh * LANES), lambda b, p, i: (b, p))],
        out_specs=pl.BlockSpec((tq, nh * MLA_V), lambda b, p, i: (b * nq + i, p)),
        out_shape=jax.ShapeDtypeStruct((T, MLA_HEADS * MLA_V), BF16),
        scratch_shapes=[pltpu.VMEM((nh, tq, LANES), F32),
                        pltpu.VMEM((nh, tq, LANES), F32)],
        compiler_params=_cparams(("parallel", "parallel", "arbitrary")),
        name="mla_flash",
    )(q, k, v)


GDN_QKV = 3 * GDN_HEADS * GDN_DK
BETA_LANE = 96
DECAY_LANE = 100
HALO = 8


def _gdn_prep_kernel(z_ref, halo_ref, m1_ref, cw_ref, alog_ref, dt_ref, qkv_out, gb_out, gbt_out,
                     *, tm, tiles_per_seq):
    i = pl.program_id(0)
    x = z_ref[...]
    halo = halo_ref[...]
    halo = jnp.where(i % tiles_per_seq == 0, jnp.zeros_like(halo), halo)
    xe = jnp.concatenate([halo, x], axis=0)
    cw = cw_ref[...]
    acc = x * cw[GDN_CONV - 1:GDN_CONV, :]
    for d in range(1, GDN_CONV):
        acc = acc + xe[HALO - d:HALO - d + tm, :] * cw[GDN_CONV - 1 - d:GDN_CONV - d, :]
    y = _silu(acc)
    nq = GDN_HEADS * GDN_DK
    for h in range(GDN_HEADS):
        sl = slice(h * GDN_DK, (h + 1) * GDN_DK)
        qh = y[:, sl]
        qkv_out[:, sl] = qh * lax.rsqrt(jnp.sum(qh * qh, axis=-1, keepdims=True) + EPS) * (GDN_DK ** -0.5)
        sl2 = slice(nq + h * GDN_DK, nq + (h + 1) * GDN_DK)
        kh = y[:, sl2]
        qkv_out[:, sl2] = kh * lax.rsqrt(jnp.sum(kh * kh, axis=-1, keepdims=True) + EPS)
    qkv_out[:, 2 * nq:] = y[:, 2 * nq:]
    m1 = m1_ref[...]
    lane = lax.broadcasted_iota(jnp.int32, m1.shape, 1)
    beta = _sigmoid(m1)
    xa = m1 + dt_ref[...]
    softplus = jnp.maximum(xa, 0.0) + jnp.log(1.0 + jnp.exp(-jnp.abs(xa)))
    decay = -jnp.exp(alog_ref[...]) * softplus
    ri = lax.broadcasted_iota(jnp.int32, (tm, tm), 0)
    ci = lax.broadcasted_iota(jnp.int32, (tm, tm), 1)
    ltri = jnp.where((ri >= ci) & (ri // GDN_CHUNK == ci // GDN_CHUNK), 1.0, 0.0).astype(BF16)
    d_hi = decay.astype(BF16)
    rem = decay - d_hi.astype(F32)
    d_mid = rem.astype(BF16)
    d_lo = (rem - d_mid.astype(F32)).astype(BF16)
    gcum = (jnp.dot(ltri, d_hi, preferred_element_type=F32) + jnp.dot(ltri, d_mid, preferred_element_type=F32)
            + jnp.dot(ltri, d_lo, preferred_element_type=F32))
    gb = jnp.where(lane < DECAY_LANE, beta, gcum)
    gb_out[...] = gb
    r = lax.broadcasted_iota(jnp.int32, (8, LANES), 0)
    c = lax.broadcasted_iota(jnp.int32, (8, LANES), 1)
    pick = (c == r + BETA_LANE).astype(F32)
    gbt_out[...] = lax.dot_general(pick, gb, (((1,), (1,)), ((), ())),
                                   preferred_element_type=F32, precision=HIGHEST)


def gdn_prep(zg, zm, conv_w, alog_v, dt_v, S, tm=512):
    T = zg.shape[0]
    tiles_per_seq = S // tm
    hb = tm // HALO
    m1_blk = 640 // LANES
    return pl.pallas_call(
        functools.partial(_gdn_prep_kernel, tm=tm, tiles_per_seq=tiles_per_seq),
        grid=(T // tm,),
        in_specs=[pl.BlockSpec((tm, GDN_QKV), lambda i: (i, 0)),
                  pl.BlockSpec((HALO, GDN_QKV), lambda i: (jnp.maximum(i * hb - 1, 0), 0)),
                  pl.BlockSpec((tm, LANES), lambda i: (i, m1_blk)),
                  pl.BlockSpec((GDN_CONV, GDN_QKV), lambda i: (0, 0)),
                  pl.BlockSpec((1, LANES), lambda i: (0, 0)),
                  pl.BlockSpec((1, LANES), lambda i: (0, 0))],
        out_specs=[pl.BlockSpec((tm, GDN_QKV), lambda i: (i, 0)),
                   pl.BlockSpec((tm, LANES), lambda i: (i, 0)),
                   pl.BlockSpec((8, tm), lambda i: (0, i))],
        out_shape=[jax.ShapeDtypeStruct((T, GDN_QKV), F32),
                   jax.ShapeDtypeStruct((T, LANES), F32),
                   jax.ShapeDtypeStruct((8, T), F32)],
        compiler_params=_cparams(("parallel",)),
        name="gdn_prep",
    )(zg, zg, zm, conv_w, alog_v, dt_v)


def _dot3(a, b):
    a_hi = a.astype(BF16)
    a_lo = (a - a_hi.astype(F32)).astype(BF16)
    b_hi = b.astype(BF16)
    b_lo = (b - b_hi.astype(F32)).astype(BF16)
    return (jnp.dot(a_hi, b_hi, preferred_element_type=F32) + jnp.dot(a_hi, b_lo, preferred_element_type=F32)
            + jnp.dot(a_lo, b_hi, preferred_element_type=F32))


def _tri_inverse_all(a_list, eye, diag_blocks):
    ad = [jnp.where(diag_blocks, a, 0.0) for a in a_list]
    ao = [a - d for a, d in zip(a_list, ad)]
    a2 = [_dot(d, d) for d in ad]
    a4 = [_dot(x, x) for x in a2]
    a8 = [_dot(x, x) for x in a4]
    t = [_dot(eye - d, eye + x) for d, x in zip(ad, a2)]
    t = [_dot(y, eye + x) for y, x in zip(t, a4)]
    dinv = [_dot(y, eye + x) for y, x in zip(t, a8)]
    n = [_dot(d, o) for d, o in zip(dinv, ao)]
    n2 = [_dot(x, x) for x in n]
    t = [_dot(eye - x, eye + y) for x, y in zip(n, n2)]
    x0 = [_dot(y, d) for y, d in zip(t, dinv)]
    res = [eye - x - _dot3(a, x) for a, x in zip(a_list, x0)]
    return [x + _dot(x, r) for x, r in zip(x0, res)]


def _gdn_chunk_kernel(qkv_ref, gb_ref, gbt_ref, zg_ref, norm_ref, o_ref, state_scr, *, lb):
    C = GDN_CHUNK
    DK = GDN_DK
    H = GDN_HEADS
    nq = H * DK
    n_chunks = lb // C

    @pl.when(pl.program_id(1) == 0)
    def _():
        state_scr[...] = jnp.zeros(state_scr.shape, F32)

    ii = lax.broadcasted_iota(jnp.int32, (C, C), 0)
    jj = lax.broadcasted_iota(jnp.int32, (C, C), 1)
    lower = ii >= jj
    strict = ii > jj
    eye = (ii == jj).astype(F32)
    diag_blocks = (ii // 16) == (jj // 16)
    norm_w = norm_ref[...]

    items = [(c, h) for c in range(n_chunks) for h in range(H)]
    rows = lambda c: slice(c * C, (c + 1) * C)
    q = [qkv_ref[rows(c), h * DK:(h + 1) * DK] for c, h in items]
    k = [qkv_ref[rows(c), nq + h * DK:nq + (h + 1) * DK] for c, h in items]
    v = [qkv_ref[rows(c), 2 * nq + h * DK:2 * nq + (h + 1) * DK] for c, h in items]
    beta = [jnp.broadcast_to(gb_ref[rows(c), BETA_LANE + h:BETA_LANE + h + 1], (C, DK)) for c, h in items]
    gc = [jnp.broadcast_to(gb_ref[rows(c), DECAY_LANE + h:DECAY_LANE + h + 1], (C, DK)) for c, h in items]
    gr = [jnp.broadcast_to(gbt_ref[4 + h:5 + h, rows(c)], (C, C)) for c, h in items]
    decay = [jnp.exp(jnp.where(lower, x[:, :C] - y, NEG)) for x, y in zip(gc, gr)]
    eg = [jnp.exp(x) for x in gc]
    kb = [x * b for x, b in zip(k, beta)]
    a = [jnp.where(strict, _dot_nt(x, y) * d, 0.0) for x, y, d in zip(kb, k, decay)]
    t_inv = _tri_inverse_all(a, eye, diag_blocks)
    uw = [_dot(t, jnp.concatenate([x * b, y * e], axis=1))
          for t, x, b, y, e in zip(t_inv, v, beta, kb, eg)]
    intra = [_dot_nt(x, y) * d for x, y, d in zip(q, k, decay)]
    g_last = [x[C - 1:C, :] for x in gc]
    k_dec = [x * jnp.exp(gl - g) for x, gl, g in zip(k, g_last, gc)]
    wq = [jnp.concatenate([x[:, DK:], y * e], axis=0) for x, y, e in zip(uw, q, eg)]

    states = [state_scr[h] for h in range(H)]
    for c in range(n_chunks):
        idx = [c * H + h for h in range(H)]
        ws = [_dot(wq[i], states[h]) for h, i in enumerate(idx)]
        v_new = [uw[i][:, :DK] - y[:C] for i, y in zip(idx, ws)]
        o = [y[C:] + _dot(intra[i], vn) for i, y, vn in zip(idx, ws, v_new)]
        states = [s * jnp.exp(g_last[i]) + _dot_tn(k_dec[i], vn) for s, i, vn in zip(states, idx, v_new)]
        for h in range(H):
            on = o[h] * lax.rsqrt(jnp.mean(o[h] * o[h], axis=-1, keepdims=True) + EPS) * norm_w
            o_ref[rows(c), h * DK:(h + 1) * DK] = (on * _silu(zg_ref[rows(c), h * DK:(h + 1) * DK])
                                                   ).astype(o_ref.dtype)
    for h in range(H):
        state_scr[h] = states[h]


def gdn_chunk(qkv, gb, gbt, zg, norm_w, B, S, lb=512):
    T = qkv.shape[0]
    nsb = S // lb
    VW = GDN_HEADS * GDN_DV
    zg_blk = GDN_QKV // VW
    return pl.pallas_call(
        functools.partial(_gdn_chunk_kernel, lb=lb),
        grid=(B, nsb),
        in_specs=[pl.BlockSpec((lb, GDN_QKV), lambda b, s: (b * nsb + s, 0)),
                  pl.BlockSpec((lb, LANES), lambda b, s: (b * nsb + s, 0)),
                  pl.BlockSpec((8, lb), lambda b, s: (0, b * nsb + s)),
                  pl.BlockSpec((lb, VW), lambda b, s: (b * nsb + s, zg_blk)),
                  pl.BlockSpec((1, GDN_DV), lambda b, s: (0, 0))],
        out_specs=pl.BlockSpec((lb, VW), lambda b, s: (b * nsb + s, 0)),
        out_shape=jax.ShapeDtypeStruct((T, VW), BF16),
        scratch_shapes=[pltpu.VMEM((GDN_HEADS, GDN_DK, GDN_DV), F32)],
        compiler_params=_cparams(("parallel", "arbitrary")),
        name="gdn_chunk",
    )(qkv, gb, gbt, zg, norm_w)


def _compress_kernel(xl_ref, xh_ref, pea_ref, peb_ref, w1a_ref, w1b_ref, w2_ref, o_ref, p1_scr, p2_scr,
                     *, per_step):
    kc = pl.program_id(1)
    nr = p1_scr.shape[0]
    width = xl_ref.shape[1] + xh_ref.shape[1]

    @pl.when(kc == 0)
    def _():
        p1_scr[...] = jnp.zeros(p1_scr.shape, F32)
        p2_scr[...] = jnp.zeros(p2_scr.shape, F32)

    for t in range(per_step):
        rows = pl.ds(kc * per_step + t, nr, stride=CMP_STRIDE)
        x = jnp.concatenate([xl_ref[rows, :], xh_ref[rows, :]], axis=1)
        cols = slice(t * width, (t + 1) * width)
        p1_scr[...] += _dot(x + pea_ref[:, cols], w1a_ref[cols, :])
        p2_scr[...] += _dot(x + peb_ref[:, cols], w1b_ref[cols, :])

    @pl.when(kc == pl.num_programs(1) - 1)
    def _():
        p2 = p2_scr[...]
        rows = p2.shape[0]
        hid = p1_scr[...] + pltpu.roll(p2, rows - 1, 0)
        c0 = math.sqrt(2.0 / math.pi)
        act = 0.5 * hid * (1.0 + jnp.tanh(c0 * (hid + 0.044715 * (hid * hid * hid))))
        o_ref[0] = _dot(act, w2_ref[...])


def nsa_compress(x, pe_a, pe_b, w1a, w1b, w2, B, S, per_step=2):
    nr = S // CMP_STRIDE
    width = x.shape[1]
    kchunk = per_step * width
    KW = CMP_STRIDE * width
    HW = NSA_GROUPS * CMP_HIDDEN
    OW = NSA_GROUPS * NSA_DIM
    return pl.pallas_call(
        functools.partial(_compress_kernel, per_step=per_step),
        grid=(B, KW // kchunk),
        in_specs=[pl.BlockSpec((S, LANES), lambda b, k: (b, 0)),
                  pl.BlockSpec((S, LANES), lambda b, k: (b, 1)),
                  pl.BlockSpec((1, kchunk), lambda b, k: (0, k)),
                  pl.BlockSpec((1, kchunk), lambda b, k: (0, k)),
                  pl.BlockSpec((kchunk, HW), lambda b, k: (k, 0)),
                  pl.BlockSpec((kchunk, HW), lambda b, k: (k, 0)),
                  pl.BlockSpec((HW, OW), lambda b, k: (0, 0))],
        out_specs=pl.BlockSpec((1, nr, OW), lambda b, k: (b, 0, 0)),
        out_shape=jax.ShapeDtypeStruct((B, nr, OW), F32),
        scratch_shapes=[pltpu.VMEM((nr, HW), F32), pltpu.VMEM((nr, HW), F32)],
        compiler_params=_cparams(("parallel", "arbitrary")),
        name="nsa_compress",
    )(x, x, pe_a, pe_b, w1a, w1b, w2)


def _slope(h):
    return float(2.0 ** (-8.0 * (h + 1) / NSA_HEADS))


SEL_TK = 2 * SEL_BLOCK
SEL_NT = 4


def _gate_expand(branch):
    e = np.zeros((LANES, NSA_HEADS * NSA_DIM), np.float32)
    for h in range(NSA_HEADS):
        e[3 * h + branch, h * NSA_DIM:(h + 1) * NSA_DIM] = 1.0
    return jnp.asarray(e).astype(BF16)


def _gate_matrix(zg, expand):
    g = _sigmoid(zg)
    g_hi = g.astype(BF16)
    rem = g - g_hi.astype(F32)
    g_mid = rem.astype(BF16)
    g_lo = (rem - g_mid.astype(F32)).astype(BF16)
    dot = lambda a: jnp.dot(a, expand, preferred_element_type=F32)
    return (dot(g_hi) + dot(g_mid)) + dot(g_lo)


def _value_blocks(vg):
    ones = jnp.ones_like(vg)
    return jnp.concatenate([vg, ones], axis=1), jnp.concatenate([ones, vg], axis=1)


def _nsa_cmp_kernel(q_ref, kc_ref, vc_ref, zg_ref, ge_ref, ov_ref, o_ref, sel_ref, flag_ref, work_scr,
                    *, tq, n_sel):
    i = pl.program_id(1)
    n_tiles = n_sel * SEL_BLOCK // SEL_TK
    blk_tile = (lax.broadcasted_iota(jnp.int32, (n_sel, n_tiles), 0) * SEL_BLOCK // SEL_TK
                == lax.broadcasted_iota(jnp.int32, (n_sel, n_tiles), 1))
    to_tile = jnp.where(blk_tile, 1.0, 0.0).astype(BF16)
    ncmp = kc_ref.shape[1]
    D = NSA_DIM
    qpos = i * tq + lax.broadcasted_iota(jnp.int32, (tq, 1), 0)
    nidx = lax.broadcasted_iota(jnp.int32, (1, ncmp), 1)
    valid = (nidx * CMP_STRIDE + (CMP_BLOCK - 1)) <= qpos
    any_valid = (qpos >= CMP_BLOCK - 1).astype(F32)
    centre_rel = (nidx * CMP_STRIDE - i * tq).astype(F32) + 0.5 * (CMP_BLOCK - 1)
    gate_mat = _gate_matrix(zg_ref[...], ge_ref[...])
    kc = kc_ref[0].astype(BF16)
    vc = vc_ref[0].astype(BF16)
    ov_t = ov_ref[...]
    jf = lax.broadcasted_iota(jnp.int32, (n_sel, 1), 0).astype(F32)
    qblk = ((i * tq + lax.broadcasted_iota(jnp.int32, (1, tq), 1)) // SEL_BLOCK).astype(F32)
    forced = (jf == 0.0) | (jf == qblk) | (jf == qblk - 1.0)
    causal_blk = jf <= qblk
    heads = range(NSA_HPG)
    groups = range(NSA_GROUPS)
    lane_half = lax.broadcasted_iota(jnp.int32, (tq, LANES), 1) < D
    assert ncmp % LANES == 0

    def scores(width):
        for g in groups:
            kg = kc[:width, g * D:(g + 1) * D]
            vg = vc[:width, g * D:(g + 1) * D]
            hs = [g * NSA_HPG + r for r in heads]
            s = [_dot_nt(q_ref[:, h * D:(h + 1) * D], kg) + (_slope(h) * LOG2E) * centre_rel[:, :width] for h in hs]
            s = [jnp.where(valid[:, :width], x, NEG) for x in s]
            v_even, v_odd = _value_blocks(vg)
            p, o = [], []
            for r, h in enumerate(hs):
                tiles = [s[r][:, c * LANES:(c + 1) * LANES] for c in range(width // LANES)]
                tile_max = functools.reduce(jnp.maximum, tiles)
                m = jnp.maximum(jnp.full((tq, LANES), NEG, F32), jnp.max(tile_max, axis=1, keepdims=True))
                e_tiles = [jnp.exp2(t - m) for t in tiles]
                pv = jnp.dot(jnp.concatenate(e_tiles, axis=1).astype(BF16), v_odd if r % 2 else v_even,
                             preferred_element_type=F32)
                rolled = pltpu.roll(pv, D, 1)
                row_sum = jnp.where(lane_half, pv, rolled) if r % 2 else jnp.where(lane_half, rolled, pv)
                norm = any_valid / row_sum
                o.append(pv * norm)
                p.append([t * norm for t in e_tiles])
            for t in range(NSA_HPG // 2):
                blk = slice((g * NSA_HPG // 2 + t) * LANES, (g * NSA_HPG // 2 + t + 1) * LANES)
                o_ref[:, blk] = gate_mat[:, blk] * jnp.where(lane_half, o[2 * t], o[2 * t + 1])
            psum = jnp.concatenate([(a + b) + (c + d) for a, b, c, d in zip(*p)], axis=1)
            p_hi = psum.astype(BF16)
            rem = psum - p_hi.astype(F32)
            p_mid = rem.astype(BF16)
            p_lo = (rem - p_mid.astype(F32)).astype(BF16)
            ov_w = ov_t[:, :width]
            imp_t = (_dot_nt(ov_w, p_hi) + _dot_nt(ov_w, p_mid)) + _dot_nt(ov_w, p_lo)
            work_scr[g] = jnp.where(forced, BIG, jnp.where(causal_blk, imp_t, NEG))

    n_valid = jnp.maximum((i * tq + tq - CMP_BLOCK) // CMP_STRIDE + 1, 1)
    tiles_needed = jnp.minimum((n_valid + LANES - 1) // LANES, ncmp // LANES)
    for nt in range(1, ncmp // LANES + 1):
        pl.when(tiles_needed == nt)(functools.partial(scores, nt * LANES))
    work = [work_scr[g] for g in groups]
    selm = [jnp.zeros((n_sel, tq), F32) for _ in groups]
    for _ in range(min(SEL_TOPN, n_sel)):
        mx = [jnp.max(w, axis=0, keepdims=True) for w in work]
        first = [jnp.min(jnp.where(w == m, jf, float(n_sel)), axis=0, keepdims=True) for w, m in zip(work, mx)]
        pick = [jf == f for f in first]
        selm = [jnp.where(pk, 1.0, sm) for pk, sm in zip(pick, selm)]
        work = [jnp.where(pk, -jnp.inf, w) for pk, w in zip(pick, work)]
    for g in groups:
        selb = selm[g].T.astype(BF16)
        sel_ref[:, g * n_sel:(g + 1) * n_sel] = selb
        hits = jnp.max(jnp.dot(selb, to_tile, preferred_element_type=F32), axis=0, keepdims=True)
        flag_ref[0, g:g + 1, :] = (hits > 0.5).astype(jnp.int32)


def nsa_cmp(q, k_cmp, v_cmp, zg, overlap, B, S, tq=256):
    T = q.shape[0]
    nq = S // tq
    n_sel = S // SEL_BLOCK
    ncmp = k_cmp.shape[1]
    QW = NSA_HEADS * NSA_DIM
    KW = NSA_GROUPS * NSA_DIM
    return pl.pallas_call(
        functools.partial(_nsa_cmp_kernel, tq=tq, n_sel=n_sel),
        grid=(B, nq),
        in_specs=[pl.BlockSpec((tq, QW), lambda b, i: (b * nq + i, 0)),
                  pl.BlockSpec((1, ncmp, KW), lambda b, i: (b, 0, 0)),
                  pl.BlockSpec((1, ncmp, KW), lambda b, i: (b, 0, 0)),
                  pl.BlockSpec((tq, LANES), lambda b, i: (b * nq + i, 0)),
                  pl.BlockSpec((LANES, QW), lambda b, i: (0, 0)),
                  pl.BlockSpec((n_sel, ncmp), lambda b, i: (0, 0))],
        out_specs=[pl.BlockSpec((tq, QW), lambda b, i: (b * nq + i, 0)),
                   pl.BlockSpec((tq, NSA_GROUPS * n_sel), lambda b, i: (b * nq + i, 0)),
                   pl.BlockSpec((1, NSA_GROUPS, S // SEL_TK), lambda b, i: (b * nq + i, 0, 0))],
        out_shape=[jax.ShapeDtypeStruct((T, QW), F32),
                   jax.ShapeDtypeStruct((T, NSA_GROUPS * n_sel), BF16),
                   jax.ShapeDtypeStruct((B * nq, NSA_GROUPS, S // SEL_TK), jnp.int32)],
        scratch_shapes=[pltpu.VMEM((NSA_GROUPS, n_sel, tq), F32)],
        compiler_params=_cparams(("parallel", "parallel")),
        name="nsa_cmp",
    )(q, k_cmp, v_cmp, zg, _gate_expand(0), overlap)


def _nsa_sel_kernel(flags_ref, q_ref, k_ref, v_ref, sel_ref, zg_ref, ge_ref, prev_ref, o_ref,
                    list_smem, m_scr, acc_scr, *, tq, n_sel, nq):
    b = pl.program_id(0)
    i = pl.program_id(1)
    D = NSA_DIM
    n_tiles = n_sel * SEL_BLOCK // SEL_TK
    n_causal = (i * tq + tq - 1) // SEL_TK + 1
    ks_w = SEL_NT * SEL_TK
    qpos = i * tq + lax.broadcasted_iota(jnp.int32, (tq, 1), 0)
    lane_t = lax.broadcasted_iota(jnp.int32, (1, SEL_TK), 1)
    blk_iota = lax.broadcasted_iota(jnp.int32, (n_sel, ks_w), 0)
    gate_mat = _gate_matrix(zg_ref[...], ge_ref[...])
    lane_half = lax.broadcasted_iota(jnp.int32, (tq, LANES), 1) < D

    for g in range(NSA_GROUPS):
        base = ((b * nq + i) * NSA_GROUPS + g) * n_tiles

        def scan(j, n, base=base):
            list_smem[n] = j
            return n + (flags_ref[base + j] != 0).astype(jnp.int32)

        count = lax.fori_loop(0, n_causal, scan, jnp.int32(0))

        for r in range(NSA_HPG):
            m_scr[r] = jnp.full(m_scr.shape[1:], NEG, F32)
            acc_scr[r] = jnp.zeros(acc_scr.shape[1:], F32)
        qs = [q_ref[:, (g * NSA_HPG + r) * D:(g * NSA_HPG + r + 1) * D] for r in range(NSA_HPG)]
        sel_g = sel_ref[:, g * n_sel:(g + 1) * n_sel]

        def step(st, carry, g=g, count=count, qs=qs, sel_g=sel_g):
            k_parts, v_parts, kpos_parts, kblk_parts = [], [], [], []
            for s in range(SEL_NT):
                idx = st * SEL_NT + s
                j = list_smem[jnp.minimum(idx, count - 1)]
                start = pl.multiple_of(j * SEL_TK, SEL_TK)
                k_parts.append(k_ref[pl.ds(start, SEL_TK), g * D:(g + 1) * D])
                v_parts.append(v_ref[pl.ds(start, SEL_TK), g * D:(g + 1) * D])
                tid = jnp.where(idx < count, j, -1)
                kpos_parts.append(tid * SEL_TK + lane_t)
                kblk_parts.append(tid * (SEL_TK // SEL_BLOCK) + lane_t // SEL_BLOCK)
            k = jnp.concatenate(k_parts, axis=0)
            v = jnp.concatenate(v_parts, axis=0)
            kpos = jnp.concatenate(kpos_parts, axis=1)
            kblk = jnp.concatenate(kblk_parts, axis=1)
            expand = jnp.where(blk_iota == kblk, 1.0, 0.0).astype(BF16)
            picked = jnp.dot(sel_g, expand, preferred_element_type=F32)
            allowed = jnp.where(kpos <= qpos, picked, 0.0) > 0.5
            krel = (kpos - i * tq).astype(F32)
            hs = range(NSA_HPG)
            s_ = [_dot_nt(qs[r], k) + (_slope(g * NSA_HPG + r) * LOG2E) * krel for r in hs]
            s_ = [jnp.where(allowed, x, NEG) for x in s_]
            v_even, v_odd = _value_blocks(v)
            new_m, pv = [], []
            for r in hs:
                tiles = [s_[r][:, c * LANES:(c + 1) * LANES] for c in range(ks_w // LANES)]
                tile_max = functools.reduce(jnp.maximum, tiles)
                m_new = jnp.maximum(m_scr[r], jnp.max(tile_max, axis=1, keepdims=True))
                p = jnp.concatenate([jnp.exp2(t - m_new) for t in tiles], axis=1).astype(BF16)
                pv.append(jnp.dot(p, v_odd if r % 2 else v_even, preferred_element_type=F32))
                new_m.append(m_new)
            for r in hs:
                acc_scr[r] = jnp.exp2(m_scr[r] - new_m[r]) * acc_scr[r] + pv[r]
                m_scr[r] = new_m[r]
            return carry

        lax.fori_loop(0, (count + SEL_NT - 1) // SEL_NT, step, jnp.int32(0))

        o = [acc_scr[r] * (1.0 / pltpu.roll(acc_scr[r], D, 1)) for r in range(NSA_HPG)]
        for t in range(NSA_HPG // 2):
            blk = slice((g * NSA_HPG // 2 + t) * LANES, (g * NSA_HPG // 2 + t + 1) * LANES)
            o_ref[:, blk] = prev_ref[:, blk] + gate_mat[:, blk] * jnp.where(lane_half, o[2 * t], o[2 * t + 1])


def nsa_sel(q, ks, vs, sel, flags, zg, prev, B, S, tq=256):
    T = q.shape[0]
    nq = S // tq
    n_sel = S // SEL_BLOCK
    QW = NSA_HEADS * NSA_DIM
    KW = NSA_GROUPS * NSA_DIM
    qmap = lambda b, i, fl: (b * nq + i, 0)
    kmap = lambda b, i, fl: (b, 0)
    grid_spec = pltpu.PrefetchScalarGridSpec(
        num_scalar_prefetch=1,
        grid=(B, nq),
        in_specs=[pl.BlockSpec((tq, QW), qmap),
                  pl.BlockSpec((S, KW), kmap),
                  pl.BlockSpec((S, KW), kmap),
                  pl.BlockSpec((tq, NSA_GROUPS * n_sel), qmap),
                  pl.BlockSpec((tq, LANES), qmap),
                  pl.BlockSpec((LANES, QW), lambda b, i, fl: (0, 0)),
                  pl.BlockSpec((tq, QW), qmap)],
        out_specs=pl.BlockSpec((tq, QW), qmap),
        scratch_shapes=[pltpu.SMEM((S // SEL_TK,), jnp.int32),
                        pltpu.VMEM((NSA_HPG, tq, LANES), F32),
                        pltpu.VMEM((NSA_HPG, tq, LANES), F32)],
    )
    return pl.pallas_call(
        functools.partial(_nsa_sel_kernel, tq=tq, n_sel=n_sel, nq=nq),
        grid_spec=grid_spec,
        out_shape=jax.ShapeDtypeStruct((T, QW), F32),
        compiler_params=_cparams(("parallel", "arbitrary")),
        name="nsa_sel",
    )(flags.reshape(-1), q, ks, vs, sel, zg, _gate_expand(1), prev)


def _nsa_win_kernel(q_ref, k0_ref, k1_ref, k2_ref, v0_ref, v1_ref, v2_ref, zg_ref, ge_ref, prev_ref, o_ref,
                    *, tq):
    i = pl.program_id(1)
    D = NSA_DIM
    nback = WINDOW // tq
    tkw = (nback + 1) * tq
    k = jnp.concatenate([k0_ref[...], k1_ref[...], k2_ref[...]], axis=0).astype(BF16)
    v = jnp.concatenate([v0_ref[...], v1_ref[...], v2_ref[...]], axis=0).astype(BF16)
    qrel = lax.broadcasted_iota(jnp.int32, (tq, tkw), 0)
    krel = lax.broadcasted_iota(jnp.int32, (tq, tkw), 1) - nback * tq
    dw = qrel - krel
    wvalid = jnp.where(dw >= 0, jnp.where(dw < WINDOW, krel + i * tq, -1), -1) >= 0
    krow = (lax.broadcasted_iota(jnp.int32, (1, tkw), 1) - nback * tq).astype(F32)
    gate_mat = _gate_matrix(zg_ref[...], ge_ref[...])
    lane_half = lax.broadcasted_iota(jnp.int32, (tq, LANES), 1) < D
    heads = range(NSA_HPG)
    for g in range(NSA_GROUPS):
        kg = k[:, g * D:(g + 1) * D]
        vg = v[:, g * D:(g + 1) * D]
        hs = [g * NSA_HPG + r for r in heads]
        s = [_dot_nt(q_ref[:, h * D:(h + 1) * D], kg) + (_slope(h) * LOG2E) * krow for h in hs]
        s = [jnp.where(wvalid, x, NEG) for x in s]
        v_even, v_odd = _value_blocks(vg)
        o = []
        for r, x in enumerate(s):
            tiles = [x[:, c * LANES:(c + 1) * LANES] for c in range(tkw // LANES)]
            tile_max = functools.reduce(jnp.maximum, tiles)
            m = jnp.maximum(jnp.full((tq, LANES), NEG, F32), jnp.max(tile_max, axis=1, keepdims=True))
            e = jnp.concatenate([jnp.exp2(t - m) for t in tiles], axis=1).astype(BF16)
            pv = jnp.dot(e, v_odd if r % 2 else v_even, preferred_element_type=F32)
            o.append(pv * (1.0 / pltpu.roll(pv, D, 1)))
        for t in range(NSA_HPG // 2):
            blk = slice((g * NSA_HPG // 2 + t) * LANES, (g * NSA_HPG // 2 + t + 1) * LANES)
            o_ref[:, blk] = (prev_ref[:, blk] + gate_mat[:, blk] * jnp.where(lane_half, o[2 * t], o[2 * t + 1])
                             ).astype(o_ref.dtype)


def nsa_win(q, kw, vw, zg, prev, B, S, tq=256):
    T = q.shape[0]
    assert WINDOW % tq == 0 and WINDOW // tq == 2
    nq = S // tq
    QW = NSA_HEADS * NSA_DIM
    KW = NSA_GROUPS * NSA_DIM
    qmap = lambda b, i: (b * nq + i, 0)
    back = lambda d: (lambda b, i: (b * nq + jnp.maximum(i - d, 0), 0))
    kspecs = [pl.BlockSpec((tq, KW), back(2)), pl.BlockSpec((tq, KW), back(1)), pl.BlockSpec((tq, KW), back(0))]
    return pl.pallas_call(
        functools.partial(_nsa_win_kernel, tq=tq),
        grid=(B, nq),
        in_specs=[pl.BlockSpec((tq, QW), qmap)] + kspecs + kspecs
                 + [pl.BlockSpec((tq, LANES), qmap), pl.BlockSpec((LANES, QW), lambda b, i: (0, 0)),
                    pl.BlockSpec((tq, QW), qmap)],
        out_specs=pl.BlockSpec((tq, QW), qmap),
        out_shape=jax.ShapeDtypeStruct((T, QW), BF16),
        compiler_params=_cparams(("parallel", "parallel")),
        name="nsa_win",
    )(q, kw, kw, kw, vw, vw, vw, zg, _gate_expand(2), prev)


def _rot_half_cols(w):
    half = w.shape[-1] // 2
    return jnp.concatenate([-w[..., half:], w[..., :half]], axis=-1)


def _even_weights(w_in, w_uq, w_ukv):
    D = w_in.shape[0]
    o = 0
    cuts = {}
    for name, n in (("cq", MLA_Q_RANK), ("ckv", MLA_KV_RANK), ("kr", MLA_ROPE), ("zq", 512), ("zk", 512),
                    ("zv", 512), ("zg", 512), ("zb", GDN_HEADS), ("za", GDN_HEADS)):
        cuts[name] = w_in[:, o:o + n]
        o += n
    z = lambda n: jnp.zeros((D, n), F32)
    misc1 = jnp.concatenate([z(MLA_NOPE), cuts["kr"], cuts["zb"], cuts["za"],
                             z(LANES - MLA_NOPE - MLA_ROPE - 2 * GDN_HEADS)], axis=1)
    misc2 = jnp.concatenate([z(MLA_NOPE), _rot_half_cols(cuts["kr"]), z(LANES - MLA_NOPE - MLA_ROPE)], axis=1)
    w_even = jnp.concatenate([cuts["cq"], cuts["ckv"], misc1, misc2,
                              cuts["zq"], cuts["zk"], cuts["zv"], cuts["zg"]], axis=1).astype(BF16)
    qd = MLA_NOPE + MLA_ROPE
    wq3 = w_uq.reshape(MLA_Q_RANK, MLA_HEADS, qd)
    zq = jnp.zeros((MLA_Q_RANK, MLA_HEADS, MLA_HB - qd), F32)
    wq = jnp.concatenate([wq3, zq], axis=2).reshape(MLA_Q_RANK, MLA_HEADS * MLA_HB).astype(BF16)
    wqr = jnp.concatenate([jnp.zeros((MLA_Q_RANK, MLA_HEADS, MLA_NOPE), F32),
                           _rot_half_cols(wq3[:, :, MLA_NOPE:]), zq], axis=2)
    wqr = wqr.reshape(MLA_Q_RANK, MLA_HEADS * MLA_HB).astype(BF16)
    wkv3 = w_ukv.reshape(MLA_KV_RANK, MLA_HEADS, MLA_NOPE + MLA_V)
    wk = jnp.concatenate([wkv3[:, :, :MLA_NOPE], jnp.zeros((MLA_KV_RANK, MLA_HEADS, MLA_HB - MLA_NOPE), F32)],
                         axis=2).reshape(MLA_KV_RANK, MLA_HEADS * MLA_HB).astype(BF16)
    wv4 = wkv3[:, :, MLA_NOPE:].reshape(MLA_KV_RANK, MLA_HEADS // 2, 2, MLA_V)
    zv = jnp.zeros((MLA_KV_RANK, MLA_HEADS // 2, MLA_V), F32)
    wv = jnp.stack([jnp.concatenate([wv4[:, :, 0], zv], axis=2), jnp.concatenate([zv, wv4[:, :, 1]], axis=2)],
                   axis=2).reshape(MLA_KV_RANK, MLA_HEADS * MLA_HB).astype(BF16)
    return w_even, wq, wqr, wk, wv


def _rope_tables(S):
    half = MLA_ROPE // 2
    inv = ROPE_BASE ** (-jnp.arange(half, dtype=F32) / half)
    ang = jnp.arange(S, dtype=F32)[:, None] * inv[None, :]
    cos = jnp.cos(ang)
    sin = jnp.sin(ang)
    pad = jnp.zeros((S, LANES - MLA_NOPE - MLA_ROPE), F32)
    cos_t = jnp.concatenate([jnp.ones((S, MLA_NOPE), F32), cos, cos, pad], axis=1)
    sin_t = jnp.concatenate([jnp.zeros((S, MLA_NOPE), F32), sin, sin, pad], axis=1)
    return cos_t, sin_t


def _lane_vec(vals, start):
    return jnp.zeros((1, LANES), F32).at[0, start:start + vals.shape[0]].set(vals)


def even_mixer_layer(h, B, S, attn_norm, w_in, q_norm, kv_norm, w_uq, w_ukv, conv_w, a_log, dt_bias,
                     gdn_norm, w_out, tables):
    w_even, wq, wqr, wk, wv = _even_weights(w_in, w_uq, w_ukv)
    zm, zg = rms_matmul(h, attn_norm, w_even, (896, 2048))
    cos_t, sin_t = tables
    q, k, v = mla_prep(zm, q_norm.reshape(1, -1), kv_norm.reshape(1, -1), wq, wqr, wk, wv, cos_t, sin_t, S)
    o_mla = mla_flash(q, k, v, B, S, tq=min(512, S))
    qkv, gb, gbt = gdn_prep(zg, zm, conv_w, _lane_vec(a_log, DECAY_LANE), _lane_vec(dt_bias, DECAY_LANE), S)
    o_gdn = gdn_chunk(qkv, gb, gbt, zg, gdn_norm.reshape(1, -1), B, S)
    nm = MLA_HEADS * MLA_V
    return [o_mla, o_gdn], [w_out[:nm].astype(BF16), w_out[nm:].astype(BF16)]


def _compress_weights(pe, w1, w2):
    G, D = NSA_GROUPS, NSA_DIM
    eye = jnp.eye(G, dtype=F32)
    w1r = w1.reshape(CMP_BLOCK, D, CMP_HIDDEN)

    def expand(wpart):
        return jnp.einsum('ldh,gk->lgdkh', wpart, eye).reshape(CMP_STRIDE * G * D, G * CMP_HIDDEN).astype(BF16)

    def pe_vec(p):
        return jnp.broadcast_to(p[:, None, :], (CMP_STRIDE, G, D)).reshape(1, CMP_STRIDE * G * D)

    w2e = jnp.einsum('hd,gk->ghkd', w2, eye).reshape(G * CMP_HIDDEN, G * D).astype(BF16)
    return (pe_vec(pe[:CMP_STRIDE]), pe_vec(pe[CMP_STRIDE:]), expand(w1r[:CMP_STRIDE]),
            expand(w1r[CMP_STRIDE:]), w2e)


def _overlap_matrix(S):
    nr = S // CMP_STRIDE
    n_sel = S // SEL_BLOCK
    n = np.arange(nr)[:, None]
    j = np.arange(n_sel)[None, :]
    start = n * CMP_STRIDE
    ov = (start <= j * SEL_BLOCK + SEL_BLOCK - 1) & (start + CMP_BLOCK - 1 >= j * SEL_BLOCK)
    ov = ov & (n < nr - 1)
    return jnp.asarray(ov.T.astype(np.float32)).astype(BF16)


def odd_mixer_layer(h, B, S, attn_norm, w_in, pe_k, w1_k, w2_k, pe_v, w1_v, w2_v, w_out):
    D = w_in.shape[0]
    n_g = 3 * NSA_HEADS
    qw = NSA_HEADS * NSA_DIM
    w_odd = jnp.concatenate([w_in[:, :qw] * (NSA_DIM ** -0.5 * LOG2E), w_in[:, qw:],
                             jnp.zeros((D, LANES - n_g), F32)], axis=1).astype(BF16)
    kvw = NSA_GROUPS * NSA_DIM
    q, kc, vc, ks, vs, kw, vw, zg = rms_matmul(
        h, attn_norm, w_odd, (qw,) + (kvw,) * 6 + (LANES,),
        dtypes=(BF16, F32, F32, BF16, BF16, BF16, BF16, F32))
    k_cmp = nsa_compress(kc, *_compress_weights(pe_k, w1_k, w2_k), B, S)
    v_cmp = nsa_compress(vc, *_compress_weights(pe_v, w1_v, w2_v), B, S)
    o1, sel, flags = nsa_cmp(q, k_cmp, v_cmp, zg, _overlap_matrix(S), B, S)
    o2 = nsa_sel(q, ks, vs, sel, flags, zg, o1, B, S)
    o3 = nsa_win(q, kw, vw, zg, o2, B, S)
    return [o3], [w_out.astype(BF16)]


def kernel(x, ev_attn_norm, ev_w_in, ev_q_norm, ev_kv_norm, ev_w_uq, ev_w_ukv, ev_conv_w, ev_a_log, ev_dt_bias, ev_gdn_norm, ev_w_out, od_attn_norm, od_w_in, od_pe_k, od_w1_k, od_w2_k, od_pe_v, od_w1_v, od_w2_v, od_w_out, ffn_norm, ffn_w_gate, ffn_w_up, ffn_w_down, final_norm):
    B, S, D = x.shape
    depth = ffn_norm.shape[0]
    h = x.reshape(B * S, D)
    tables = _rope_tables(S)
    wg_all, wu_all, wd_all = ffn_w_gate.astype(BF16), ffn_w_up.astype(BF16), ffn_w_down.astype(BF16)
    for layer in range(depth):
        i = layer // 2
        if layer % 2 == 0:
            mix, w_mix = even_mixer_layer(h, B, S, ev_attn_norm[i], ev_w_in[i], ev_q_norm[i], ev_kv_norm[i],
                                          ev_w_uq[i], ev_w_ukv[i], ev_conv_w[i], ev_a_log[i], ev_dt_bias[i],
                                          ev_gdn_norm[i], ev_w_out[i], tables)
        else:
            mix, w_mix = odd_mixer_layer(h, B, S, od_attn_norm[i], od_w_in[i], od_pe_k[i], od_w1_k[i], od_w2_k[i],
                                         od_pe_v[i], od_w1_v[i], od_w2_v[i], od_w_out[i])
        h = proj_ffn(mix, w_mix, h, ffn_norm[layer], wg_all[layer], wu_all[layer], wd_all[layer],
                     final_g=final_norm if layer == depth - 1 else None)
    return h.reshape(B, S, D)
```

```python
import functools
import math

import jax
import jax.numpy as jnp
import numpy as np
from jax import lax
from jax.experimental import pallas as pl
from jax.experimental.pallas import tpu as pltpu

F32 = jnp.float32
BF16 = jnp.bfloat16

EPS = 1e-6
NEG = -1e30
BIG = 1e30
LANES = 128

MLA_HEADS = 8
MLA_Q_RANK = 384
MLA_KV_RANK = 256
MLA_NOPE = 64
MLA_ROPE = 32
MLA_V = 64
ROPE_BASE = 10000.0
GDN_HEADS = 4
GDN_DK = 128
GDN_DV = 128
GDN_CONV = 4
GDN_CHUNK = 64
NSA_HEADS = 16
NSA_GROUPS = 4
NSA_HPG = 4
NSA_DIM = 64
CMP_BLOCK = 32
CMP_STRIDE = 16
CMP_HIDDEN = 256
SEL_BLOCK = 64
SEL_TOPN = 16
WINDOW = 512

LOG2E = math.log2(math.e)
VMEM_LIMIT = 56 * 1024 * 1024
HIGHEST = lax.Precision.HIGHEST


def _cparams(sem):
    return pltpu.CompilerParams(dimension_semantics=sem, vmem_limit_bytes=VMEM_LIMIT)


def _dot(a, b):
    return jnp.dot(a.astype(BF16), b.astype(BF16), preferred_element_type=F32)


def _dot_nt(a, b):
    return lax.dot_general(a.astype(BF16), b.astype(BF16), (((1,), (1,)), ((), ())),
                           preferred_element_type=F32)


def _dot_tn(a, b):
    return lax.dot_general(a.astype(BF16), b.astype(BF16), (((0,), (0,)), ((), ())),
                           preferred_element_type=F32)


def _dot_f32(a, b):
    return jnp.dot(a, b, preferred_element_type=F32, precision=HIGHEST)


def _rms(x, g):
    var = jnp.mean(x * x, axis=-1, keepdims=True)
    return x * lax.rsqrt(var + EPS) * g


def _silu(x):
    return x * (1.0 / (1.0 + jnp.exp(-x)))


def _sigmoid(x):
    return 1.0 / (1.0 + jnp.exp(-x))


def _rms_matmul_kernel(x_ref, g_ref, w_ref, *out_refs, splits):
    xn = _rms(x_ref[...], g_ref[...])
    acc = _dot(xn, w_ref[...])
    off = 0
    for o_ref, n in zip(out_refs, splits):
        o_ref[...] = acc[:, off:off + n].astype(o_ref.dtype)
        off += n


def rms_matmul(x, g, w, splits, tm=512, dtypes=None):
    T, K = x.shape
    N = w.shape[1]
    assert sum(splits) == N and T % tm == 0
    dtypes = dtypes or (F32,) * len(splits)
    return pl.pallas_call(
        functools.partial(_rms_matmul_kernel, splits=splits),
        grid=(T // tm,),
        in_specs=[pl.BlockSpec((tm, K), lambda i: (i, 0)),
                  pl.BlockSpec((1, K), lambda i: (0, 0)),
                  pl.BlockSpec((K, N), lambda i: (0, 0))],
        out_specs=[pl.BlockSpec((tm, n), lambda i: (i, 0)) for n in splits],
        out_shape=[jax.ShapeDtypeStruct((T, n), dt) for n, dt in zip(splits, dtypes)],
        compiler_params=_cparams(("parallel",)),
        name="rms_matmul",
    )(x, g.reshape(1, K), w)


def _proj_ffn_kernel(*refs, n_in, chunks, final_norm):
    a_refs = refs[:n_in]
    w_refs = refs[n_in:2 * n_in]
    res_ref, g_ref, wg_ref, wu_ref, wd_ref = refs[2 * n_in:2 * n_in + 5]
    fg_ref = refs[2 * n_in + 5] if final_norm else None
    o_ref = refs[-1]
    proj = functools.reduce(lambda x, y: x + y, [jnp.dot(a_ref[...], w_ref[...], preferred_element_type=F32)
                                                 for a_ref, w_ref in zip(a_refs, w_refs)])
    h = res_ref[...] + proj
    xn = _rms(h, g_ref[...]).astype(BF16)
    acc = h
    off = 0
    for n in chunks:
        gate = jnp.dot(xn, wg_ref[:, off:off + n], preferred_element_type=F32)
        up = jnp.dot(xn, wu_ref[:, off:off + n], preferred_element_type=F32)
        act = (_silu(gate) * up).astype(BF16)
        acc = acc + jnp.dot(act, wd_ref[off:off + n, :], preferred_element_type=F32)
        off += n
    o_ref[...] = _rms(acc, fg_ref[...]) if final_norm else acc


def proj_ffn(a_list, w_list, res, g, wg, wu, wd, final_g=None, tm=512):
    T, D = res.shape
    Hd = wg.shape[1]
    n_in = len(a_list)
    nch = 2 if (Hd % 256 == 0) else 1
    chunks = (Hd // nch,) * nch
    single = pl.Buffered(1)
    row = lambda n: pl.BlockSpec((tm, n), lambda i: (i, 0))
    const = lambda a: pl.BlockSpec(a.shape, lambda i: (0, 0))
    in_specs = [row(a.shape[1]) for a in a_list] + [const(w) for w in w_list]
    in_specs += [row(D), pl.BlockSpec((1, D), lambda i: (0, 0)),
                 pl.BlockSpec((D, Hd), lambda i: (0, 0), pipeline_mode=single),
                 pl.BlockSpec((D, Hd), lambda i: (0, 0), pipeline_mode=single),
                 pl.BlockSpec((Hd, D), lambda i: (0, 0), pipeline_mode=single)]
    args = [*a_list, *w_list, res, g.reshape(1, D), wg, wu, wd]
    if final_g is not None:
        in_specs.append(pl.BlockSpec((1, D), lambda i: (0, 0)))
        args.append(final_g.reshape(1, D))
    return pl.pallas_call(
        functools.partial(_proj_ffn_kernel, n_in=n_in, chunks=chunks, final_norm=final_g is not None),
        grid=(T // tm,),
        in_specs=in_specs,
        out_specs=row(D),
        out_shape=jax.ShapeDtypeStruct((T, D), F32),
        compiler_params=_cparams(("parallel",)),
        name="proj_ffn",
    )(*args)


MLA_HB = 128
MLA_FLASH_HEADS = 4


def _mla_prep_kernel(zm_ref, qn_ref, kvn_ref, wq_ref, wqr_ref, wk_ref, wv_ref, c_ref, s_ref,
                     q_out, k_out, v_out):
    zm = zm_ref[...]
    cq = zm[:, :MLA_Q_RANK]
    ckv = zm[:, MLA_Q_RANK:MLA_Q_RANK + MLA_KV_RANK]
    m1 = zm[:, 640:768]
    m2 = zm[:, 768:896]
    cqn = _rms(cq, qn_ref[...]).astype(BF16)
    ckvn = _rms(ckv, kvn_ref[...]).astype(BF16)
    q = jnp.dot(cqn, wq_ref[...], preferred_element_type=F32)
    qr = jnp.dot(cqn, wqr_ref[...], preferred_element_type=F32)
    kn = jnp.dot(ckvn, wk_ref[...], preferred_element_type=F32)
    cos = c_ref[...]
    sin = s_ref[...]
    lane = lax.broadcasted_iota(jnp.int32, cos.shape, 1)
    rope_lane = (lane >= MLA_NOPE) & (lane < MLA_NOPE + MLA_ROPE)
    krot = jnp.where(rope_lane, m1 * cos + m2 * sin, 0.0)
    scale = (MLA_NOPE + MLA_ROPE) ** -0.5 * LOG2E
    for h in range(MLA_HEADS):
        sl = slice(h * MLA_HB, (h + 1) * MLA_HB)
        q_out[:, sl] = ((q[:, sl] * cos + qr[:, sl] * sin) * scale).astype(BF16)
        k_out[:, sl] = (kn[:, sl] + krot).astype(BF16)
    vlane = lax.broadcasted_iota(jnp.int32, (1, MLA_HEADS * LANES), 1)
    ones_half = ((vlane // LANES) % 2 == 0) == ((vlane % LANES) >= MLA_V)
    v = jnp.dot(ckvn, wv_ref[...], preferred_element_type=F32)
    v_out[...] = jnp.where(ones_half, 1.0, v).astype(BF16)


def mla_prep(zm, qn, kvn, wq, wqr, wk, wv, cos_t, sin_t, S, tm=512):
    T = zm.shape[0]
    nsb = S // tm
    HW = MLA_HEADS * MLA_HB
    full = lambda a: pl.BlockSpec(a.shape, lambda i: (0, 0))
    return pl.pallas_call(
        _mla_prep_kernel,
        grid=(T // tm,),
        in_specs=[pl.BlockSpec((tm, zm.shape[1]), lambda i: (i, 0)),
                  full(qn), full(kvn), full(wq), full(wqr), full(wk), full(wv),
                  pl.BlockSpec((tm, LANES), lambda i: (i % nsb, 0)),
                  pl.BlockSpec((tm, LANES), lambda i: (i % nsb, 0))],
        out_specs=[pl.BlockSpec((tm, HW), lambda i: (i, 0)),
                   pl.BlockSpec((tm, HW), lambda i: (i, 0)),
                   pl.BlockSpec((tm, HW), lambda i: (i, 0))],
        out_shape=[jax.ShapeDtypeStruct((T, HW), BF16),
                   jax.ShapeDtypeStruct((T, HW), BF16),
                   jax.ShapeDtypeStruct((T, HW), BF16)],
        compiler_params=_cparams(("parallel",)),
        name="mla_prep",
    )(zm, qn, kvn, wq, wqr, wk, wv, cos_t, sin_t)


def _mla_flash_kernel(q_ref, k_ref, v_ref, o_ref, m_scr, acc_scr, *, tq, tk):
    i = pl.program_id(2)
    per_q = tq // tk
    m_scr[...] = jnp.full(m_scr.shape, NEG, F32)
    acc_scr[...] = jnp.zeros(acc_scr.shape, F32)
    heads = range(MLA_FLASH_HEADS)
    lane_tiles = range(tk // LANES)

    def update(j, diag):
        rows = pl.ds(pl.multiple_of(j * tk, tk), tk)
        s = [lax.dot_general(q_ref[:, h * MLA_HB:(h + 1) * MLA_HB], k_ref[rows, h * MLA_HB:(h + 1) * MLA_HB],
                             (((1,), (1,)), ((), ())), preferred_element_type=F32) for h in heads]
        if diag is not None:
            qpos = lax.broadcasted_iota(jnp.int32, (tq, tk), 0)
            kpos = lax.broadcasted_iota(jnp.int32, (tq, tk), 1) + diag * tk
            mask = kpos <= qpos
            s = [jnp.where(mask, x, NEG) for x in s]
        new_m = []
        pv = []
        for h in heads:
            tiles = [s[h][:, c * LANES:(c + 1) * LANES] for c in lane_tiles]
            tile_max = functools.reduce(jnp.maximum, tiles)
            m_new = jnp.maximum(m_scr[h], jnp.max(tile_max, axis=1, keepdims=True))
            p = jnp.concatenate([jnp.exp2(t - m_new) for t in tiles], axis=1).astype(BF16)
            pv.append(jnp.dot(p, v_ref[rows, h * LANES:(h + 1) * LANES], preferred_element_type=F32))
            new_m.append(m_new)
        for h in heads:
            acc_scr[h] = jnp.exp2(m_scr[h] - new_m[h]) * acc_scr[h] + pv[h]
            m_scr[h] = new_m[h]

    def body(j, carry):
        update(j, None)
        return carry

    lax.fori_loop(0, i * per_q, body, jnp.int32(0))
    for d in range(per_q):
        update(i * per_q + d, d)
    lane = lax.broadcasted_iota(jnp.int32, (tq, LANES), 1)
    for pr in range(MLA_FLASH_HEADS // 2):
        a0 = acc_scr[2 * pr]
        a1 = acc_scr[2 * pr + 1]
        o0 = a0 * (1.0 / pltpu.roll(a0, MLA_V, 1))
        o1 = a1 * (1.0 / pltpu.roll(a1, MLA_V, 1))
        o_ref[:, pr * LANES:(pr + 1) * LANES] = jnp.where(lane < MLA_V, o0, o1).astype(o_ref.dtype)


def mla_flash(q, k, v, B, S, tq=512, tk=512):
    T = q.shape[0]
    tk = min(tk, tq)
    nq = S // tq
    nh = MLA_FLASH_HEADS
    return pl.pallas_call(
        functools.partial(_mla_flash_kernel, tq=tq, tk=tk),
        grid=(B, MLA_HEADS // nh, nq),
        in_specs=[pl.BlockSpec((tq, nh * MLA_HB), lambda b, p, i: (b * nq + i, p)),
                  pl.BlockSpec((S, nh * MLA_HB), lambda b, p, i: (b, p)),
                  pl.BlockSpec((S, nh * LANES), lambda b, p, i: (b, p))],
        out_specs=pl.BlockSpec((tq, nh * MLA_V), lambda b, p, i: (b * nq + i, p)),
        out_shape=jax.ShapeDtypeStruct((T, MLA_HEADS * MLA_V), BF16),
        scratch_shapes=[pltpu.VMEM((nh, tq, LANES), F32),
                        pltpu.VMEM((nh, tq, LANES), F32)],
        compiler_params=_cparams(("parallel", "parallel", "arbitrary")),
        name="mla_flash",
    )(q, k, v)


GDN_QKV = 3 * GDN_HEADS * GDN_DK
BETA_LANE = 96
DECAY_LANE = 100
HALO = 8


def _gdn_prep_kernel(z_ref, halo_ref, m1_ref, cw_ref, alog_ref, dt_ref, qkv_out, gb_out, gbt_out,
                     *, tm, tiles_per_seq):
    i = pl.program_id(0)
    x = z_ref[...]
    halo = halo_ref[...]
    halo = jnp.where(i % tiles_per_seq == 0, jnp.zeros_like(halo), halo)
    xe = jnp.concatenate([halo, x], axis=0)
    cw = cw_ref[...]
    acc = x * cw[GDN_CONV - 1:GDN_CONV, :]
    for d in range(1, GDN_CONV):
        acc = acc + xe[HALO - d:HALO - d + tm, :] * cw[GDN_CONV - 1 - d:GDN_CONV - d, :]
    y = _silu(acc)
    nq = GDN_HEADS * GDN_DK
    for h in range(GDN_HEADS):
        sl = slice(h * GDN_DK, (h + 1) * GDN_DK)
        qh = y[:, sl]
        qkv_out[:, sl] = (qh * lax.rsqrt(jnp.sum(qh * qh, axis=-1, keepdims=True) + EPS) * (GDN_DK ** -0.5)
                          ).astype(qkv_out.dtype)
        sl2 = slice(nq + h * GDN_DK, nq + (h + 1) * GDN_DK)
        kh = y[:, sl2]
        qkv_out[:, sl2] = (kh * lax.rsqrt(jnp.sum(kh * kh, axis=-1, keepdims=True) + EPS)).astype(qkv_out.dtype)
    qkv_out[:, 2 * nq:] = y[:, 2 * nq:].astype(qkv_out.dtype)
    m1 = m1_ref[...]
    lane = lax.broadcasted_iota(jnp.int32, m1.shape, 1)
    beta = _sigmoid(m1)
    xa = m1 + dt_ref[...]
    softplus = jnp.maximum(xa, 0.0) + jnp.log(1.0 + jnp.exp(-jnp.abs(xa)))
    decay = -jnp.exp(alog_ref[...]) * softplus
    ri = lax.broadcasted_iota(jnp.int32, (tm, tm), 0)
    ci = lax.broadcasted_iota(jnp.int32, (tm, tm), 1)
    ltri = jnp.where((ri >= ci) & (ri // GDN_CHUNK == ci // GDN_CHUNK), 1.0, 0.0).astype(BF16)
    d_hi = decay.astype(BF16)
    rem = decay - d_hi.astype(F32)
    d_mid = rem.astype(BF16)
    d_lo = (rem - d_mid.astype(F32)).astype(BF16)
    gcum = (jnp.dot(ltri, d_hi, preferred_element_type=F32) + jnp.dot(ltri, d_mid, preferred_element_type=F32)
            + jnp.dot(ltri, d_lo, preferred_element_type=F32))
    gb = jnp.where(lane < DECAY_LANE, beta, gcum)
    gb_out[...] = gb
    r = lax.broadcasted_iota(jnp.int32, (8, LANES), 0)
    c = lax.broadcasted_iota(jnp.int32, (8, LANES), 1)
    pick = (c == r + BETA_LANE).astype(F32)
    gbt_out[...] = lax.dot_general(pick, gb, (((1,), (1,)), ((), ())),
                                   preferred_element_type=F32, precision=HIGHEST)


def gdn_prep(zg, zm, conv_w, alog_v, dt_v, S, tm=512):
    T = zg.shape[0]
    tiles_per_seq = S // tm
    hb = tm // HALO
    m1_blk = 640 // LANES
    return pl.pallas_call(
        functools.partial(_gdn_prep_kernel, tm=tm, tiles_per_seq=tiles_per_seq),
        grid=(T // tm,),
        in_specs=[pl.BlockSpec((tm, GDN_QKV), lambda i: (i, 0)),
                  pl.BlockSpec((HALO, GDN_QKV), lambda i: (jnp.maximum(i * hb - 1, 0), 0)),
                  pl.BlockSpec((tm, LANES), lambda i: (i, m1_blk)),
                  pl.BlockSpec((GDN_CONV, GDN_QKV), lambda i: (0, 0)),
                  pl.BlockSpec((1, LANES), lambda i: (0, 0)),
                  pl.BlockSpec((1, LANES), lambda i: (0, 0))],
        out_specs=[pl.BlockSpec((tm, GDN_QKV), lambda i: (i, 0)),
                   pl.BlockSpec((tm, LANES), lambda i: (i, 0)),
                   pl.BlockSpec((8, tm), lambda i: (0, i))],
        out_shape=[jax.ShapeDtypeStruct((T, GDN_QKV), BF16),
                   jax.ShapeDtypeStruct((T, LANES), F32),
                   jax.ShapeDtypeStruct((8, T), F32)],
        compiler_params=_cparams(("parallel",)),
        name="gdn_prep",
    )(zg, zg, zm, conv_w, alog_v, dt_v)


def _dot3(a, b):
    a_hi = a.astype(BF16)
    a_lo = (a - a_hi.astype(F32)).astype(BF16)
    b_hi = b.astype(BF16)
    b_lo = (b - b_hi.astype(F32)).astype(BF16)
    return (jnp.dot(a_hi, b_hi, preferred_element_type=F32) + jnp.dot(a_hi, b_lo, preferred_element_type=F32)
            + jnp.dot(a_lo, b_hi, preferred_element_type=F32))


def _tri_inverse_all(a_list, eye, diag_blocks):
    ad = [jnp.where(diag_blocks, a, 0.0) for a in a_list]
    ao = [a - d for a, d in zip(a_list, ad)]
    a2 = [_dot(d, d) for d in ad]
    a4 = [_dot(x, x) for x in a2]
    a8 = [_dot(x, x) for x in a4]
    t = [_dot(eye - d, eye + x) for d, x in zip(ad, a2)]
    t = [_dot(y, eye + x) for y, x in zip(t, a4)]
    dinv = [_dot(y, eye + x) for y, x in zip(t, a8)]
    n = [_dot(d, o) for d, o in zip(dinv, ao)]
    n2 = [_dot(x, x) for x in n]
    t = [_dot(eye - x, eye + y) for x, y in zip(n, n2)]
    x0 = [_dot(y, d) for y, d in zip(t, dinv)]
    res = [eye - x - _dot3(a, x) for a, x in zip(a_list, x0)]
    return [x + _dot(x, r) for x, r in zip(x0, res)]


def _gdn_chunk_kernel(qkv_ref, gb_ref, gbt_ref, zg_ref, norm_ref, o_ref, state_scr, *, lb):
    C = GDN_CHUNK
    DK = GDN_DK
    H = GDN_HEADS
    nq = H * DK
    n_chunks = lb // C

    @pl.when(pl.program_id(1) == 0)
    def _():
        state_scr[...] = jnp.zeros(state_scr.shape, F32)

    ii = lax.broadcasted_iota(jnp.int32, (C, C), 0)
    jj = lax.broadcasted_iota(jnp.int32, (C, C), 1)
    lower = ii >= jj
    strict = ii > jj
    eye = (ii == jj).astype(F32)
    diag_blocks = (ii // 16) == (jj // 16)
    norm_w = norm_ref[...]

    items = [(c, h) for c in range(n_chunks) for h in range(H)]
    rows = lambda c: slice(c * C, (c + 1) * C)
    q = [qkv_ref[rows(c), h * DK:(h + 1) * DK] for c, h in items]
    k = [qkv_ref[rows(c), nq + h * DK:nq + (h + 1) * DK] for c, h in items]
    v = [qkv_ref[rows(c), 2 * nq + h * DK:2 * nq + (h + 1) * DK] for c, h in items]
    beta = [jnp.broadcast_to(gb_ref[rows(c), BETA_LANE + h:BETA_LANE + h + 1], (C, DK)) for c, h in items]
    gc = [jnp.broadcast_to(gb_ref[rows(c), DECAY_LANE + h:DECAY_LANE + h + 1], (C, DK)) for c, h in items]
    gr = [jnp.broadcast_to(gbt_ref[4 + h:5 + h, rows(c)], (C, C)) for c, h in items]
    decay = [jnp.exp(jnp.where(lower, x[:, :C] - y, NEG)) for x, y in zip(gc, gr)]
    eg = [jnp.exp(x) for x in gc]
    kb = [x * b for x, b in zip(k, beta)]
    a = [jnp.where(strict, _dot_nt(x, y) * d, 0.0) for x, y, d in zip(kb, k, decay)]
    t_inv = _tri_inverse_all(a, eye, diag_blocks)
    uw = [_dot(t, jnp.concatenate([x * b, y * e], axis=1))
          for t, x, b, y, e in zip(t_inv, v, beta, kb, eg)]
    intra = [_dot_nt(x, y) * d for x, y, d in zip(q, k, decay)]
    g_last = [x[C - 1:C, :] for x in gc]
    k_dec = [x * jnp.exp(gl - g) for x, gl, g in zip(k, g_last, gc)]
    wq = [jnp.concatenate([x[:, DK:], y * e], axis=0) for x, y, e in zip(uw, q, eg)]

    states = [state_scr[h] for h in range(H)]
    for c in range(n_chunks):
        idx = [c * H + h for h in range(H)]
        ws = [_dot(wq[i], states[h]) for h, i in enumerate(idx)]
        v_new = [uw[i][:, :DK] - y[:C] for i, y in zip(idx, ws)]
        o = [y[C:] + _dot(intra[i], vn) for i, y, vn in zip(idx, ws, v_new)]
        states = [s * jnp.exp(g_last[i]) + _dot_tn(k_dec[i], vn) for s, i, vn in zip(states, idx, v_new)]
        for h in range(H):
            on = o[h] * lax.rsqrt(jnp.mean(o[h] * o[h], axis=-1, keepdims=True) + EPS) * norm_w
            o_ref[rows(c), h * DK:(h + 1) * DK] = (on * _silu(zg_ref[rows(c), h * DK:(h + 1) * DK])
                                                   ).astype(o_ref.dtype)
    for h in range(H):
        state_scr[h] = states[h]


def gdn_chunk(qkv, gb, gbt, zg, norm_w, B, S, lb=512):
    T = qkv.shape[0]
    nsb = S // lb
    VW = GDN_HEADS * GDN_DV
    zg_blk = GDN_QKV // VW
    return pl.pallas_call(
        functools.partial(_gdn_chunk_kernel, lb=lb),
        grid=(B, nsb),
        in_specs=[pl.BlockSpec((lb, GDN_QKV), lambda b, s: (b * nsb + s, 0)),
                  pl.BlockSpec((lb, LANES), lambda b, s: (b * nsb + s, 0)),
                  pl.BlockSpec((8, lb), lambda b, s: (0, b * nsb + s)),
                  pl.BlockSpec((lb, VW), lambda b, s: (b * nsb + s, zg_blk)),
                  pl.BlockSpec((1, GDN_DV), lambda b, s: (0, 0))],
        out_specs=pl.BlockSpec((lb, VW), lambda b, s: (b * nsb + s, 0)),
        out_shape=jax.ShapeDtypeStruct((T, VW), BF16),
        scratch_shapes=[pltpu.VMEM((GDN_HEADS, GDN_DK, GDN_DV), F32)],
        compiler_params=_cparams(("parallel", "arbitrary")),
        name="gdn_chunk",
    )(qkv, gb, gbt, zg, norm_w)


def _compress_kernel(xl_ref, xh_ref, pea_ref, peb_ref, w1a_ref, w1b_ref, w2_ref, o_ref, p1_scr, p2_scr,
                     *, per_step):
    kc = pl.program_id(1)
    nr = p1_scr.shape[0]
    width = xl_ref.shape[1] + xh_ref.shape[1]

    @pl.when(kc == 0)
    def _():
        p1_scr[...] = jnp.zeros(p1_scr.shape, F32)
        p2_scr[...] = jnp.zeros(p2_scr.shape, F32)

    for t in range(per_step):
        rows = pl.ds(kc * per_step + t, nr, stride=CMP_STRIDE)
        x = jnp.concatenate([xl_ref[rows, :], xh_ref[rows, :]], axis=1)
        cols = slice(t * width, (t + 1) * width)
        p1_scr[...] += _dot(x + pea_ref[:, cols], w1a_ref[cols, :])
        p2_scr[...] += _dot(x + peb_ref[:, cols], w1b_ref[cols, :])

    @pl.when(kc == pl.num_programs(1) - 1)
    def _():
        p2 = p2_scr[...]
        rows = p2.shape[0]
        hid = p1_scr[...] + pltpu.roll(p2, rows - 1, 0)
        c0 = math.sqrt(2.0 / math.pi)
        act = 0.5 * hid * (1.0 + jnp.tanh(c0 * (hid + 0.044715 * (hid * hid * hid))))
        o_ref[0] = _dot(act, w2_ref[...])


def nsa_compress(x, pe_a, pe_b, w1a, w1b, w2, B, S, per_step=2):
    nr = S // CMP_STRIDE
    width = x.shape[1]
    kchunk = per_step * width
    KW = CMP_STRIDE * width
    HW = NSA_GROUPS * CMP_HIDDEN
    OW = NSA_GROUPS * NSA_DIM
    return pl.pallas_call(
        functools.partial(_compress_kernel, per_step=per_step),
        grid=(B, KW // kchunk),
        in_specs=[pl.BlockSpec((S, LANES), lambda b, k: (b, 0)),
                  pl.BlockSpec((S, LANES), lambda b, k: (b, 1)),
                  pl.BlockSpec((1, kchunk), lambda b, k: (0, k)),
                  pl.BlockSpec((1, kchunk), lambda b, k: (0, k)),
                  pl.BlockSpec((kchunk, HW), lambda b, k: (k, 0)),
                  pl.BlockSpec((kchunk, HW), lambda b, k: (k, 0)),
                  pl.BlockSpec((HW, OW), lambda b, k: (0, 0))],
        out_specs=pl.BlockSpec((1, nr, OW), lambda b, k: (b, 0, 0)),
        out_shape=jax.ShapeDtypeStruct((B, nr, OW), F32),
        scratch_shapes=[pltpu.VMEM((nr, HW), F32), pltpu.VMEM((nr, HW), F32)],
        compiler_params=_cparams(("parallel", "arbitrary")),
        name="nsa_compress",
    )(x, x, pe_a, pe_b, w1a, w1b, w2)


def _slope(h):
    return float(2.0 ** (-8.0 * (h + 1) / NSA_HEADS))


SEL_TK = 2 * SEL_BLOCK
SEL_NT = 4
SEL_TQ = 256


def _gate_expand(branch):
    e = np.zeros((LANES, NSA_HEADS * NSA_DIM), np.float32)
    for h in range(NSA_HEADS):
        e[3 * h + branch, h * NSA_DIM:(h + 1) * NSA_DIM] = 1.0
    return jnp.asarray(e).astype(BF16)


def _gate_matrix(zg, expand):
    g = _sigmoid(zg)
    g_hi = g.astype(BF16)
    rem = g - g_hi.astype(F32)
    g_mid = rem.astype(BF16)
    g_lo = (rem - g_mid.astype(F32)).astype(BF16)
    dot = lambda a: jnp.dot(a, expand, preferred_element_type=F32)
    return (dot(g_hi) + dot(g_mid)) + dot(g_lo)


def _value_blocks(vg):
    ones = jnp.ones_like(vg)
    return jnp.concatenate([vg, ones], axis=1), jnp.concatenate([ones, vg], axis=1)


def _nsa_cmp_kernel(q_ref, kc_ref, vc_ref, zg_ref, ge_ref, ov_ref, o_ref, sel_ref, flag_ref, work_scr,
                    *, tq, n_sel):
    i = pl.program_id(1)
    n_tiles = n_sel * SEL_BLOCK // SEL_TK
    blk_tile = (lax.broadcasted_iota(jnp.int32, (n_sel, n_tiles), 0) * SEL_BLOCK // SEL_TK
                == lax.broadcasted_iota(jnp.int32, (n_sel, n_tiles), 1))
    to_tile = jnp.where(blk_tile, 1.0, 0.0).astype(BF16)
    ncmp = kc_ref.shape[1]
    D = NSA_DIM
    qpos = i * tq + lax.broadcasted_iota(jnp.int32, (tq, 1), 0)
    nidx = lax.broadcasted_iota(jnp.int32, (1, ncmp), 1)
    valid = (nidx * CMP_STRIDE + (CMP_BLOCK - 1)) <= qpos
    any_valid = (qpos >= CMP_BLOCK - 1).astype(F32)
    centre_rel = (nidx * CMP_STRIDE - i * tq).astype(F32) + 0.5 * (CMP_BLOCK - 1)
    gate_mat = _gate_matrix(zg_ref[...], ge_ref[...])
    kc = kc_ref[0].astype(BF16)
    vc = vc_ref[0].astype(BF16)
    ov_t = ov_ref[...]
    jf = lax.broadcasted_iota(jnp.int32, (n_sel, 1), 0).astype(F32)
    qblk = ((i * tq + lax.broadcasted_iota(jnp.int32, (1, tq), 1)) // SEL_BLOCK).astype(F32)
    forced = (jf == 0.0) | (jf == qblk) | (jf == qblk - 1.0)
    causal_blk = jf <= qblk
    heads = range(NSA_HPG)
    groups = range(NSA_GROUPS)
    lane_half = lax.broadcasted_iota(jnp.int32, (tq, LANES), 1) < D
    assert ncmp % LANES == 0

    def scores(width):
        for g in groups:
            kg = kc[:width, g * D:(g + 1) * D]
            vg = vc[:width, g * D:(g + 1) * D]
            hs = [g * NSA_HPG + r for r in heads]
            s = [_dot_nt(q_ref[:, h * D:(h + 1) * D], kg) + (_slope(h) * LOG2E) * centre_rel[:, :width] for h in hs]
            s = [jnp.where(valid[:, :width], x, NEG) for x in s]
            v_even, v_odd = _value_blocks(vg)
            p, o = [], []
            for r, h in enumerate(hs):
                tiles = [s[r][:, c * LANES:(c + 1) * LANES] for c in range(width // LANES)]
                tile_max = functools.reduce(jnp.maximum, tiles)
                m = jnp.maximum(jnp.full((tq, LANES), NEG, F32), jnp.max(tile_max, axis=1, keepdims=True))
                e_tiles = [jnp.exp2(t - m) for t in tiles]
                pv = jnp.dot(jnp.concatenate(e_tiles, axis=1).astype(BF16), v_odd if r % 2 else v_even,
                             preferred_element_type=F32)
                rolled = pltpu.roll(pv, D, 1)
                row_sum = jnp.where(lane_half, pv, rolled) if r % 2 else jnp.where(lane_half, rolled, pv)
                norm = any_valid / row_sum
                o.append(pv * norm)
                p.append([t * norm for t in e_tiles])
            for t in range(NSA_HPG // 2):
                blk = slice((g * NSA_HPG // 2 + t) * LANES, (g * NSA_HPG // 2 + t + 1) * LANES)
                o_ref[:, blk] = gate_mat[:, blk] * jnp.where(lane_half, o[2 * t], o[2 * t + 1])
            psum = jnp.concatenate([(a + b) + (c + d) for a, b, c, d in zip(*p)], axis=1)
            p_hi = psum.astype(BF16)
            rem = psum - p_hi.astype(F32)
            p_mid = rem.astype(BF16)
            p_lo = (rem - p_mid.astype(F32)).astype(BF16)
            ov_w = ov_t[:, :width]
            imp_t = (_dot_nt(ov_w, p_hi) + _dot_nt(ov_w, p_mid)) + _dot_nt(ov_w, p_lo)
            work_scr[g] = jnp.where(forced, BIG, jnp.where(causal_blk, imp_t, NEG))

    n_valid = jnp.maximum((i * tq + tq - CMP_BLOCK) // CMP_STRIDE + 1, 1)
    tiles_needed = jnp.minimum((n_valid + LANES - 1) // LANES, ncmp // LANES)
    for nt in range(1, ncmp // LANES + 1):
        pl.when(tiles_needed == nt)(functools.partial(scores, nt * LANES))
    work = [work_scr[g] for g in groups]
    selm = [jnp.zeros((n_sel, tq), F32) for _ in groups]
    for _ in range(min(SEL_TOPN, n_sel)):
        mx = [jnp.max(w, axis=0, keepdims=True) for w in work]
        first = [jnp.min(jnp.where(w == m, jf, float(n_sel)), axis=0, keepdims=True) for w, m in zip(work, mx)]
        pick = [jf == f for f in first]
        selm = [jnp.where(pk, 1.0, sm) for pk, sm in zip(pick, selm)]
        work = [jnp.where(pk, -jnp.inf, w) for pk, w in zip(pick, work)]
    for g in groups:
        selb = selm[g].T.astype(BF16)
        sel_ref[:, g * n_sel:(g + 1) * n_sel] = selb
        tile_hits = jnp.dot(selb, to_tile, preferred_element_type=F32)
        for part in range(tq // SEL_TQ):
            hits = jnp.max(tile_hits[part * SEL_TQ:(part + 1) * SEL_TQ], axis=0, keepdims=True)
            flag_ref[part, g:g + 1, :] = (hits > 0.5).astype(jnp.int32)


def nsa_cmp(q, k_cmp, v_cmp, zg, overlap, B, S, tq=SEL_TQ):
    T = q.shape[0]
    tq = min(tq, S)
    nq = S // tq
    parts = tq // SEL_TQ
    n_sel = S // SEL_BLOCK
    ncmp = k_cmp.shape[1]
    QW = NSA_HEADS * NSA_DIM
    KW = NSA_GROUPS * NSA_DIM
    return pl.pallas_call(
        functools.partial(_nsa_cmp_kernel, tq=tq, n_sel=n_sel),
        grid=(B, nq),
        in_specs=[pl.BlockSpec((tq, QW), lambda b, i: (b * nq + i, 0)),
                  pl.BlockSpec((1, ncmp, KW), lambda b, i: (b, 0, 0)),
                  pl.BlockSpec((1, ncmp, KW), lambda b, i: (b, 0, 0)),
                  pl.BlockSpec((tq, LANES), lambda b, i: (b * nq + i, 0)),
                  pl.BlockSpec((LANES, QW), lambda b, i: (0, 0)),
                  pl.BlockSpec((n_sel, ncmp), lambda b, i: (0, 0))],
        out_specs=[pl.BlockSpec((tq, QW), lambda b, i: (b * nq + i, 0)),
                   pl.BlockSpec((tq, NSA_GROUPS * n_sel), lambda b, i: (b * nq + i, 0)),
                   pl.BlockSpec((parts, NSA_GROUPS, S // SEL_TK), lambda b, i: (b * nq + i, 0, 0))],
        out_shape=[jax.ShapeDtypeStruct((T, QW), F32),
                   jax.ShapeDtypeStruct((T, NSA_GROUPS * n_sel), BF16),
                   jax.ShapeDtypeStruct((B * nq * parts, NSA_GROUPS, S // SEL_TK), jnp.int32)],
        scratch_shapes=[pltpu.VMEM((NSA_GROUPS, n_sel, tq), F32)],
        compiler_params=_cparams(("parallel", "parallel")),
        name="nsa_cmp",
    )(q, k_cmp, v_cmp, zg, _gate_expand(0), overlap)


def _nsa_sel_kernel(flags_ref, q_ref, k_ref, v_ref, sel_ref, zg_ref, ge_sel_ref, ge_win_ref,
                    kw0_ref, kw1_ref, kw2_ref, vw0_ref, vw1_ref, vw2_ref, prev_ref, o_ref,
                    list_smem, m_scr, acc_scr, *, tq, n_sel, nq):
    b = pl.program_id(0)
    i = pl.program_id(1)
    D = NSA_DIM
    n_tiles = n_sel * SEL_BLOCK // SEL_TK
    n_causal = (i * tq + tq - 1) // SEL_TK + 1
    ks_w = SEL_NT * SEL_TK
    qpos = i * tq + lax.broadcasted_iota(jnp.int32, (tq, 1), 0)
    lane_t = lax.broadcasted_iota(jnp.int32, (1, SEL_TK), 1)
    blk_iota = lax.broadcasted_iota(jnp.int32, (n_sel, ks_w), 0)
    zg = zg_ref[...]
    gate_sel = _gate_matrix(zg, ge_sel_ref[...])
    gate_win = _gate_matrix(zg, ge_win_ref[...])
    lane_half = lax.broadcasted_iota(jnp.int32, (tq, LANES), 1) < D

    nback = WINDOW // tq
    tkw = (nback + 1) * tq
    k_win = jnp.concatenate([kw0_ref[...], kw1_ref[...], kw2_ref[...]], axis=0)
    v_win = jnp.concatenate([vw0_ref[...], vw1_ref[...], vw2_ref[...]], axis=0)
    qrel = lax.broadcasted_iota(jnp.int32, (tq, tkw), 0)
    krel_w = lax.broadcasted_iota(jnp.int32, (tq, tkw), 1) - nback * tq
    dw = qrel - krel_w
    wvalid = jnp.where(dw >= 0, jnp.where(dw < WINDOW, krel_w + i * tq, -1), -1) >= 0
    krow_w = (lax.broadcasted_iota(jnp.int32, (1, tkw), 1) - nback * tq).astype(F32)

    for g in range(NSA_GROUPS):
        base = ((b * nq + i) * NSA_GROUPS + g) * n_tiles

        def scan(j, n, base=base):
            list_smem[n] = j
            return n + (flags_ref[base + j] != 0).astype(jnp.int32)

        count = lax.fori_loop(0, n_causal, scan, jnp.int32(0))

        for r in range(NSA_HPG):
            m_scr[r] = jnp.full(m_scr.shape[1:], NEG, F32)
            acc_scr[r] = jnp.zeros(acc_scr.shape[1:], F32)
        qs = [q_ref[:, (g * NSA_HPG + r) * D:(g * NSA_HPG + r + 1) * D] for r in range(NSA_HPG)]
        sel_g = sel_ref[:, g * n_sel:(g + 1) * n_sel]

        def step(st, carry, g=g, count=count, qs=qs, sel_g=sel_g):
            k_parts, v_parts, kpos_parts, kblk_parts = [], [], [], []
            for s in range(SEL_NT):
                idx = st * SEL_NT + s
                j = list_smem[jnp.minimum(idx, count - 1)]
                start = pl.multiple_of(j * SEL_TK, SEL_TK)
                k_parts.append(k_ref[pl.ds(start, SEL_TK), g * D:(g + 1) * D])
                v_parts.append(v_ref[pl.ds(start, SEL_TK), g * D:(g + 1) * D])
                tid = jnp.where(idx < count, j, -1)
                kpos_parts.append(tid * SEL_TK + lane_t)
                kblk_parts.append(tid * (SEL_TK // SEL_BLOCK) + lane_t // SEL_BLOCK)
            k = jnp.concatenate(k_parts, axis=0)
            v = jnp.concatenate(v_parts, axis=0)
            kpos = jnp.concatenate(kpos_parts, axis=1)
            kblk = jnp.concatenate(kblk_parts, axis=1)
            expand = jnp.where(blk_iota == kblk, 1.0, 0.0).astype(BF16)
            picked = jnp.dot(sel_g, expand, preferred_element_type=F32)
            allowed = jnp.where(kpos <= qpos, picked, 0.0) > 0.5
            krel = (kpos - i * tq).astype(F32)
            hs = range(NSA_HPG)
            s_ = [_dot_nt(qs[r], k) + (_slope(g * NSA_HPG + r) * LOG2E) * krel for r in hs]
            s_ = [jnp.where(allowed, x, NEG) for x in s_]
            v_even, v_odd = _value_blocks(v)
            new_m, pv = [], []
            for r in hs:
                tiles = [s_[r][:, c * LANES:(c + 1) * LANES] for c in range(ks_w // LANES)]
                tile_max = functools.reduce(jnp.maximum, tiles)
                m_new = jnp.maximum(m_scr[r], jnp.max(tile_max, axis=1, keepdims=True))
                p = jnp.concatenate([jnp.exp2(t - m_new) for t in tiles], axis=1).astype(BF16)
                pv.append(jnp.dot(p, v_odd if r % 2 else v_even, preferred_element_type=F32))
                new_m.append(m_new)
            for r in hs:
                acc_scr[r] = jnp.exp2(m_scr[r] - new_m[r]) * acc_scr[r] + pv[r]
                m_scr[r] = new_m[r]
            return carry

        lax.fori_loop(0, (count + SEL_NT - 1) // SEL_NT, step, jnp.int32(0))

        o = [acc_scr[r] * (1.0 / pltpu.roll(acc_scr[r], D, 1)) for r in range(NSA_HPG)]

        kg = k_win[:, g * D:(g + 1) * D]
        w_even, w_odd = _value_blocks(v_win[:, g * D:(g + 1) * D])
        sw = [_dot_nt(qs[r], kg) + (_slope(g * NSA_HPG + r) * LOG2E) * krow_w for r in range(NSA_HPG)]
        sw = [jnp.where(wvalid, x, NEG) for x in sw]
        ow = []
        for r, x in enumerate(sw):
            tiles = [x[:, c * LANES:(c + 1) * LANES] for c in range(tkw // LANES)]
            tile_max = functools.reduce(jnp.maximum, tiles)
            m = jnp.maximum(jnp.full((tq, LANES), NEG, F32), jnp.max(tile_max, axis=1, keepdims=True))
            e = jnp.concatenate([jnp.exp2(t - m) for t in tiles], axis=1).astype(BF16)
            pv = jnp.dot(e, w_odd if r % 2 else w_even, preferred_element_type=F32)
            ow.append(pv * (1.0 / pltpu.roll(pv, D, 1)))

        for t in range(NSA_HPG // 2):
            blk = slice((g * NSA_HPG // 2 + t) * LANES, (g * NSA_HPG // 2 + t + 1) * LANES)
            o_ref[:, blk] = (prev_ref[:, blk]
                             + gate_sel[:, blk] * jnp.where(lane_half, o[2 * t], o[2 * t + 1])
                             + gate_win[:, blk] * jnp.where(lane_half, ow[2 * t], ow[2 * t + 1])).astype(o_ref.dtype)


def nsa_sel_win(q, ks, vs, kw, vw, sel, flags, zg, prev, B, S, tq=SEL_TQ):
    T = q.shape[0]
    assert WINDOW % tq == 0 and WINDOW // tq == 2
    nq = S // tq
    n_sel = S // SEL_BLOCK
    QW = NSA_HEADS * NSA_DIM
    KW = NSA_GROUPS * NSA_DIM
    qmap = lambda b, i, fl: (b * nq + i, 0)
    kmap = lambda b, i, fl: (b, 0)
    const = lambda b, i, fl: (0, 0)
    back = lambda d: (lambda b, i, fl: (b * nq + jnp.maximum(i - d, 0), 0))
    wspecs = [pl.BlockSpec((tq, KW), back(2)), pl.BlockSpec((tq, KW), back(1)), pl.BlockSpec((tq, KW), back(0))]
    grid_spec = pltpu.PrefetchScalarGridSpec(
        num_scalar_prefetch=1,
        grid=(B, nq),
        in_specs=[pl.BlockSpec((tq, QW), qmap),
                  pl.BlockSpec((S, KW), kmap),
                  pl.BlockSpec((S, KW), kmap),
                  pl.BlockSpec((tq, NSA_GROUPS * n_sel), qmap),
                  pl.BlockSpec((tq, LANES), qmap),
                  pl.BlockSpec((LANES, QW), const),
                  pl.BlockSpec((LANES, QW), const)] + wspecs + wspecs
                 + [pl.BlockSpec((tq, QW), qmap)],
        out_specs=pl.BlockSpec((tq, QW), qmap),
        scratch_shapes=[pltpu.SMEM((S // SEL_TK,), jnp.int32),
                        pltpu.VMEM((NSA_HPG, tq, LANES), F32),
                        pltpu.VMEM((NSA_HPG, tq, LANES), F32)],
    )
    return pl.pallas_call(
        functools.partial(_nsa_sel_kernel, tq=tq, n_sel=n_sel, nq=nq),
        grid_spec=grid_spec,
        out_shape=jax.ShapeDtypeStruct((T, QW), BF16),
        compiler_params=_cparams(("parallel", "arbitrary")),
        name="nsa_sel_win",
    )(flags.reshape(-1), q, ks, vs, sel, zg, _gate_expand(1), _gate_expand(2), kw, kw, kw, vw, vw, vw, prev)


def _rot_half_cols(w):
    half = w.shape[-1] // 2
    return jnp.concatenate([-w[..., half:], w[..., :half]], axis=-1)


def _even_weights(w_in, w_uq, w_ukv):
    D = w_in.shape[0]
    o = 0
    cuts = {}
    for name, n in (("cq", MLA_Q_RANK), ("ckv", MLA_KV_RANK), ("kr", MLA_ROPE), ("zq", 512), ("zk", 512),
                    ("zv", 512), ("zg", 512), ("zb", GDN_HEADS), ("za", GDN_HEADS)):
        cuts[name] = w_in[:, o:o + n]
        o += n
    z = lambda n: jnp.zeros((D, n), F32)
    misc1 = jnp.concatenate([z(MLA_NOPE), cuts["kr"], cuts["zb"], cuts["za"],
                             z(LANES - MLA_NOPE - MLA_ROPE - 2 * GDN_HEADS)], axis=1)
    misc2 = jnp.concatenate([z(MLA_NOPE), _rot_half_cols(cuts["kr"]), z(LANES - MLA_NOPE - MLA_ROPE)], axis=1)
    w_even = jnp.concatenate([cuts["cq"], cuts["ckv"], misc1, misc2,
                              cuts["zq"], cuts["zk"], cuts["zv"], cuts["zg"]], axis=1).astype(BF16)
    qd = MLA_NOPE + MLA_ROPE
    wq3 = w_uq.reshape(MLA_Q_RANK, MLA_HEADS, qd)
    zq = jnp.zeros((MLA_Q_RANK, MLA_HEADS, MLA_HB - qd), F32)
    wq = jnp.concatenate([wq3, zq], axis=2).reshape(MLA_Q_RANK, MLA_HEADS * MLA_HB).astype(BF16)
    wqr = jnp.concatenate([jnp.zeros((MLA_Q_RANK, MLA_HEADS, MLA_NOPE), F32),
                           _rot_half_cols(wq3[:, :, MLA_NOPE:]), zq], axis=2)
    wqr = wqr.reshape(MLA_Q_RANK, MLA_HEADS * MLA_HB).astype(BF16)
    wkv3 = w_ukv.reshape(MLA_KV_RANK, MLA_HEADS, MLA_NOPE + MLA_V)
    wk = jnp.concatenate([wkv3[:, :, :MLA_NOPE], jnp.zeros((MLA_KV_RANK, MLA_HEADS, MLA_HB - MLA_NOPE), F32)],
                         axis=2).reshape(MLA_KV_RANK, MLA_HEADS * MLA_HB).astype(BF16)
    wv4 = wkv3[:, :, MLA_NOPE:].reshape(MLA_KV_RANK, MLA_HEADS // 2, 2, MLA_V)
    zv = jnp.zeros((MLA_KV_RANK, MLA_HEADS // 2, MLA_V), F32)
    wv = jnp.stack([jnp.concatenate([wv4[:, :, 0], zv], axis=2), jnp.concatenate([zv, wv4[:, :, 1]], axis=2)],
                   axis=2).reshape(MLA_KV_RANK, MLA_HEADS * MLA_HB).astype(BF16)
    return w_even, wq, wqr, wk, wv


def _rope_tables(S):
    half = MLA_ROPE // 2
    inv = ROPE_BASE ** (-jnp.arange(half, dtype=F32) / half)
    ang = jnp.arange(S, dtype=F32)[:, None] * inv[None, :]
    cos = jnp.cos(ang)
    sin = jnp.sin(ang)
    pad = jnp.zeros((S, LANES - MLA_NOPE - MLA_ROPE), F32)
    cos_t = jnp.concatenate([jnp.ones((S, MLA_NOPE), F32), cos, cos, pad], axis=1)
    sin_t = jnp.concatenate([jnp.zeros((S, MLA_NOPE), F32), sin, sin, pad], axis=1)
    return cos_t, sin_t


def _lane_vec(vals, start):
    return jnp.zeros((1, LANES), F32).at[0, start:start + vals.shape[0]].set(vals)


def even_mixer_layer(h, B, S, attn_norm, w_in, q_norm, kv_norm, w_uq, w_ukv, conv_w, a_log, dt_bias,
                     gdn_norm, w_out, tables):
    w_even, wq, wqr, wk, wv = _even_weights(w_in, w_uq, w_ukv)
    zm, zg = rms_matmul(h, attn_norm, w_even, (896, 2048))
    cos_t, sin_t = tables
    q, k, v = mla_prep(zm, q_norm.reshape(1, -1), kv_norm.reshape(1, -1), wq, wqr, wk, wv, cos_t, sin_t, S)
    o_mla = mla_flash(q, k, v, B, S, tq=min(512, S))
    qkv, gb, gbt = gdn_prep(zg, zm, conv_w, _lane_vec(a_log, DECAY_LANE), _lane_vec(dt_bias, DECAY_LANE), S)
    o_gdn = gdn_chunk(qkv, gb, gbt, zg, gdn_norm.reshape(1, -1), B, S)
    nm = MLA_HEADS * MLA_V
    return [o_mla, o_gdn], [w_out[:nm].astype(BF16), w_out[nm:].astype(BF16)]


def _compress_weights(pe, w1, w2):
    G, D = NSA_GROUPS, NSA_DIM
    eye = jnp.eye(G, dtype=F32)
    w1r = w1.reshape(CMP_BLOCK, D, CMP_HIDDEN)

    def expand(wpart):
        return jnp.einsum('ldh,gk->lgdkh', wpart, eye).reshape(CMP_STRIDE * G * D, G * CMP_HIDDEN).astype(BF16)

    def pe_vec(p):
        return jnp.broadcast_to(p[:, None, :], (CMP_STRIDE, G, D)).reshape(1, CMP_STRIDE * G * D)

    w2e = jnp.einsum('hd,gk->ghkd', w2, eye).reshape(G * CMP_HIDDEN, G * D).astype(BF16)
    return (pe_vec(pe[:CMP_STRIDE]), pe_vec(pe[CMP_STRIDE:]), expand(w1r[:CMP_STRIDE]),
            expand(w1r[CMP_STRIDE:]), w2e)


def _overlap_matrix(S):
    nr = S // CMP_STRIDE
    n_sel = S // SEL_BLOCK
    n = np.arange(nr)[:, None]
    j = np.arange(n_sel)[None, :]
    start = n * CMP_STRIDE
    ov = (start <= j * SEL_BLOCK + SEL_BLOCK - 1) & (start + CMP_BLOCK - 1 >= j * SEL_BLOCK)
    ov = ov & (n < nr - 1)
    return jnp.asarray(ov.T.astype(np.float32)).astype(BF16)


def odd_mixer_layer(h, B, S, attn_norm, w_in, pe_k, w1_k, w2_k, pe_v, w1_v, w2_v, w_out):
    D = w_in.shape[0]
    n_g = 3 * NSA_HEADS
    qw = NSA_HEADS * NSA_DIM
    w_odd = jnp.concatenate([w_in[:, :qw] * (NSA_DIM ** -0.5 * LOG2E), w_in[:, qw:],
                             jnp.zeros((D, LANES - n_g), F32)], axis=1).astype(BF16)
    kvw = NSA_GROUPS * NSA_DIM
    q, kc, vc, ks, vs, kw, vw, zg = rms_matmul(
        h, attn_norm, w_odd, (qw,) + (kvw,) * 6 + (LANES,),
        dtypes=(BF16, F32, F32, BF16, BF16, BF16, BF16, F32))
    k_cmp = nsa_compress(kc, *_compress_weights(pe_k, w1_k, w2_k), B, S)
    v_cmp = nsa_compress(vc, *_compress_weights(pe_v, w1_v, w2_v), B, S)
    o1, sel, flags = nsa_cmp(q, k_cmp, v_cmp, zg, _overlap_matrix(S), B, S)
    o3 = nsa_sel_win(q, ks, vs, kw, vw, sel, flags, zg, o1, B, S)
    return [o3], [w_out.astype(BF16)]


def kernel(x, ev_attn_norm, ev_w_in, ev_q_norm, ev_kv_norm, ev_w_uq, ev_w_ukv, ev_conv_w, ev_a_log, ev_dt_bias, ev_gdn_norm, ev_w_out, od_attn_norm, od_w_in, od_pe_k, od_w1_k, od_w2_k, od_pe_v, od_w1_v, od_w2_v, od_w_out, ffn_norm, ffn_w_gate, ffn_w_up, ffn_w_down, final_norm):
    B, S, D = x.shape
    depth = ffn_norm.shape[0]
    h = x.reshape(B * S, D)
    tables = _rope_tables(S)
    wg_all, wu_all, wd_all = ffn_w_gate.astype(BF16), ffn_w_up.astype(BF16), ffn_w_down.astype(BF16)
    for layer in range(depth):
        i = layer // 2
        if layer % 2 == 0:
            mix, w_mix = even_mixer_layer(h, B, S, ev_attn_norm[i], ev_w_in[i], ev_q_norm[i], ev_kv_norm[i],
                                          ev_w_uq[i], ev_w_ukv[i], ev_conv_w[i], ev_a_log[i], ev_dt_bias[i],
                                          ev_gdn_norm[i], ev_w_out[i], tables)
        else:
            mix, w_mix = odd_mixer_layer(h, B, S, od_attn_norm[i], od_w_in[i], od_pe_k[i], od_w1_k[i], od_w2_k[i],
                                         od_pe_v[i], od_w1_v[i], od_w2_v[i], od_w_out[i])
        h = proj_ffn(mix, w_mix, h, ffn_norm[layer], wg_all[layer], wu_all[layer], wd_all[layer],
                     final_g=final_norm if layer == depth - 1 else None)
    return h.reshape(B, S, D)
```

```python
import functools
import math

import jax
import jax.numpy as jnp
import numpy as np
from jax import lax
from jax.experimental import pallas as pl
from jax.experimental.pallas import tpu as pltpu

F32 = jnp.float32
BF16 = jnp.bfloat16

EPS = 1e-6
NEG = -1e30
BIG = 1e30
LANES = 128

MLA_HEADS = 8
MLA_Q_RANK = 384
MLA_KV_RANK = 256
MLA_NOPE = 64
MLA_ROPE = 32
MLA_V = 64
ROPE_BASE = 10000.0
GDN_HEADS = 4
GDN_DK = 128
GDN_DV = 128
GDN_CONV = 4
GDN_CHUNK = 64
NSA_HEADS = 16
NSA_GROUPS = 4
NSA_HPG = 4
NSA_DIM = 64
CMP_BLOCK = 32
CMP_STRIDE = 16
CMP_HIDDEN = 256
SEL_BLOCK = 64
SEL_TOPN = 16
WINDOW = 512

LOG2E = math.log2(math.e)
VMEM_LIMIT = 56 * 1024 * 1024
HIGHEST = lax.Precision.HIGHEST


def _cparams(sem):
    return pltpu.CompilerParams(dimension_semantics=sem, vmem_limit_bytes=VMEM_LIMIT)


def _dot(a, b):
    return jnp.dot(a.astype(BF16), b.astype(BF16), preferred_element_type=F32)


def _dot_nt(a, b):
    return lax.dot_general(a.astype(BF16), b.astype(BF16), (((1,), (1,)), ((), ())),
                           preferred_element_type=F32)


def _dot_tn(a, b):
    return lax.dot_general(a.astype(BF16), b.astype(BF16), (((0,), (0,)), ((), ())),
                           preferred_element_type=F32)


def _dot_f32(a, b):
    return jnp.dot(a, b, preferred_element_type=F32, precision=HIGHEST)


def _rms(x, g):
    var = jnp.mean(x * x, axis=-1, keepdims=True)
    return x * lax.rsqrt(var + EPS) * g


def _silu(x):
    return x * (1.0 / (1.0 + jnp.exp(-x)))


def _sigmoid(x):
    return 1.0 / (1.0 + jnp.exp(-x))


def _rms_matmul_kernel(x_ref, g_ref, w_ref, *out_refs, splits):
    xn = _rms(x_ref[...], g_ref[...])
    acc = _dot(xn, w_ref[...])
    off = 0
    for o_ref, n in zip(out_refs, splits):
        o_ref[...] = acc[:, off:off + n].astype(o_ref.dtype)
        off += n


def rms_matmul(x, g, w, splits, tm=512, dtypes=None):
    T, K = x.shape
    N = w.shape[1]
    assert sum(splits) == N and T % tm == 0
    dtypes = dtypes or (F32,) * len(splits)
    return pl.pallas_call(
        functools.partial(_rms_matmul_kernel, splits=splits),
        grid=(T // tm,),
        in_specs=[pl.BlockSpec((tm, K), lambda i: (i, 0)),
                  pl.BlockSpec((1, K), lambda i: (0, 0)),
                  pl.BlockSpec((K, N), lambda i: (0, 0))],
        out_specs=[pl.BlockSpec((tm, n), lambda i: (i, 0)) for n in splits],
        out_shape=[jax.ShapeDtypeStruct((T, n), dt) for n, dt in zip(splits, dtypes)],
        compiler_params=_cparams(("parallel",)),
        name="rms_matmul",
    )(x, g.reshape(1, K), w)


def _proj_ffn_kernel(*refs, n_in, chunks, final_norm):
    a_refs = refs[:n_in]
    w_refs = refs[n_in:2 * n_in]
    res_ref, g_ref, wg_ref, wu_ref, wd_ref = refs[2 * n_in:2 * n_in + 5]
    fg_ref = refs[2 * n_in + 5] if final_norm else None
    o_ref = refs[-1]
    proj = functools.reduce(lambda x, y: x + y, [jnp.dot(a_ref[...], w_ref[...], preferred_element_type=F32)
                                                 for a_ref, w_ref in zip(a_refs, w_refs)])
    h = res_ref[...] + proj
    xn = _rms(h, g_ref[...]).astype(BF16)
    acc = h
    off = 0
    for n in chunks:
        gate = jnp.dot(xn, wg_ref[:, off:off + n], preferred_element_type=F32)
        up = jnp.dot(xn, wu_ref[:, off:off + n], preferred_element_type=F32)
        act = (_silu(gate) * up).astype(BF16)
        acc = acc + jnp.dot(act, wd_ref[off:off + n, :], preferred_element_type=F32)
        off += n
    o_ref[...] = _rms(acc, fg_ref[...]) if final_norm else acc


def proj_ffn(a_list, w_list, res, g, wg, wu, wd, final_g=None, tm=512):
    T, D = res.shape
    Hd = wg.shape[1]
    n_in = len(a_list)
    nch = 2 if (Hd % 256 == 0) else 1
    chunks = (Hd // nch,) * nch
    single = pl.Buffered(1)
    row = lambda n: pl.BlockSpec((tm, n), lambda i: (i, 0))
    const = lambda a: pl.BlockSpec(a.shape, lambda i: (0, 0))
    in_specs = [row(a.shape[1]) for a in a_list] + [const(w) for w in w_list]
    in_specs += [row(D), pl.BlockSpec((1, D), lambda i: (0, 0)),
                 pl.BlockSpec((D, Hd), lambda i: (0, 0), pipeline_mode=single),
                 pl.BlockSpec((D, Hd), lambda i: (0, 0), pipeline_mode=single),
                 pl.BlockSpec((Hd, D), lambda i: (0, 0), pipeline_mode=single)]
    args = [*a_list, *w_list, res, g.reshape(1, D), wg, wu, wd]
    if final_g is not None:
        in_specs.append(pl.BlockSpec((1, D), lambda i: (0, 0)))
        args.append(final_g.reshape(1, D))
    return pl.pallas_call(
        functools.partial(_proj_ffn_kernel, n_in=n_in, chunks=chunks, final_norm=final_g is not None),
        grid=(T // tm,),
        in_specs=in_specs,
        out_specs=row(D),
        out_shape=jax.ShapeDtypeStruct((T, D), F32),
        compiler_params=_cparams(("parallel",)),
        name="proj_ffn",
    )(*args)


MLA_HB = 128
MLA_FLASH_HEADS = 8


def _mla_prep_kernel(zm_ref, qn_ref, kvn_ref, wq_ref, wqr_ref, wk_ref, wv_ref, c_ref, s_ref,
                     q_out, k_out, v_out):
    zm = zm_ref[...]
    cq = zm[:, :MLA_Q_RANK]
    ckv = zm[:, MLA_Q_RANK:MLA_Q_RANK + MLA_KV_RANK]
    m1 = zm[:, 640:768]
    m2 = zm[:, 768:896]
    cqn = _rms(cq, qn_ref[...]).astype(BF16)
    ckvn = _rms(ckv, kvn_ref[...]).astype(BF16)
    q = jnp.dot(cqn, wq_ref[...], preferred_element_type=F32)
    qr = jnp.dot(cqn, wqr_ref[...], preferred_element_type=F32)
    kn = jnp.dot(ckvn, wk_ref[...], preferred_element_type=F32)
    cos = c_ref[...]
    sin = s_ref[...]
    lane = lax.broadcasted_iota(jnp.int32, cos.shape, 1)
    rope_lane = (lane >= MLA_NOPE) & (lane < MLA_NOPE + MLA_ROPE)
    krot = jnp.where(rope_lane, m1 * cos + m2 * sin, 0.0)
    scale = (MLA_NOPE + MLA_ROPE) ** -0.5 * LOG2E
    for h in range(MLA_HEADS):
        sl = slice(h * MLA_HB, (h + 1) * MLA_HB)
        q_out[:, sl] = ((q[:, sl] * cos + qr[:, sl] * sin) * scale).astype(BF16)
        k_out[:, sl] = (kn[:, sl] + krot).astype(BF16)
    vlane = lax.broadcasted_iota(jnp.int32, (1, MLA_HEADS * LANES), 1)
    ones_half = ((vlane // LANES) % 2 == 0) == ((vlane % LANES) >= MLA_V)
    v = jnp.dot(ckvn, wv_ref[...], preferred_element_type=F32)
    v_out[...] = jnp.where(ones_half, 1.0, v).astype(BF16)


def mla_prep(zm, qn, kvn, wq, wqr, wk, wv, cos_t, sin_t, S, tm=512):
    T = zm.shape[0]
    nsb = S // tm
    HW = MLA_HEADS * MLA_HB
    full = lambda a: pl.BlockSpec(a.shape, lambda i: (0, 0))
    return pl.pallas_call(
        _mla_prep_kernel,
        grid=(T // tm,),
        in_specs=[pl.BlockSpec((tm, zm.shape[1]), lambda i: (i, 0)),
                  full(qn), full(kvn), full(wq), full(wqr), full(wk), full(wv),
                  pl.BlockSpec((tm, LANES), lambda i: (i % nsb, 0)),
                  pl.BlockSpec((tm, LANES), lambda i: (i % nsb, 0))],
        out_specs=[pl.BlockSpec((tm, HW), lambda i: (i, 0)),
                   pl.BlockSpec((tm, HW), lambda i: (i, 0)),
                   pl.BlockSpec((tm, HW), lambda i: (i, 0))],
        out_shape=[jax.ShapeDtypeStruct((T, HW), BF16),
                   jax.ShapeDtypeStruct((T, HW), BF16),
                   jax.ShapeDtypeStruct((T, HW), BF16)],
        compiler_params=_cparams(("parallel",)),
        name="mla_prep",
    )(zm, qn, kvn, wq, wqr, wk, wv, cos_t, sin_t)


def _mla_flash_kernel(q_ref, k_ref, v_ref, o_ref, m_scr, acc_scr, *, tq, tk):
    i = pl.program_id(2)
    per_q = tq // tk
    m_scr[...] = jnp.full(m_scr.shape, NEG, F32)
    acc_scr[...] = jnp.zeros(acc_scr.shape, F32)
    heads = range(MLA_FLASH_HEADS)
    lane_tiles = range(tk // LANES)

    def update(j, diag):
        rows = pl.ds(pl.multiple_of(j * tk, tk), tk)
        s = [lax.dot_general(q_ref[:, h * MLA_HB:(h + 1) * MLA_HB], k_ref[rows, h * MLA_HB:(h + 1) * MLA_HB],
                             (((1,), (1,)), ((), ())), preferred_element_type=F32) for h in heads]
        if diag is not None:
            qpos = lax.broadcasted_iota(jnp.int32, (tq, tk), 0)
            kpos = lax.broadcasted_iota(jnp.int32, (tq, tk), 1) + diag * tk
            mask = kpos <= qpos
            s = [jnp.where(mask, x, NEG) for x in s]
        new_m = []
        pv = []
        for h in heads:
            tiles = [s[h][:, c * LANES:(c + 1) * LANES] for c in lane_tiles]
            tile_max = functools.reduce(jnp.maximum, tiles)
            m_new = jnp.maximum(m_scr[h], jnp.max(tile_max, axis=1, keepdims=True))
            p = jnp.concatenate([jnp.exp2(t - m_new) for t in tiles], axis=1).astype(BF16)
            pv.append(jnp.dot(p, v_ref[rows, h * LANES:(h + 1) * LANES], preferred_element_type=F32))
            new_m.append(m_new)
        for h in heads:
            acc_scr[h] = jnp.exp2(m_scr[h] - new_m[h]) * acc_scr[h] + pv[h]
            m_scr[h] = new_m[h]

    def body(j, carry):
        update(j, None)
        return carry

    lax.fori_loop(0, i * per_q, body, jnp.int32(0))
    for d in range(per_q):
        update(i * per_q + d, d)
    lane = lax.broadcasted_iota(jnp.int32, (tq, LANES), 1)
    for pr in range(MLA_FLASH_HEADS // 2):
        a0 = acc_scr[2 * pr]
        a1 = acc_scr[2 * pr + 1]
        o0 = a0 * (1.0 / pltpu.roll(a0, MLA_V, 1))
        o1 = a1 * (1.0 / pltpu.roll(a1, MLA_V, 1))
        o_ref[:, pr * LANES:(pr + 1) * LANES] = jnp.where(lane < MLA_V, o0, o1).astype(o_ref.dtype)


def mla_flash(q, k, v, B, S, tq=512, tk=512):
    T = q.shape[0]
    tk = min(tk, tq)
    nq = S // tq
    nh = MLA_FLASH_HEADS
    return pl.pallas_call(
        functools.partial(_mla_flash_kernel, tq=tq, tk=tk),
        grid=(B, MLA_HEADS // nh, nq),
        in_specs=[pl.BlockSpec((tq, nh * MLA_HB), lambda b, p, i: (b * nq + i, p)),
                  pl.BlockSpec((S, nh * MLA_HB), lambda b, p, i: (b, p), pipeline_mode=pl.Buffered(1)),
                  pl.BlockSpec((S, nh * LANES), lambda b, p, i: (b, p), pipeline_mode=pl.Buffered(1))],
        out_specs=pl.BlockSpec((tq, nh * MLA_V), lambda b, p, i: (b * nq + i, p)),
        out_shape=jax.ShapeDtypeStruct((T, MLA_HEADS * MLA_V), BF16),
        scratch_shapes=[pltpu.VMEM((nh, tq, LANES), F32),
                        pltpu.VMEM((nh, tq, LANES), F32)],
        compiler_params=_cparams(("parallel", "parallel", "arbitrary")),
        name="mla_flash",
    )(q, k, v)


GDN_QKV = 3 * GDN_HEADS * GDN_DK
BETA_LANE = 96
DECAY_LANE = 100
HALO = 16


def _gdn_prep_kernel(z_ref, halo_ref, m1_ref, cw_ref, alog_ref, dt_ref, qkv_out, gb_out, gbt_out,
                     *, tm, tiles_per_seq):
    i = pl.program_id(0)
    x = z_ref[...].astype(F32)
    halo = halo_ref[...].astype(F32)
    halo = jnp.where(i % tiles_per_seq == 0, jnp.zeros_like(halo), halo)
    xe = jnp.concatenate([halo, x], axis=0)
    cw = cw_ref[...]
    acc = x * cw[GDN_CONV - 1:GDN_CONV, :]
    for d in range(1, GDN_CONV):
        acc = acc + xe[HALO - d:HALO - d + tm, :] * cw[GDN_CONV - 1 - d:GDN_CONV - d, :]
    y = _silu(acc)
    nq = GDN_HEADS * GDN_DK
    for h in range(GDN_HEADS):
        sl = slice(h * GDN_DK, (h + 1) * GDN_DK)
        qh = y[:, sl]
        qkv_out[:, sl] = (qh * lax.rsqrt(jnp.sum(qh * qh, axis=-1, keepdims=True) + EPS) * (GDN_DK ** -0.5)
                          ).astype(qkv_out.dtype)
        sl2 = slice(nq + h * GDN_DK, nq + (h + 1) * GDN_DK)
        kh = y[:, sl2]
        qkv_out[:, sl2] = (kh * lax.rsqrt(jnp.sum(kh * kh, axis=-1, keepdims=True) + EPS)).astype(qkv_out.dtype)
    qkv_out[:, 2 * nq:] = y[:, 2 * nq:].astype(qkv_out.dtype)
    m1 = m1_ref[...]
    lane = lax.broadcasted_iota(jnp.int32, m1.shape, 1)
    beta = _sigmoid(m1)
    xa = m1 + dt_ref[...]
    softplus = jnp.maximum(xa, 0.0) + jnp.log(1.0 + jnp.exp(-jnp.abs(xa)))
    decay = -jnp.exp(alog_ref[...]) * softplus
    ri = lax.broadcasted_iota(jnp.int32, (tm, tm), 0)
    ci = lax.broadcasted_iota(jnp.int32, (tm, tm), 1)
    ltri = jnp.where((ri >= ci) & (ri // GDN_CHUNK == ci // GDN_CHUNK), 1.0, 0.0).astype(BF16)
    d_hi = decay.astype(BF16)
    rem = decay - d_hi.astype(F32)
    d_mid = rem.astype(BF16)
    d_lo = (rem - d_mid.astype(F32)).astype(BF16)
    gcum = (jnp.dot(ltri, d_hi, preferred_element_type=F32) + jnp.dot(ltri, d_mid, preferred_element_type=F32)
            + jnp.dot(ltri, d_lo, preferred_element_type=F32))
    gb = jnp.where(lane < DECAY_LANE, beta, gcum)
    gb_out[...] = gb
    r = lax.broadcasted_iota(jnp.int32, (8, LANES), 0)
    c = lax.broadcasted_iota(jnp.int32, (8, LANES), 1)
    pick = (c == r + BETA_LANE).astype(F32)
    gbt_out[...] = lax.dot_general(pick, gb, (((1,), (1,)), ((), ())),
                                   preferred_element_type=F32, precision=HIGHEST)


def gdn_prep(zg, zm, conv_w, alog_v, dt_v, S, tm=512):
    T = zg.shape[0]
    tiles_per_seq = S // tm
    hb = tm // HALO
    m1_blk = 640 // LANES
    return pl.pallas_call(
        functools.partial(_gdn_prep_kernel, tm=tm, tiles_per_seq=tiles_per_seq),
        grid=(T // tm,),
        in_specs=[pl.BlockSpec((tm, GDN_QKV), lambda i: (i, 0)),
                  pl.BlockSpec((HALO, GDN_QKV), lambda i: (jnp.maximum(i * hb - 1, 0), 0)),
                  pl.BlockSpec((tm, LANES), lambda i: (i, m1_blk)),
                  pl.BlockSpec((GDN_CONV, GDN_QKV), lambda i: (0, 0)),
                  pl.BlockSpec((1, LANES), lambda i: (0, 0)),
                  pl.BlockSpec((1, LANES), lambda i: (0, 0))],
        out_specs=[pl.BlockSpec((tm, GDN_QKV), lambda i: (i, 0)),
                   pl.BlockSpec((tm, LANES), lambda i: (i, 0)),
                   pl.BlockSpec((8, tm), lambda i: (0, i))],
        out_shape=[jax.ShapeDtypeStruct((T, GDN_QKV), F32),
                   jax.ShapeDtypeStruct((T, LANES), F32),
                   jax.ShapeDtypeStruct((8, T), F32)],
        compiler_params=_cparams(("parallel",)),
        name="gdn_prep",
    )(zg, zg, zm, conv_w, alog_v, dt_v)


def _dot3(a, b):
    a_hi = a.astype(BF16)
    a_lo = (a - a_hi.astype(F32)).astype(BF16)
    b_hi = b.astype(BF16)
    b_lo = (b - b_hi.astype(F32)).astype(BF16)
    return (jnp.dot(a_hi, b_hi, preferred_element_type=F32) + jnp.dot(a_hi, b_lo, preferred_element_type=F32)
            + jnp.dot(a_lo, b_hi, preferred_element_type=F32))


def _tri_inverse_all(a_list, eye, diag_blocks):
    ad = [jnp.where(diag_blocks, a, 0.0) for a in a_list]
    ao = [a - d for a, d in zip(a_list, ad)]
    a2 = [_dot(d, d) for d in ad]
    a4 = [_dot(x, x) for x in a2]
    a8 = [_dot(x, x) for x in a4]
    t = [_dot(eye - d, eye + x) for d, x in zip(ad, a2)]
    t = [_dot(y, eye + x) for y, x in zip(t, a4)]
    dinv = [_dot(y, eye + x) for y, x in zip(t, a8)]
    n = [_dot(d, o) for d, o in zip(dinv, ao)]
    n2 = [_dot(x, x) for x in n]
    t = [_dot(eye - x, eye + y) for x, y in zip(n, n2)]
    x0 = [_dot(y, d) for y, d in zip(t, dinv)]
    res = [eye - x - _dot3(a, x) for a, x in zip(a_list, x0)]
    return [x + _dot(x, r) for x, r in zip(x0, res)]


def _gdn_chunk_kernel(qkv_ref, gb_ref, gbt_ref, zg_ref, norm_ref, o_ref, state_scr, *, lb):
    C = GDN_CHUNK
    DK = GDN_DK
    H = GDN_HEADS
    nq = H * DK
    n_chunks = lb // C

    @pl.when(pl.program_id(1) == 0)
    def _():
        state_scr[...] = jnp.zeros(state_scr.shape, F32)

    ii = lax.broadcasted_iota(jnp.int32, (C, C), 0)
    jj = lax.broadcasted_iota(jnp.int32, (C, C), 1)
    lower = ii >= jj
    strict = ii > jj
    eye = (ii == jj).astype(F32)
    diag_blocks = (ii // 16) == (jj // 16)
    norm_w = norm_ref[...]

    items = [(c, h) for c in range(n_chunks) for h in range(H)]
    rows = lambda c: slice(c * C, (c + 1) * C)
    q = [qkv_ref[rows(c), h * DK:(h + 1) * DK] for c, h in items]
    k = [qkv_ref[rows(c), nq + h * DK:nq + (h + 1) * DK] for c, h in items]
    v = [qkv_ref[rows(c), 2 * nq + h * DK:2 * nq + (h + 1) * DK] for c, h in items]
    beta = [jnp.broadcast_to(gb_ref[rows(c), BETA_LANE + h:BETA_LANE + h + 1], (C, DK)) for c, h in items]
    gc = [jnp.broadcast_to(gb_ref[rows(c), DECAY_LANE + h:DECAY_LANE + h + 1], (C, DK)) for c, h in items]
    gr = [jnp.broadcast_to(gbt_ref[4 + h:5 + h, rows(c)], (C, C)) for c, h in items]
    decay = [jnp.exp(jnp.where(lower, x[:, :C] - y, NEG)) for x, y in zip(gc, gr)]
    eg = [jnp.exp(x) for x in gc]
    kb = [x * b for x, b in zip(k, beta)]
    a = [jnp.where(strict, _dot_nt(x, y) * d, 0.0) for x, y, d in zip(kb, k, decay)]
    t_inv = _tri_inverse_all(a, eye, diag_blocks)
    uw = [_dot(t, jnp.concatenate([x * b, y * e], axis=1))
          for t, x, b, y, e in zip(t_inv, v, beta, kb, eg)]
    intra = [_dot_nt(x, y) * d for x, y, d in zip(q, k, decay)]
    g_last = [x[C - 1:C, :] for x in gc]
    k_dec = [x * jnp.exp(gl - g) for x, gl, g in zip(k, g_last, gc)]
    wq = [jnp.concatenate([x[:, DK:], y * e], axis=0) for x, y, e in zip(uw, q, eg)]

    states = [state_scr[h] for h in range(H)]
    for c in range(n_chunks):
        idx = [c * H + h for h in range(H)]
        ws = [_dot(wq[i], states[h]) for h, i in enumerate(idx)]
        v_new = [uw[i][:, :DK] - y[:C] for i, y in zip(idx, ws)]
        o = [y[C:] + _dot(intra[i], vn) for i, y, vn in zip(idx, ws, v_new)]
        states = [s * jnp.exp(g_last[i]) + _dot_tn(k_dec[i], vn) for s, i, vn in zip(states, idx, v_new)]
        for h in range(H):
            on = o[h] * lax.rsqrt(jnp.mean(o[h] * o[h], axis=-1, keepdims=True) + EPS) * norm_w
            o_ref[rows(c), h * DK:(h + 1) * DK] = (on * _silu(zg_ref[rows(c), h * DK:(h + 1) * DK].astype(F32))
                                                   ).astype(o_ref.dtype)
    for h in range(H):
        state_scr[h] = states[h]


def gdn_chunk(qkv, gb, gbt, zg, norm_w, B, S, lb=512):
    T = qkv.shape[0]
    nsb = S // lb
    VW = GDN_HEADS * GDN_DV
    zg_blk = GDN_QKV // VW
    return pl.pallas_call(
        functools.partial(_gdn_chunk_kernel, lb=lb),
        grid=(B, nsb),
        in_specs=[pl.BlockSpec((lb, GDN_QKV), lambda b, s: (b * nsb + s, 0)),
                  pl.BlockSpec((lb, LANES), lambda b, s: (b * nsb + s, 0)),
                  pl.BlockSpec((8, lb), lambda b, s: (0, b * nsb + s)),
                  pl.BlockSpec((lb, VW), lambda b, s: (b * nsb + s, zg_blk)),
                  pl.BlockSpec((1, GDN_DV), lambda b, s: (0, 0))],
        out_specs=pl.BlockSpec((lb, VW), lambda b, s: (b * nsb + s, 0)),
        out_shape=jax.ShapeDtypeStruct((T, VW), BF16),
        scratch_shapes=[pltpu.VMEM((GDN_HEADS, GDN_DK, GDN_DV), F32)],
        compiler_params=_cparams(("parallel", "arbitrary")),
        name="gdn_chunk",
    )(qkv, gb, gbt, zg, norm_w)


def _compress_kernel(xl_ref, xh_ref, pea_ref, peb_ref, w1a_ref, w1b_ref, w2_ref, o_ref, p1_scr, p2_scr,
                     *, per_step):
    kc = pl.program_id(1)
    nr = p1_scr.shape[0]
    width = xl_ref.shape[1] + xh_ref.shape[1]

    @pl.when(kc == 0)
    def _():
        p1_scr[...] = jnp.zeros(p1_scr.shape, F32)
        p2_scr[...] = jnp.zeros(p2_scr.shape, F32)

    for t in range(per_step):
        rows = pl.ds(kc * per_step + t, nr, stride=CMP_STRIDE)
        x = jnp.concatenate([xl_ref[rows, :], xh_ref[rows, :]], axis=1)
        cols = slice(t * width, (t + 1) * width)
        p1_scr[...] += _dot(x + pea_ref[:, cols], w1a_ref[cols, :])
        p2_scr[...] += _dot(x + peb_ref[:, cols], w1b_ref[cols, :])

    @pl.when(kc == pl.num_programs(1) - 1)
    def _():
        p2 = p2_scr[...]
        rows = p2.shape[0]
        hid = p1_scr[...] + pltpu.roll(p2, rows - 1, 0)
        c0 = math.sqrt(2.0 / math.pi)
        act = 0.5 * hid * (1.0 + jnp.tanh(c0 * (hid + 0.044715 * (hid * hid * hid))))
        o_ref[0] = _dot(act, w2_ref[...])


def nsa_compress(x, pe_a, pe_b, w1a, w1b, w2, B, S, per_step=2):
    nr = S // CMP_STRIDE
    width = x.shape[1]
    kchunk = per_step * width
    KW = CMP_STRIDE * width
    HW = NSA_GROUPS * CMP_HIDDEN
    OW = NSA_GROUPS * NSA_DIM
    return pl.pallas_call(
        functools.partial(_compress_kernel, per_step=per_step),
        grid=(B, KW // kchunk),
        in_specs=[pl.BlockSpec((S, LANES), lambda b, k: (b, 0)),
                  pl.BlockSpec((S, LANES), lambda b, k: (b, 1)),
                  pl.BlockSpec((1, kchunk), lambda b, k: (0, k)),
                  pl.BlockSpec((1, kchunk), lambda b, k: (0, k)),
                  pl.BlockSpec((kchunk, HW), lambda b, k: (k, 0)),
                  pl.BlockSpec((kchunk, HW), lambda b, k: (k, 0)),
                  pl.BlockSpec((HW, OW), lambda b, k: (0, 0))],
        out_specs=pl.BlockSpec((1, nr, OW), lambda b, k: (b, 0, 0)),
        out_shape=jax.ShapeDtypeStruct((B, nr, OW), F32),
        scratch_shapes=[pltpu.VMEM((nr, HW), F32), pltpu.VMEM((nr, HW), F32)],
        compiler_params=_cparams(("parallel", "arbitrary")),
        name="nsa_compress",
    )(x, x, pe_a, pe_b, w1a, w1b, w2)


def _slope(h):
    return float(2.0 ** (-8.0 * (h + 1) / NSA_HEADS))


SEL_TK = 2 * SEL_BLOCK
SEL_NT = 4
SEL_TQ = 256


def _gate_expand(branch):
    e = np.zeros((LANES, NSA_HEADS * NSA_DIM), np.float32)
    for h in range(NSA_HEADS):
        e[3 * h + branch, h * NSA_DIM:(h + 1) * NSA_DIM] = 1.0
    return jnp.asarray(e).astype(BF16)


def _gate_matrix(zg, expand):
    g = _sigmoid(zg)
    g_hi = g.astype(BF16)
    rem = g - g_hi.astype(F32)
    g_mid = rem.astype(BF16)
    g_lo = (rem - g_mid.astype(F32)).astype(BF16)
    dot = lambda a: jnp.dot(a, expand, preferred_element_type=F32)
    return (dot(g_hi) + dot(g_mid)) + dot(g_lo)


def _value_blocks(vg):
    ones = jnp.ones_like(vg)
    return jnp.concatenate([vg, ones], axis=1), jnp.concatenate([ones, vg], axis=1)


def _nsa_cmp_kernel(q_ref, kc_ref, vc_ref, zg_ref, ge_ref, ov_ref, o_ref, sel_ref, flag_ref, work_scr,
                    *, tq, n_sel):
    i = pl.program_id(1)
    n_tiles = n_sel * SEL_BLOCK // SEL_TK
    blk_tile = (lax.broadcasted_iota(jnp.int32, (n_sel, n_tiles), 0) * SEL_BLOCK // SEL_TK
                == lax.broadcasted_iota(jnp.int32, (n_sel, n_tiles), 1))
    to_tile = jnp.where(blk_tile, 1.0, 0.0).astype(BF16)
    ncmp = kc_ref.shape[1]
    D = NSA_DIM
    qpos = i * tq + lax.broadcasted_iota(jnp.int32, (tq, 1), 0)
    nidx = lax.broadcasted_iota(jnp.int32, (1, ncmp), 1)
    valid = (nidx * CMP_STRIDE + (CMP_BLOCK - 1)) <= qpos
    any_valid = (qpos >= CMP_BLOCK - 1).astype(F32)
    centre_rel = (nidx * CMP_STRIDE - i * tq).astype(F32) + 0.5 * (CMP_BLOCK - 1)
    gate_mat = _gate_matrix(zg_ref[...], ge_ref[...])
    kc = kc_ref[0].astype(BF16)
    vc = vc_ref[0].astype(BF16)
    ov_t = ov_ref[...]
    jf = lax.broadcasted_iota(jnp.int32, (n_sel, 1), 0).astype(F32)
    qblk = ((i * tq + lax.broadcasted_iota(jnp.int32, (1, tq), 1)) // SEL_BLOCK).astype(F32)
    forced = (jf == 0.0) | (jf == qblk) | (jf == qblk - 1.0)
    causal_blk = jf <= qblk
    heads = range(NSA_HPG)
    groups = range(NSA_GROUPS)
    lane_half = lax.broadcasted_iota(jnp.int32, (tq, LANES), 1) < D
    assert ncmp % LANES == 0

    def scores(width):
        for g in groups:
            kg = kc[:width, g * D:(g + 1) * D]
            vg = vc[:width, g * D:(g + 1) * D]
            hs = [g * NSA_HPG + r for r in heads]
            s = [_dot_nt(q_ref[:, h * D:(h + 1) * D], kg) + (_slope(h) * LOG2E) * centre_rel[:, :width] for h in hs]
            s = [jnp.where(valid[:, :width], x, NEG) for x in s]
            v_even, v_odd = _value_blocks(vg)
            p, o = [], []
            for r, h in enumerate(hs):
                tiles = [s[r][:, c * LANES:(c + 1) * LANES] for c in range(width // LANES)]
                tile_max = functools.reduce(jnp.maximum, tiles)
                m = jnp.maximum(jnp.full((tq, LANES), NEG, F32), jnp.max(tile_max, axis=1, keepdims=True))
                e_tiles = [jnp.exp2(t - m) for t in tiles]
                pv = jnp.dot(jnp.concatenate(e_tiles, axis=1).astype(BF16), v_odd if r % 2 else v_even,
                             preferred_element_type=F32)
                rolled = pltpu.roll(pv, D, 1)
                row_sum = jnp.where(lane_half, pv, rolled) if r % 2 else jnp.where(lane_half, rolled, pv)
                norm = any_valid / row_sum
                o.append(pv * norm)
                p.append([t * norm for t in e_tiles])
            for t in range(NSA_HPG // 2):
                blk = slice((g * NSA_HPG // 2 + t) * LANES, (g * NSA_HPG // 2 + t + 1) * LANES)
                o_ref[:, blk] = gate_mat[:, blk] * jnp.where(lane_half, o[2 * t], o[2 * t + 1])
            psum = jnp.concatenate([(a + b) + (c + d) for a, b, c, d in zip(*p)], axis=1)
            p_hi = psum.astype(BF16)
            rem = psum - p_hi.astype(F32)
            p_mid = rem.astype(BF16)
            p_lo = (rem - p_mid.astype(F32)).astype(BF16)
            ov_w = ov_t[:, :width]
            imp_t = (_dot_nt(ov_w, p_hi) + _dot_nt(ov_w, p_mid)) + _dot_nt(ov_w, p_lo)
            work_scr[g] = jnp.where(forced, BIG, jnp.where(causal_blk, imp_t, NEG))

    n_valid = jnp.maximum((i * tq + tq - CMP_BLOCK) // CMP_STRIDE + 1, 1)
    tiles_needed = jnp.minimum((n_valid + LANES - 1) // LANES, ncmp // LANES)
    for nt in range(1, ncmp // LANES + 1):
        pl.when(tiles_needed == nt)(functools.partial(scores, nt * LANES))
    work = [work_scr[g] for g in groups]
    selm = [jnp.zeros((n_sel, tq), F32) for _ in groups]
    for _ in range(min(SEL_TOPN, n_sel)):
        mx = [jnp.max(w, axis=0, keepdims=True) for w in work]
        first = [jnp.min(jnp.where(w == m, jf, float(n_sel)), axis=0, keepdims=True) for w, m in zip(work, mx)]
        pick = [jf == f for f in first]
        selm = [jnp.where(pk, 1.0, sm) for pk, sm in zip(pick, selm)]
        work = [jnp.where(pk, -jnp.inf, w) for pk, w in zip(pick, work)]
    for g in groups:
        selb = selm[g].T.astype(BF16)
        sel_ref[:, g * n_sel:(g + 1) * n_sel] = selb
        tile_hits = jnp.dot(selb, to_tile, preferred_element_type=F32)
        for part in range(tq // SEL_TQ):
            hits = jnp.max(tile_hits[part * SEL_TQ:(part + 1) * SEL_TQ], axis=0, keepdims=True)
            flag_ref[part, g:g + 1, :] = (hits > 0.5).astype(jnp.int32)


def nsa_cmp(q, k_cmp, v_cmp, zg, overlap, B, S, tq=SEL_TQ):
    T = q.shape[0]
    tq = min(tq, S)
    nq = S // tq
    parts = tq // SEL_TQ
    n_sel = S // SEL_BLOCK
    ncmp = k_cmp.shape[1]
    QW = NSA_HEADS * NSA_DIM
    KW = NSA_GROUPS * NSA_DIM
    return pl.pallas_call(
        functools.partial(_nsa_cmp_kernel, tq=tq, n_sel=n_sel),
        grid=(B, nq),
        in_specs=[pl.BlockSpec((tq, QW), lambda b, i: (b * nq + i, 0)),
                  pl.BlockSpec((1, ncmp, KW), lambda b, i: (b, 0, 0)),
                  pl.BlockSpec((1, ncmp, KW), lambda b, i: (b, 0, 0)),
                  pl.BlockSpec((tq, LANES), lambda b, i: (b * nq + i, 0)),
                  pl.BlockSpec((LANES, QW), lambda b, i: (0, 0)),
                  pl.BlockSpec((n_sel, ncmp), lambda b, i: (0, 0))],
        out_specs=[pl.BlockSpec((tq, QW), lambda b, i: (b * nq + i, 0)),
                   pl.BlockSpec((tq, NSA_GROUPS * n_sel), lambda b, i: (b * nq + i, 0)),
                   pl.BlockSpec((parts, NSA_GROUPS, S // SEL_TK), lambda b, i: (b * nq + i, 0, 0))],
        out_shape=[jax.ShapeDtypeStruct((T, QW), F32),
                   jax.ShapeDtypeStruct((T, NSA_GROUPS * n_sel), BF16),
                   jax.ShapeDtypeStruct((B * nq * parts, NSA_GROUPS, S // SEL_TK), jnp.int32)],
        scratch_shapes=[pltpu.VMEM((NSA_GROUPS, n_sel, tq), F32)],
        compiler_params=_cparams(("parallel", "parallel")),
        name="nsa_cmp",
    )(q, k_cmp, v_cmp, zg, _gate_expand(0), overlap)


def _nsa_sel_kernel(flags_ref, q_ref, k_ref, v_ref, sel_ref, zg_ref, ge_sel_ref, ge_win_ref,
                    kw0_ref, kw1_ref, kw2_ref, vw0_ref, vw1_ref, vw2_ref, prev_ref, o_ref,
                    list_smem, m_scr, acc_scr, *, tq, n_sel, nq):
    b = pl.program_id(0)
    i = pl.program_id(1)
    D = NSA_DIM
    n_tiles = n_sel * SEL_BLOCK // SEL_TK
    n_causal = (i * tq + tq - 1) // SEL_TK + 1
    ks_w = SEL_NT * SEL_TK
    qpos = i * tq + lax.broadcasted_iota(jnp.int32, (tq, 1), 0)
    lane_t = lax.broadcasted_iota(jnp.int32, (1, SEL_TK), 1)
    blk_iota = lax.broadcasted_iota(jnp.int32, (n_sel, ks_w), 0)
    zg = zg_ref[...]
    gate_sel = _gate_matrix(zg, ge_sel_ref[...])
    gate_win = _gate_matrix(zg, ge_win_ref[...])
    lane_half = lax.broadcasted_iota(jnp.int32, (tq, LANES), 1) < D

    nback = WINDOW // tq
    tkw = (nback + 1) * tq
    k_win = jnp.concatenate([kw0_ref[...], kw1_ref[...], kw2_ref[...]], axis=0)
    v_win = jnp.concatenate([vw0_ref[...], vw1_ref[...], vw2_ref[...]], axis=0)
    qrel = lax.broadcasted_iota(jnp.int32, (tq, tkw), 0)
    krel_w = lax.broadcasted_iota(jnp.int32, (tq, tkw), 1) - nback * tq
    dw = qrel - krel_w
    wvalid = jnp.where(dw >= 0, jnp.where(dw < WINDOW, krel_w + i * tq, -1), -1) >= 0
    krow_w = (lax.broadcasted_iota(jnp.int32, (1, tkw), 1) - nback * tq).astype(F32)

    for g in range(NSA_GROUPS):
        base = ((b * nq + i) * NSA_GROUPS + g) * n_tiles

        def scan(j, n, base=base):
            list_smem[n] = j
            return n + (flags_ref[base + j] != 0).astype(jnp.int32)

        count = lax.fori_loop(0, n_causal, scan, jnp.int32(0))

        for r in range(NSA_HPG):
            m_scr[r] = jnp.full(m_scr.shape[1:], NEG, F32)
            acc_scr[r] = jnp.zeros(acc_scr.shape[1:], F32)
        qs = [q_ref[:, (g * NSA_HPG + r) * D:(g * NSA_HPG + r + 1) * D] for r in range(NSA_HPG)]
        sel_g = sel_ref[:, g * n_sel:(g + 1) * n_sel]

        def step(st, carry, g=g, count=count, qs=qs, sel_g=sel_g):
            k_parts, v_parts, kpos_parts, kblk_parts = [], [], [], []
            for s in range(SEL_NT):
                idx = st * SEL_NT + s
                j = list_smem[jnp.minimum(idx, count - 1)]
                start = pl.multiple_of(j * SEL_TK, SEL_TK)
                k_parts.append(k_ref[pl.ds(start, SEL_TK), g * D:(g + 1) * D])
                v_parts.append(v_ref[pl.ds(start, SEL_TK), g * D:(g + 1) * D])
                tid = jnp.where(idx < count, j, -1)
                kpos_parts.append(tid * SEL_TK + lane_t)
                kblk_parts.append(tid * (SEL_TK // SEL_BLOCK) + lane_t // SEL_BLOCK)
            k = jnp.concatenate(k_parts, axis=0)
            v = jnp.concatenate(v_parts, axis=0)
            kpos = jnp.concatenate(kpos_parts, axis=1)
            kblk = jnp.concatenate(kblk_parts, axis=1)
            expand = jnp.where(blk_iota == kblk, 1.0, 0.0).astype(BF16)
            picked = jnp.dot(sel_g, expand, preferred_element_type=F32)
            allowed = jnp.where(kpos <= qpos, picked, 0.0) > 0.5
            krel = (kpos - i * tq).astype(F32)
            hs = range(NSA_HPG)
            s_ = [_dot_nt(qs[r], k) + (_slope(g * NSA_HPG + r) * LOG2E) * krel for r in hs]
            s_ = [jnp.where(allowed, x, NEG) for x in s_]
            v_even, v_odd = _value_blocks(v)
            new_m, pv = [], []
            for r in hs:
                tiles = [s_[r][:, c * LANES:(c + 1) * LANES] for c in range(ks_w // LANES)]
                tile_max = functools.reduce(jnp.maximum, tiles)
                m_new = jnp.maximum(m_scr[r], jnp.max(tile_max, axis=1, keepdims=True))
                p = jnp.concatenate([jnp.exp2(t - m_new) for t in tiles], axis=1).astype(BF16)
                pv.append(jnp.dot(p, v_odd if r % 2 else v_even, preferred_element_type=F32))
                new_m.append(m_new)
            for r in hs:
                acc_scr[r] = jnp.exp2(m_scr[r] - new_m[r]) * acc_scr[r] + pv[r]
                m_scr[r] = new_m[r]
            return carry

        lax.fori_loop(0, (count + SEL_NT - 1) // SEL_NT, step, jnp.int32(0))

        o = [acc_scr[r] * (1.0 / pltpu.roll(acc_scr[r], D, 1)) for r in range(NSA_HPG)]

        kg = k_win[:, g * D:(g + 1) * D]
        w_even, w_odd = _value_blocks(v_win[:, g * D:(g + 1) * D])
        sw = [_dot_nt(qs[r], kg) + (_slope(g * NSA_HPG + r) * LOG2E) * krow_w for r in range(NSA_HPG)]
        sw = [jnp.where(wvalid, x, NEG) for x in sw]
        ow = []
        for r, x in enumerate(sw):
            tiles = [x[:, c * LANES:(c + 1) * LANES] for c in range(tkw // LANES)]
            tile_max = functools.reduce(jnp.maximum, tiles)
            m = jnp.maximum(jnp.full((tq, LANES), NEG, F32), jnp.max(tile_max, axis=1, keepdims=True))
            e = jnp.concatenate([jnp.exp2(t - m) for t in tiles], axis=1).astype(BF16)
            pv = jnp.dot(e, w_odd if r % 2 else w_even, preferred_element_type=F32)
            ow.append(pv * (1.0 / pltpu.roll(pv, D, 1)))

        for t in range(NSA_HPG // 2):
            blk = slice((g * NSA_HPG // 2 + t) * LANES, (g * NSA_HPG // 2 + t + 1) * LANES)
            o_ref[:, blk] = (prev_ref[:, blk]
                             + gate_sel[:, blk] * jnp.where(lane_half, o[2 * t], o[2 * t + 1])
                             + gate_win[:, blk] * jnp.where(lane_half, ow[2 * t], ow[2 * t + 1])).astype(o_ref.dtype)


def nsa_sel_win(q, ks, vs, kw, vw, sel, flags, zg, prev, B, S, tq=SEL_TQ):
    T = q.shape[0]
    assert WINDOW % tq == 0 and WINDOW // tq == 2
    nq = S // tq
    n_sel = S // SEL_BLOCK
    QW = NSA_HEADS * NSA_DIM
    KW = NSA_GROUPS * NSA_DIM
    qmap = lambda b, i, fl: (b * nq + i, 0)
    kmap = lambda b, i, fl: (b, 0)
    const = lambda b, i, fl: (0, 0)
    back = lambda d: (lambda b, i, fl: (b * nq + jnp.maximum(i - d, 0), 0))
    wspecs = [pl.BlockSpec((tq, KW), back(2)), pl.BlockSpec((tq, KW), back(1)), pl.BlockSpec((tq, KW), back(0))]
    grid_spec = pltpu.PrefetchScalarGridSpec(
        num_scalar_prefetch=1,
        grid=(B, nq),
        in_specs=[pl.BlockSpec((tq, QW), qmap),
                  pl.BlockSpec((S, KW), kmap),
                  pl.BlockSpec((S, KW), kmap),
                  pl.BlockSpec((tq, NSA_GROUPS * n_sel), qmap),
                  pl.BlockSpec((tq, LANES), qmap),
                  pl.BlockSpec((LANES, QW), const),
                  pl.BlockSpec((LANES, QW), const)] + wspecs + wspecs
                 + [pl.BlockSpec((tq, QW), qmap)],
        out_specs=pl.BlockSpec((tq, QW), qmap),
        scratch_shapes=[pltpu.SMEM((S // SEL_TK,), jnp.int32),
                        pltpu.VMEM((NSA_HPG, tq, LANES), F32),
                        pltpu.VMEM((NSA_HPG, tq, LANES), F32)],
    )
    return pl.pallas_call(
        functools.partial(_nsa_sel_kernel, tq=tq, n_sel=n_sel, nq=nq),
        grid_spec=grid_spec,
        out_shape=jax.ShapeDtypeStruct((T, QW), BF16),
        compiler_params=_cparams(("parallel", "arbitrary")),
        name="nsa_sel_win",
    )(flags.reshape(-1), q, ks, vs, sel, zg, _gate_expand(1), _gate_expand(2), kw, kw, kw, vw, vw, vw, prev)


def _rot_half_cols(w):
    half = w.shape[-1] // 2
    return jnp.concatenate([-w[..., half:], w[..., :half]], axis=-1)


def _even_weights(w_in, w_uq, w_ukv):
    D = w_in.shape[0]
    o = 0
    cuts = {}
    for name, n in (("cq", MLA_Q_RANK), ("ckv", MLA_KV_RANK), ("kr", MLA_ROPE), ("zq", 512), ("zk", 512),
                    ("zv", 512), ("zg", 512), ("zb", GDN_HEADS), ("za", GDN_HEADS)):
        cuts[name] = w_in[:, o:o + n]
        o += n
    z = lambda n: jnp.zeros((D, n), F32)
    misc1 = jnp.concatenate([z(MLA_NOPE), cuts["kr"], cuts["zb"], cuts["za"],
                             z(LANES - MLA_NOPE - MLA_ROPE - 2 * GDN_HEADS)], axis=1)
    misc2 = jnp.concatenate([z(MLA_NOPE), _rot_half_cols(cuts["kr"]), z(LANES - MLA_NOPE - MLA_ROPE)], axis=1)
    w_even = jnp.concatenate([cuts["cq"], cuts["ckv"], misc1, misc2,
                              cuts["zq"], cuts["zk"], cuts["zv"], cuts["zg"]], axis=1).astype(BF16)
    qd = MLA_NOPE + MLA_ROPE
    wq3 = w_uq.reshape(MLA_Q_RANK, MLA_HEADS, qd)
    zq = jnp.zeros((MLA_Q_RANK, MLA_HEADS, MLA_HB - qd), F32)
    wq = jnp.concatenate([wq3, zq], axis=2).reshape(MLA_Q_RANK, MLA_HEADS * MLA_HB).astype(BF16)
    wqr = jnp.concatenate([jnp.zeros((MLA_Q_RANK, MLA_HEADS, MLA_NOPE), F32),
                           _rot_half_cols(wq3[:, :, MLA_NOPE:]), zq], axis=2)
    wqr = wqr.reshape(MLA_Q_RANK, MLA_HEADS * MLA_HB).astype(BF16)
    wkv3 = w_ukv.reshape(MLA_KV_RANK, MLA_HEADS, MLA_NOPE + MLA_V)
    wk = jnp.concatenate([wkv3[:, :, :MLA_NOPE], jnp.zeros((MLA_KV_RANK, MLA_HEADS, MLA_HB - MLA_NOPE), F32)],
                         axis=2).reshape(MLA_KV_RANK, MLA_HEADS * MLA_HB).astype(BF16)
    wv4 = wkv3[:, :, MLA_NOPE:].reshape(MLA_KV_RANK, MLA_HEADS // 2, 2, MLA_V)
    zv = jnp.zeros((MLA_KV_RANK, MLA_HEADS // 2, MLA_V), F32)
    wv = jnp.stack([jnp.concatenate([wv4[:, :, 0], zv], axis=2), jnp.concatenate([zv, wv4[:, :, 1]], axis=2)],
                   axis=2).reshape(MLA_KV_RANK, MLA_HEADS * MLA_HB).astype(BF16)
    return w_even, wq, wqr, wk, wv


def _rope_tables(S):
    half = MLA_ROPE // 2
    inv = ROPE_BASE ** (-jnp.arange(half, dtype=F32) / half)
    ang = jnp.arange(S, dtype=F32)[:, None] * inv[None, :]
    cos = jnp.cos(ang)
    sin = jnp.sin(ang)
    pad = jnp.zeros((S, LANES - MLA_NOPE - MLA_ROPE), F32)
    cos_t = jnp.concatenate([jnp.ones((S, MLA_NOPE), F32), cos, cos, pad], axis=1)
    sin_t = jnp.concatenate([jnp.zeros((S, MLA_NOPE), F32), sin, sin, pad], axis=1)
    return cos_t, sin_t


def _lane_vec(vals, start):
    return jnp.zeros((1, LANES), F32).at[0, start:start + vals.shape[0]].set(vals)


def even_mixer_layer(h, B, S, attn_norm, w_in, q_norm, kv_norm, w_uq, w_ukv, conv_w, a_log, dt_bias,
                     gdn_norm, w_out, tables):
    w_even, wq, wqr, wk, wv = _even_weights(w_in, w_uq, w_ukv)
    zm, zg = rms_matmul(h, attn_norm, w_even, (896, 2048), dtypes=(F32, BF16))
    cos_t, sin_t = tables
    q, k, v = mla_prep(zm, q_norm.reshape(1, -1), kv_norm.reshape(1, -1), wq, wqr, wk, wv, cos_t, sin_t, S)
    o_mla = mla_flash(q, k, v, B, S, tq=min(512, S))
    qkv, gb, gbt = gdn_prep(zg, zm, conv_w, _lane_vec(a_log, DECAY_LANE), _lane_vec(dt_bias, DECAY_LANE), S)
    o_gdn = gdn_chunk(qkv, gb, gbt, zg, gdn_norm.reshape(1, -1), B, S)
    nm = MLA_HEADS * MLA_V
    return [o_mla, o_gdn], [w_out[:nm].astype(BF16), w_out[nm:].astype(BF16)]


def _compress_weights(pe, w1, w2):
    G, D = NSA_GROUPS, NSA_DIM
    eye = jnp.eye(G, dtype=F32)
    w1r = w1.reshape(CMP_BLOCK, D, CMP_HIDDEN)

    def expand(wpart):
        return jnp.einsum('ldh,gk->lgdkh', wpart, eye).reshape(CMP_STRIDE * G * D, G * CMP_HIDDEN).astype(BF16)

    def pe_vec(p):
        return jnp.broadcast_to(p[:, None, :], (CMP_STRIDE, G, D)).reshape(1, CMP_STRIDE * G * D)

    w2e = jnp.einsum('hd,gk->ghkd', w2, eye).reshape(G * CMP_HIDDEN, G * D).astype(BF16)
    return (pe_vec(pe[:CMP_STRIDE]), pe_vec(pe[CMP_STRIDE:]), expand(w1r[:CMP_STRIDE]),
            expand(w1r[CMP_STRIDE:]), w2e)


def _overlap_matrix(S):
    nr = S // CMP_STRIDE
    n_sel = S // SEL_BLOCK
    n = np.arange(nr)[:, None]
    j = np.arange(n_sel)[None, :]
    start = n * CMP_STRIDE
    ov = (start <= j * SEL_BLOCK + SEL_BLOCK - 1) & (start + CMP_BLOCK - 1 >= j * SEL_BLOCK)
    ov = ov & (n < nr - 1)
    return jnp.asarray(ov.T.astype(np.float32)).astype(BF16)


def odd_mixer_layer(h, B, S, attn_norm, w_in, pe_k, w1_k, w2_k, pe_v, w1_v, w2_v, w_out):
    D = w_in.shape[0]
    n_g = 3 * NSA_HEADS
    qw = NSA_HEADS * NSA_DIM
    w_odd = jnp.concatenate([w_in[:, :qw] * (NSA_DIM ** -0.5 * LOG2E), w_in[:, qw:],
                             jnp.zeros((D, LANES - n_g), F32)], axis=1).astype(BF16)
    kvw = NSA_GROUPS * NSA_DIM
    q, kc, vc, ks, vs, kw, vw, zg = rms_matmul(
        h, attn_norm, w_odd, (qw,) + (kvw,) * 6 + (LANES,),
        dtypes=(BF16, F32, F32, BF16, BF16, BF16, BF16, F32))
    k_cmp = nsa_compress(kc, *_compress_weights(pe_k, w1_k, w2_k), B, S)
    v_cmp = nsa_compress(vc, *_compress_weights(pe_v, w1_v, w2_v), B, S)
    o1, sel, flags = nsa_cmp(q, k_cmp, v_cmp, zg, _overlap_matrix(S), B, S)
    o3 = nsa_sel_win(q, ks, vs, kw, vw, sel, flags, zg, o1, B, S)
    return [o3], [w_out.astype(BF16)]


def kernel(x, ev_attn_norm, ev_w_in, ev_q_norm, ev_kv_norm, ev_w_uq, ev_w_ukv, ev_conv_w, ev_a_log, ev_dt_bias, ev_gdn_norm, ev_w_out, od_attn_norm, od_w_in, od_pe_k, od_w1_k, od_w2_k, od_pe_v, od_w1_v, od_w2_v, od_w_out, ffn_norm, ffn_w_gate, ffn_w_up, ffn_w_down, final_norm):
    B, S, D = x.shape
    depth = ffn_norm.shape[0]
    h = x.reshape(B * S, D)
    tables = _rope_tables(S)
    wg_all, wu_all, wd_all = ffn_w_gate.astype(BF16), ffn_w_up.astype(BF16), ffn_w_down.astype(BF16)
    for layer in range(depth):
        i = layer // 2
        if layer % 2 == 0:
            mix, w_mix = even_mixer_layer(h, B, S, ev_attn_norm[i], ev_w_in[i], ev_q_norm[i], ev_kv_norm[i],
                                          ev_w_uq[i], ev_w_ukv[i], ev_conv_w[i], ev_a_log[i], ev_dt_bias[i],
                                          ev_gdn_norm[i], ev_w_out[i], tables)
        else:
            mix, w_mix = odd_mixer_layer(h, B, S, od_attn_norm[i], od_w_in[i], od_pe_k[i], od_w1_k[i], od_w2_k[i],
                                         od_pe_v[i], od_w1_v[i], od_w2_v[i], od_w_out[i])
        h = proj_ffn(mix, w_mix, h, ffn_norm[layer], wg_all[layer], wu_all[layer], wd_all[layer],
                     final_g=final_norm if layer == depth - 1 else None)
    return h.reshape(B, S, D)
```

```python
import functools
import math

import jax
import jax.numpy as jnp
import numpy as np
from jax import lax
from jax.experimental import pallas as pl
from jax.experimental.pallas import tpu as pltpu

F32 = jnp.float32
BF16 = jnp.bfloat16

EPS = 1e-6
NEG = -1e30
BIG = 1e30
LANES = 128

MLA_HEADS = 8
MLA_Q_RANK = 384
MLA_KV_RANK = 256
MLA_NOPE = 64
MLA_ROPE = 32
MLA_V = 64
ROPE_BASE = 10000.0
GDN_HEADS = 4
GDN_DK = 128
GDN_DV = 128
GDN_CONV = 4
GDN_CHUNK = 64
NSA_HEADS = 16
NSA_GROUPS = 4
NSA_HPG = 4
NSA_DIM = 64
CMP_BLOCK = 32
CMP_STRIDE = 16
CMP_HIDDEN = 256
SEL_BLOCK = 64
SEL_TOPN = 16
WINDOW = 512

LOG2E = math.log2(math.e)
VMEM_LIMIT = 56 * 1024 * 1024
HIGHEST = lax.Precision.HIGHEST


def _cparams(sem):
    return pltpu.CompilerParams(dimension_semantics=sem, vmem_limit_bytes=VMEM_LIMIT)


def _dot(a, b):
    return jnp.dot(a.astype(BF16), b.astype(BF16), preferred_element_type=F32)


def _dot_nt(a, b):
    return lax.dot_general(a.astype(BF16), b.astype(BF16), (((1,), (1,)), ((), ())),
                           preferred_element_type=F32)


def _dot_tn(a, b):
    return lax.dot_general(a.astype(BF16), b.astype(BF16), (((0,), (0,)), ((), ())),
                           preferred_element_type=F32)


def _dot_f32(a, b):
    return jnp.dot(a, b, preferred_element_type=F32, precision=HIGHEST)


def _rms(x, g):
    var = jnp.mean(x * x, axis=-1, keepdims=True)
    return x * lax.rsqrt(var + EPS) * g


def _silu(x):
    return x * (1.0 / (1.0 + jnp.exp(-x)))


def _sigmoid(x):
    return 1.0 / (1.0 + jnp.exp(-x))


def _rms_matmul_kernel(x_ref, g_ref, w_ref, *out_refs, splits):
    xn = _rms(x_ref[...], g_ref[...])
    acc = _dot(xn, w_ref[...])
    off = 0
    for o_ref, n in zip(out_refs, splits):
        o_ref[...] = acc[:, off:off + n].astype(o_ref.dtype)
        off += n


def rms_matmul(x, g, w, splits, tm=512, dtypes=None):
    T, K = x.shape
    N = w.shape[1]
    assert sum(splits) == N and T % tm == 0
    dtypes = dtypes or (F32,) * len(splits)
    return pl.pallas_call(
        functools.partial(_rms_matmul_kernel, splits=splits),
        grid=(T // tm,),
        in_specs=[pl.BlockSpec((tm, K), lambda i: (i, 0)),
                  pl.BlockSpec((1, K), lambda i: (0, 0)),
                  pl.BlockSpec((K, N), lambda i: (0, 0))],
        out_specs=[pl.BlockSpec((tm, n), lambda i: (i, 0)) for n in splits],
        out_shape=[jax.ShapeDtypeStruct((T, n), dt) for n, dt in zip(splits, dtypes)],
        compiler_params=_cparams(("parallel",)),
        name="rms_matmul",
    )(x, g.reshape(1, K), w)


def _proj_ffn_kernel(*refs, n_in, chunks, final_norm):
    a_refs = refs[:n_in]
    w_refs = refs[n_in:2 * n_in]
    res_ref, g_ref, wg_ref, wu_ref, wd_ref = refs[2 * n_in:2 * n_in + 5]
    fg_ref = refs[2 * n_in + 5] if final_norm else None
    o_ref = refs[-1]
    proj = functools.reduce(lambda x, y: x + y, [jnp.dot(a_ref[...], w_ref[...], preferred_element_type=F32)
                                                 for a_ref, w_ref in zip(a_refs, w_refs)])
    h = res_ref[...] + proj
    xn = _rms(h, g_ref[...]).astype(BF16)
    acc = h
    off = 0
    for n in chunks:
        gate = jnp.dot(xn, wg_ref[:, off:off + n], preferred_element_type=F32)
        up = jnp.dot(xn, wu_ref[:, off:off + n], preferred_element_type=F32)
        act = (_silu(gate) * up).astype(BF16)
        acc = acc + jnp.dot(act, wd_ref[off:off + n, :], preferred_element_type=F32)
        off += n
    o_ref[...] = _rms(acc, fg_ref[...]) if final_norm else acc


def proj_ffn(a_list, w_list, res, g, wg, wu, wd, final_g=None, tm=512):
    T, D = res.shape
    Hd = wg.shape[1]
    n_in = len(a_list)
    nch = 2 if (Hd % 256 == 0) else 1
    chunks = (Hd // nch,) * nch
    single = pl.Buffered(1)
    row = lambda n: pl.BlockSpec((tm, n), lambda i: (i, 0))
    const = lambda a: pl.BlockSpec(a.shape, lambda i: (0, 0))
    in_specs = [row(a.shape[1]) for a in a_list] + [const(w) for w in w_list]
    in_specs += [row(D), pl.BlockSpec((1, D), lambda i: (0, 0)),
                 pl.BlockSpec((D, Hd), lambda i: (0, 0), pipeline_mode=single),
                 pl.BlockSpec((D, Hd), lambda i: (0, 0), pipeline_mode=single),
                 pl.BlockSpec((Hd, D), lambda i: (0, 0), pipeline_mode=single)]
    args = [*a_list, *w_list, res, g.reshape(1, D), wg, wu, wd]
    if final_g is not None:
        in_specs.append(pl.BlockSpec((1, D), lambda i: (0, 0)))
        args.append(final_g.reshape(1, D))
    return pl.pallas_call(
        functools.partial(_proj_ffn_kernel, n_in=n_in, chunks=chunks, final_norm=final_g is not None),
        grid=(T // tm,),
        in_specs=in_specs,
        out_specs=row(D),
        out_shape=jax.ShapeDtypeStruct((T, D), F32),
        compiler_params=_cparams(("parallel",)),
        name="proj_ffn",
    )(*args)


MLA_HB = 128
MLA_FLASH_HEADS = 8


def _mla_prep_kernel(zm_ref, qn_ref, kvn_ref, wq_ref, wqr_ref, wk_ref, wv_ref, c_ref, s_ref,
                     q_out, k_out, v_out):
    zm = zm_ref[...]
    cq = zm[:, :MLA_Q_RANK]
    ckv = zm[:, MLA_Q_RANK:MLA_Q_RANK + MLA_KV_RANK]
    m1 = zm[:, 640:768]
    m2 = zm[:, 768:896]
    cqn = _rms(cq, qn_ref[...]).astype(BF16)
    ckvn = _rms(ckv, kvn_ref[...]).astype(BF16)
    q = jnp.dot(cqn, wq_ref[...], preferred_element_type=F32)
    qr = jnp.dot(cqn, wqr_ref[...], preferred_element_type=F32)
    kn = jnp.dot(ckvn, wk_ref[...], preferred_element_type=F32)
    cos = c_ref[...]
    sin = s_ref[...]
    lane = lax.broadcasted_iota(jnp.int32, cos.shape, 1)
    rope_lane = (lane >= MLA_NOPE) & (lane < MLA_NOPE + MLA_ROPE)
    krot = jnp.where(rope_lane, m1 * cos + m2 * sin, 0.0)
    scale = (MLA_NOPE + MLA_ROPE) ** -0.5 * LOG2E
    for h in range(MLA_HEADS):
        sl = slice(h * MLA_HB, (h + 1) * MLA_HB)
        q_out[:, sl] = ((q[:, sl] * cos + qr[:, sl] * sin) * scale).astype(BF16)
        k_out[:, sl] = (kn[:, sl] + krot).astype(BF16)
    vlane = lax.broadcasted_iota(jnp.int32, (1, MLA_HEADS * LANES), 1)
    ones_half = ((vlane // LANES) % 2 == 0) == ((vlane % LANES) >= MLA_V)
    v = jnp.dot(ckvn, wv_ref[...], preferred_element_type=F32)
    v_out[...] = jnp.where(ones_half, 1.0, v).astype(BF16)


def mla_prep(zm, qn, kvn, wq, wqr, wk, wv, cos_t, sin_t, S, tm=512):
    T = zm.shape[0]
    nsb = S // tm
    HW = MLA_HEADS * MLA_HB
    full = lambda a: pl.BlockSpec(a.shape, lambda i: (0, 0))
    return pl.pallas_call(
        _mla_prep_kernel,
        grid=(T // tm,),
        in_specs=[pl.BlockSpec((tm, zm.shape[1]), lambda i: (i, 0)),
                  full(qn), full(kvn), full(wq), full(wqr), full(wk), full(wv),
                  pl.BlockSpec((tm, LANES), lambda i: (i % nsb, 0)),
                  pl.BlockSpec((tm, LANES), lambda i: (i % nsb, 0))],
        out_specs=[pl.BlockSpec((tm, HW), lambda i: (i, 0)),
                   pl.BlockSpec((tm, HW), lambda i: (i, 0)),
                   pl.BlockSpec((tm, HW), lambda i: (i, 0))],
        out_shape=[jax.ShapeDtypeStruct((T, HW), BF16),
                   jax.ShapeDtypeStruct((T, HW), BF16),
                   jax.ShapeDtypeStruct((T, HW), BF16)],
        compiler_params=_cparams(("parallel",)),
        name="mla_prep",
    )(zm, qn, kvn, wq, wqr, wk, wv, cos_t, sin_t)


def _mla_flash_kernel(q_ref, k_ref, v_ref, o_ref, m_scr, acc_scr, *, tq, tk):
    i = pl.program_id(2)
    per_q = tq // tk
    m_scr[...] = jnp.full(m_scr.shape, NEG, F32)
    acc_scr[...] = jnp.zeros(acc_scr.shape, F32)
    heads = range(MLA_FLASH_HEADS)
    lane_tiles = range(tk // LANES)

    def update(j, diag):
        rows = pl.ds(pl.multiple_of(j * tk, tk), tk)
        s = [lax.dot_general(q_ref[:, h * MLA_HB:(h + 1) * MLA_HB], k_ref[rows, h * MLA_HB:(h + 1) * MLA_HB],
                             (((1,), (1,)), ((), ())), preferred_element_type=F32) for h in heads]
        if diag is not None:
            qpos = lax.broadcasted_iota(jnp.int32, (tq, tk), 0)
            kpos = lax.broadcasted_iota(jnp.int32, (tq, tk), 1) + diag * tk
            mask = kpos <= qpos
            s = [jnp.where(mask, x, NEG) for x in s]
        new_m = []
        pv = []
        for h in heads:
            tiles = [s[h][:, c * LANES:(c + 1) * LANES] for c in lane_tiles]
            tile_max = functools.reduce(jnp.maximum, tiles)
            m_new = jnp.maximum(m_scr[h], jnp.max(tile_max, axis=1, keepdims=True))
            p = jnp.concatenate([jnp.exp2(t - m_new) for t in tiles], axis=1).astype(BF16)
            pv.append(jnp.dot(p, v_ref[rows, h * LANES:(h + 1) * LANES], preferred_element_type=F32))
            new_m.append(m_new)
        for h in heads:
            acc_scr[h] = jnp.exp2(m_scr[h] - new_m[h]) * acc_scr[h] + pv[h]
            m_scr[h] = new_m[h]

    def body(j, carry):
        update(j, None)
        return carry

    lax.fori_loop(0, i * per_q, body, jnp.int32(0))
    for d in range(per_q):
        update(i * per_q + d, d)
    lane = lax.broadcasted_iota(jnp.int32, (tq, LANES), 1)
    for pr in range(MLA_FLASH_HEADS // 2):
        a0 = acc_scr[2 * pr]
        a1 = acc_scr[2 * pr + 1]
        o0 = a0 * (1.0 / pltpu.roll(a0, MLA_V, 1))
        o1 = a1 * (1.0 / pltpu.roll(a1, MLA_V, 1))
        o_ref[:, pr * LANES:(pr + 1) * LANES] = jnp.where(lane < MLA_V, o0, o1).astype(o_ref.dtype)


def mla_flash(q, k, v, B, S, tq=512, tk=512):
    T = q.shape[0]
    tk = min(tk, tq)
    nq = S // tq
    nh = MLA_FLASH_HEADS
    return pl.pallas_call(
        functools.partial(_mla_flash_kernel, tq=tq, tk=tk),
        grid=(B, MLA_HEADS // nh, nq),
        in_specs=[pl.BlockSpec((tq, nh * MLA_HB), lambda b, p, i: (b * nq + i, p)),
                  pl.BlockSpec((S, nh * MLA_HB), lambda b, p, i: (b, p), pipeline_mode=pl.Buffered(1)),
                  pl.BlockSpec((S, nh * LANES), lambda b, p, i: (b, p), pipeline_mode=pl.Buffered(1))],
        out_specs=pl.BlockSpec((tq, nh * MLA_V), lambda b, p, i: (b * nq + i, p)),
        out_shape=jax.ShapeDtypeStruct((T, MLA_HEADS * MLA_V), BF16),
        scratch_shapes=[pltpu.VMEM((nh, tq, LANES), F32),
                        pltpu.VMEM((nh, tq, LANES), F32)],
        compiler_params=_cparams(("parallel", "parallel", "arbitrary")),
        name="mla_flash",
    )(q, k, v)


GDN_QKV = 3 * GDN_HEADS * GDN_DK
BETA_LANE = 96
DECAY_LANE = 100
HALO = 8


def _gdn_prep_kernel(z_ref, halo_ref, m1_ref, cw_ref, alog_ref, dt_ref, qkv_out, gb_out, gbt_out,
                     *, tm, tiles_per_seq):
    i = pl.program_id(0)
    x = z_ref[...]
    halo = halo_ref[...]
    halo = jnp.where(i % tiles_per_seq == 0, jnp.zeros_like(halo), halo)
    xe = jnp.concatenate([halo, x], axis=0)
    cw = cw_ref[...]
    acc = x * cw[GDN_CONV - 1:GDN_CONV, :]
    for d in range(1, GDN_CONV):
        acc = acc + xe[HALO - d:HALO - d + tm, :] * cw[GDN_CONV - 1 - d:GDN_CONV - d, :]
    y = _silu(acc)
    nq = GDN_HEADS * GDN_DK
    for h in range(GDN_HEADS):
        sl = slice(h * GDN_DK, (h + 1) * GDN_DK)
        qh = y[:, sl]
        qkv_out[:, sl] = qh * lax.rsqrt(jnp.sum(qh * qh, axis=-1, keepdims=True) + EPS) * (GDN_DK ** -0.5)
        sl2 = slice(nq + h * GDN_DK, nq + (h + 1) * GDN_DK)
        kh = y[:, sl2]
        qkv_out[:, sl2] = kh * lax.rsqrt(jnp.sum(kh * kh, axis=-1, keepdims=True) + EPS)
    qkv_out[:, 2 * nq:] = y[:, 2 * nq:]
    m1 = m1_ref[...]
    lane = lax.broadcasted_iota(jnp.int32, m1.shape, 1)
    beta = _sigmoid(m1)
    xa = m1 + dt_ref[...]
    softplus = jnp.maximum(xa, 0.0) + jnp.log(1.0 + jnp.exp(-jnp.abs(xa)))
    decay = -jnp.exp(alog_ref[...]) * softplus
    ri = lax.broadcasted_iota(jnp.int32, (tm, tm), 0)
    ci = lax.broadcasted_iota(jnp.int32, (tm, tm), 1)
    ltri = jnp.where((ri >= ci) & (ri // GDN_CHUNK == ci // GDN_CHUNK), 1.0, 0.0).astype(BF16)
    d_hi = decay.astype(BF16)
    rem = decay - d_hi.astype(F32)
    d_mid = rem.astype(BF16)
    d_lo = (rem - d_mid.astype(F32)).astype(BF16)
    gcum = (jnp.dot(ltri, d_hi, preferred_element_type=F32) + jnp.dot(ltri, d_mid, preferred_element_type=F32)
            + jnp.dot(ltri, d_lo, preferred_element_type=F32))
    gb = jnp.where(lane < DECAY_LANE, beta, gcum)
    gb_out[...] = gb
    r = lax.broadcasted_iota(jnp.int32, (8, LANES), 0)
    c = lax.broadcasted_iota(jnp.int32, (8, LANES), 1)
    pick = (c == r + BETA_LANE).astype(F32)
    gbt_out[...] = lax.dot_general(pick, gb, (((1,), (1,)), ((), ())),
                                   preferred_element_type=F32, precision=HIGHEST)


def gdn_prep(zg, zm, conv_w, alog_v, dt_v, S, tm=512):
    T = zg.shape[0]
    tiles_per_seq = S // tm
    hb = tm // HALO
    m1_blk = 640 // LANES
    return pl.pallas_call(
        functools.partial(_gdn_prep_kernel, tm=tm, tiles_per_seq=tiles_per_seq),
        grid=(T // tm,),
        in_specs=[pl.BlockSpec((tm, GDN_QKV), lambda i: (i, 0)),
                  pl.BlockSpec((HALO, GDN_QKV), lambda i: (jnp.maximum(i * hb - 1, 0), 0)),
                  pl.BlockSpec((tm, LANES), lambda i: (i, m1_blk)),
                  pl.BlockSpec((GDN_CONV, GDN_QKV), lambda i: (0, 0)),
                  pl.BlockSpec((1, LANES), lambda i: (0, 0)),
                  pl.BlockSpec((1, LANES), lambda i: (0, 0))],
        out_specs=[pl.BlockSpec((tm, GDN_QKV), lambda i: (i, 0)),
                   pl.BlockSpec((tm, LANES), lambda i: (i, 0)),
                   pl.BlockSpec((8, tm), lambda i: (0, i))],
        out_shape=[jax.ShapeDtypeStruct((T, GDN_QKV), F32),
                   jax.ShapeDtypeStruct((T, LANES), F32),
                   jax.ShapeDtypeStruct((8, T), F32)],
        compiler_params=_cparams(("parallel",)),
        name="gdn_prep",
    )(zg, zg, zm, conv_w, alog_v, dt_v)


def _dot3(a, b):
    a_hi = a.astype(BF16)
    a_lo = (a - a_hi.astype(F32)).astype(BF16)
    b_hi = b.astype(BF16)
    b_lo = (b - b_hi.astype(F32)).astype(BF16)
    return (jnp.dot(a_hi, b_hi, preferred_element_type=F32) + jnp.dot(a_hi, b_lo, preferred_element_type=F32)
            + jnp.dot(a_lo, b_hi, preferred_element_type=F32))


def _tri_inverse_all(a_list, eye, diag_blocks):
    ad = [jnp.where(diag_blocks, a, 0.0) for a in a_list]
    ao = [a - d for a, d in zip(a_list, ad)]
    a2 = [_dot(d, d) for d in ad]
    a4 = [_dot(x, x) for x in a2]
    a8 = [_dot(x, x) for x in a4]
    t = [_dot(eye - d, eye + x) for d, x in zip(ad, a2)]
    t = [_dot(y, eye + x) for y, x in zip(t, a4)]
    dinv = [_dot(y, eye + x) for y, x in zip(t, a8)]
    n = [_dot(d, o) for d, o in zip(dinv, ao)]
    n2 = [_dot(x, x) for x in n]
    t = [_dot(eye - x, eye + y) for x, y in zip(n, n2)]
    x0 = [_dot(y, d) for y, d in zip(t, dinv)]
    res = [eye - x - _dot3(a, x) for a, x in zip(a_list, x0)]
    return [x + _dot(x, r) for x, r in zip(x0, res)]


def _gdn_chunk_kernel(qkv_ref, gb_ref, gbt_ref, zg_ref, norm_ref, o_ref, state_scr, *, lb):
    C = GDN_CHUNK
    DK = GDN_DK
    H = GDN_HEADS
    nq = H * DK
    n_chunks = lb // C

    @pl.when(pl.program_id(1) == 0)
    def _():
        state_scr[...] = jnp.zeros(state_scr.shape, F32)

    ii = lax.broadcasted_iota(jnp.int32, (C, C), 0)
    jj = lax.broadcasted_iota(jnp.int32, (C, C), 1)
    lower = ii >= jj
    strict = ii > jj
    eye = (ii == jj).astype(F32)
    diag_blocks = (ii // 16) == (jj // 16)
    norm_w = norm_ref[...]

    items = [(c, h) for c in range(n_chunks) for h in range(H)]
    rows = lambda c: slice(c * C, (c + 1) * C)
    q = [qkv_ref[rows(c), h * DK:(h + 1) * DK] for c, h in items]
    k = [qkv_ref[rows(c), nq + h * DK:nq + (h + 1) * DK] for c, h in items]
    v = [qkv_ref[rows(c), 2 * nq + h * DK:2 * nq + (h + 1) * DK] for c, h in items]
    beta = [jnp.broadcast_to(gb_ref[rows(c), BETA_LANE + h:BETA_LANE + h + 1], (C, DK)) for c, h in items]
    gc = [jnp.broadcast_to(gb_ref[rows(c), DECAY_LANE + h:DECAY_LANE + h + 1], (C, DK)) for c, h in items]
    gr = [jnp.broadcast_to(gbt_ref[4 + h:5 + h, rows(c)], (C, C)) for c, h in items]
    decay = [jnp.exp(jnp.where(lower, x[:, :C] - y, NEG)) for x, y in zip(gc, gr)]
    eg = [jnp.exp(x) for x in gc]
    kb = [x * b for x, b in zip(k, beta)]
    a = [jnp.where(strict, _dot_nt(x, y) * d, 0.0) for x, y, d in zip(kb, k, decay)]
    t_inv = _tri_inverse_all(a, eye, diag_blocks)
    uw = [_dot(t, jnp.concatenate([x * b, y * e], axis=1))
          for t, x, b, y, e in zip(t_inv, v, beta, kb, eg)]
    intra = [_dot_nt(x, y) * d for x, y, d in zip(q, k, decay)]
    g_last = [x[C - 1:C, :] for x in gc]
    k_dec = [x * jnp.exp(gl - g) for x, gl, g in zip(k, g_last, gc)]
    wq = [jnp.concatenate([x[:, DK:], y * e], axis=0) for x, y, e in zip(uw, q, eg)]

    states = [state_scr[h] for h in range(H)]
    for c in range(n_chunks):
        idx = [c * H + h for h in range(H)]
        ws = [_dot(wq[i], states[h]) for h, i in enumerate(idx)]
        v_new = [uw[i][:, :DK] - y[:C] for i, y in zip(idx, ws)]
        o = [y[C:] + _dot(intra[i], vn) for i, y, vn in zip(idx, ws, v_new)]
        states = [s * jnp.exp(g_last[i]) + _dot_tn(k_dec[i], vn) for s, i, vn in zip(states, idx, v_new)]
        for h in range(H):
            on = o[h] * lax.rsqrt(jnp.mean(o[h] * o[h], axis=-1, keepdims=True) + EPS) * norm_w
            o_ref[rows(c), h * DK:(h + 1) * DK] = (on * _silu(zg_ref[rows(c), h * DK:(h + 1) * DK])
                                                   ).astype(o_ref.dtype)
    for h in range(H):
        state_scr[h] = states[h]


def gdn_chunk(qkv, gb, gbt, zg, norm_w, B, S, lb=512):
    T = qkv.shape[0]
    nsb = S // lb
    VW = GDN_HEADS * GDN_DV
    zg_blk = GDN_QKV // VW
    return pl.pallas_call(
        functools.partial(_gdn_chunk_kernel, lb=lb),
        grid=(B, nsb),
        in_specs=[pl.BlockSpec((lb, GDN_QKV), lambda b, s: (b * nsb + s, 0)),
                  pl.BlockSpec((lb, LANES), lambda b, s: (b * nsb + s, 0)),
                  pl.BlockSpec((8, lb), lambda b, s: (0, b * nsb + s)),
                  pl.BlockSpec((lb, VW), lambda b, s: (b * nsb + s, zg_blk)),
                  pl.BlockSpec((1, GDN_DV), lambda b, s: (0, 0))],
        out_specs=pl.BlockSpec((lb, VW), lambda b, s: (b * nsb + s, 0)),
        out_shape=jax.ShapeDtypeStruct((T, VW), BF16),
        scratch_shapes=[pltpu.VMEM((GDN_HEADS, GDN_DK, GDN_DV), F32)],
        compiler_params=_cparams(("parallel", "arbitrary")),
        name="gdn_chunk",
    )(qkv, gb, gbt, zg, norm_w)


def _compress_kernel(xl_ref, xh_ref, pea_ref, peb_ref, w1a_ref, w1b_ref, w2_ref, o_ref, p1_scr, p2_scr,
                     *, per_step):
    kc = pl.program_id(1)
    nr = p1_scr.shape[0]
    width = xl_ref.shape[1] + xh_ref.shape[1]

    @pl.when(kc == 0)
    def _():
        p1_scr[...] = jnp.zeros(p1_scr.shape, F32)
        p2_scr[...] = jnp.zeros(p2_scr.shape, F32)

    for t in range(per_step):
        rows = pl.ds(kc * per_step + t, nr, stride=CMP_STRIDE)
        x = jnp.concatenate([xl_ref[rows, :], xh_ref[rows, :]], axis=1)
        cols = slice(t * width, (t + 1) * width)
        p1_scr[...] += _dot(x + pea_ref[:, cols], w1a_ref[cols, :])
        p2_scr[...] += _dot(x + peb_ref[:, cols], w1b_ref[cols, :])

    @pl.when(kc == pl.num_programs(1) - 1)
    def _():
        p2 = p2_scr[...]
        rows = p2.shape[0]
        hid = p1_scr[...] + pltpu.roll(p2, rows - 1, 0)
        c0 = math.sqrt(2.0 / math.pi)
        act = 0.5 * hid * (1.0 + jnp.tanh(c0 * (hid + 0.044715 * (hid * hid * hid))))
        o_ref[0] = _dot(act, w2_ref[...])


def nsa_compress(x, pe_a, pe_b, w1a, w1b, w2, B, S, per_step=2):
    nr = S // CMP_STRIDE
    width = x.shape[1]
    kchunk = per_step * width
    KW = CMP_STRIDE * width
    HW = NSA_GROUPS * CMP_HIDDEN
    OW = NSA_GROUPS * NSA_DIM
    return pl.pallas_call(
        functools.partial(_compress_kernel, per_step=per_step),
        grid=(B, KW // kchunk),
        in_specs=[pl.BlockSpec((S, LANES), lambda b, k: (b, 0)),
                  pl.BlockSpec((S, LANES), lambda b, k: (b, 1)),
                  pl.BlockSpec((1, kchunk), lambda b, k: (0, k)),
                  pl.BlockSpec((1, kchunk), lambda b, k: (0, k)),
                  pl.BlockSpec((kchunk, HW), lambda b, k: (k, 0)),
                  pl.BlockSpec((kchunk, HW), lambda b, k: (k, 0)),
                  pl.BlockSpec((HW, OW), lambda b, k: (0, 0))],
        out_specs=pl.BlockSpec((1, nr, OW), lambda b, k: (b, 0, 0)),
        out_shape=jax.ShapeDtypeStruct((B, nr, OW), F32),
        scratch_shapes=[pltpu.VMEM((nr, HW), F32), pltpu.VMEM((nr, HW), F32)],
        compiler_params=_cparams(("parallel", "arbitrary")),
        name="nsa_compress",
    )(x, x, pe_a, pe_b, w1a, w1b, w2)


def _slope(h):
    return float(2.0 ** (-8.0 * (h + 1) / NSA_HEADS))


SEL_TK = 2 * SEL_BLOCK
SEL_NT = 4
SEL_TQ = 256


def _gate_expand(branch):
    e = np.zeros((LANES, NSA_HEADS * NSA_DIM), np.float32)
    for h in range(NSA_HEADS):
        e[3 * h + branch, h * NSA_DIM:(h + 1) * NSA_DIM] = 1.0
    return jnp.asarray(e).astype(BF16)


def _gate_matrix(zg, expand):
    g = _sigmoid(zg)
    g_hi = g.astype(BF16)
    rem = g - g_hi.astype(F32)
    g_mid = rem.astype(BF16)
    g_lo = (rem - g_mid.astype(F32)).astype(BF16)
    dot = lambda a: jnp.dot(a, expand, preferred_element_type=F32)
    return (dot(g_hi) + dot(g_mid)) + dot(g_lo)


def _value_blocks(vg):
    ones = jnp.ones_like(vg)
    return jnp.concatenate([vg, ones], axis=1), jnp.concatenate([ones, vg], axis=1)


def _nsa_cmp_kernel(q_ref, kc_ref, vc_ref, zg_ref, ge_ref, ov_ref, o_ref, sel_ref, flag_ref, work_scr,
                    *, tq, n_sel):
    i = pl.program_id(1)
    n_tiles = n_sel * SEL_BLOCK // SEL_TK
    blk_tile = (lax.broadcasted_iota(jnp.int32, (n_sel, n_tiles), 0) * SEL_BLOCK // SEL_TK
                == lax.broadcasted_iota(jnp.int32, (n_sel, n_tiles), 1))
    to_tile = jnp.where(blk_tile, 1.0, 0.0).astype(BF16)
    ncmp = kc_ref.shape[1]
    D = NSA_DIM
    qpos = i * tq + lax.broadcasted_iota(jnp.int32, (tq, 1), 0)
    nidx = lax.broadcasted_iota(jnp.int32, (1, ncmp), 1)
    valid = (nidx * CMP_STRIDE + (CMP_BLOCK - 1)) <= qpos
    any_valid = (qpos >= CMP_BLOCK - 1).astype(F32)
    centre_rel = (nidx * CMP_STRIDE - i * tq).astype(F32) + 0.5 * (CMP_BLOCK - 1)
    gate_mat = _gate_matrix(zg_ref[...], ge_ref[...])
    kc = kc_ref[0].astype(BF16)
    vc = vc_ref[0].astype(BF16)
    ov_t = ov_ref[...]
    jf = lax.broadcasted_iota(jnp.int32, (n_sel, 1), 0).astype(F32)
    qblk = ((i * tq + lax.broadcasted_iota(jnp.int32, (1, tq), 1)) // SEL_BLOCK).astype(F32)
    forced = (jf == 0.0) | (jf == qblk) | (jf == qblk - 1.0)
    causal_blk = jf <= qblk
    heads = range(NSA_HPG)
    groups = range(NSA_GROUPS)
    lane_half = lax.broadcasted_iota(jnp.int32, (tq, LANES), 1) < D
    assert ncmp % LANES == 0

    def scores(width):
        for g in groups:
            kg = kc[:width, g * D:(g + 1) * D]
            vg = vc[:width, g * D:(g + 1) * D]
            hs = [g * NSA_HPG + r for r in heads]
            s = [_dot_nt(q_ref[:, h * D:(h + 1) * D], kg) + (_slope(h) * LOG2E) * centre_rel[:, :width] for h in hs]
            s = [jnp.where(valid[:, :width], x, NEG) for x in s]
            v_even, v_odd = _value_blocks(vg)
            p, o = [], []
            for r, h in enumerate(hs):
                tiles = [s[r][:, c * LANES:(c + 1) * LANES] for c in range(width // LANES)]
                tile_max = functools.reduce(jnp.maximum, tiles)
                m = jnp.maximum(jnp.full((tq, LANES), NEG, F32), jnp.max(tile_max, axis=1, keepdims=True))
                e_tiles = [jnp.exp2(t - m) for t in tiles]
                pv = jnp.dot(jnp.concatenate(e_tiles, axis=1).astype(BF16), v_odd if r % 2 else v_even,
                             preferred_element_type=F32)
                rolled = pltpu.roll(pv, D, 1)
                row_sum = jnp.where(lane_half, pv, rolled) if r % 2 else jnp.where(lane_half, rolled, pv)
                norm = any_valid / row_sum
                o.append(pv * norm)
                p.append([t * norm for t in e_tiles])
            for t in range(NSA_HPG // 2):
                blk = slice((g * NSA_HPG // 2 + t) * LANES, (g * NSA_HPG // 2 + t + 1) * LANES)
                o_ref[:, blk] = gate_mat[:, blk] * jnp.where(lane_half, o[2 * t], o[2 * t + 1])
            psum = jnp.concatenate([(a + b) + (c + d) for a, b, c, d in zip(*p)], axis=1)
            p_hi = psum.astype(BF16)
            rem = psum - p_hi.astype(F32)
            p_mid = rem.astype(BF16)
            p_lo = (rem - p_mid.astype(F32)).astype(BF16)
            ov_w = ov_t[:, :width]
            imp_t = (_dot_nt(ov_w, p_hi) + _dot_nt(ov_w, p_mid)) + _dot_nt(ov_w, p_lo)
            work_scr[g] = jnp.where(forced, BIG, jnp.where(causal_blk, imp_t, NEG))

    n_valid = jnp.maximum((i * tq + tq - CMP_BLOCK) // CMP_STRIDE + 1, 1)
    tiles_needed = jnp.minimum((n_valid + LANES - 1) // LANES, ncmp // LANES)
    for nt in range(1, ncmp // LANES + 1):
        pl.when(tiles_needed == nt)(functools.partial(scores, nt * LANES))
    work = [work_scr[g] for g in groups]
    selm = [jnp.zeros((n_sel, tq), F32) for _ in groups]
    for _ in range(min(SEL_TOPN, n_sel)):
        mx = [jnp.max(w, axis=0, keepdims=True) for w in work]
        first = [jnp.min(jnp.where(w == m, jf, float(n_sel)), axis=0, keepdims=True) for w, m in zip(work, mx)]
        pick = [jf == f for f in first]
        selm = [jnp.where(pk, 1.0, sm) for pk, sm in zip(pick, selm)]
        work = [jnp.where(pk, -jnp.inf, w) for pk, w in zip(pick, work)]
    for g in groups:
        selb = selm[g].T.astype(BF16)
        sel_ref[:, g * n_sel:(g + 1) * n_sel] = selb
        tile_hits = jnp.dot(selb, to_tile, preferred_element_type=F32)
        for part in range(tq // SEL_TQ):
            hits = jnp.max(tile_hits[part * SEL_TQ:(part + 1) * SEL_TQ], axis=0, keepdims=True)
            flag_ref[part, g:g + 1, :] = (hits > 0.5).astype(jnp.int32)


def nsa_cmp(q, k_cmp, v_cmp, zg, overlap, B, S, tq=SEL_TQ):
    T = q.shape[0]
    tq = min(tq, S)
    nq = S // tq
    parts = tq // SEL_TQ
    n_sel = S // SEL_BLOCK
    ncmp = k_cmp.shape[1]
    QW = NSA_HEADS * NSA_DIM
    KW = NSA_GROUPS * NSA_DIM
    return pl.pallas_call(
        functools.partial(_nsa_cmp_kernel, tq=tq, n_sel=n_sel),
        grid=(B, nq),
        in_specs=[pl.BlockSpec((tq, QW), lambda b, i: (b * nq + i, 0)),
                  pl.BlockSpec((1, ncmp, KW), lambda b, i: (b, 0, 0)),
                  pl.BlockSpec((1, ncmp, KW), lambda b, i: (b, 0, 0)),
                  pl.BlockSpec((tq, LANES), lambda b, i: (b * nq + i, 0)),
                  pl.BlockSpec((LANES, QW), lambda b, i: (0, 0)),
                  pl.BlockSpec((n_sel, ncmp), lambda b, i: (0, 0))],
        out_specs=[pl.BlockSpec((tq, QW), lambda b, i: (b * nq + i, 0)),
                   pl.BlockSpec((tq, NSA_GROUPS * n_sel), lambda b, i: (b * nq + i, 0)),
                   pl.BlockSpec((parts, NSA_GROUPS, S // SEL_TK), lambda b, i: (b * nq + i, 0, 0))],
        out_shape=[jax.ShapeDtypeStruct((T, QW), F32),
                   jax.ShapeDtypeStruct((T, NSA_GROUPS * n_sel), BF16),
                   jax.ShapeDtypeStruct((B * nq * parts, NSA_GROUPS, S // SEL_TK), jnp.int32)],
        scratch_shapes=[pltpu.VMEM((NSA_GROUPS, n_sel, tq), F32)],
        compiler_params=_cparams(("parallel", "parallel")),
        name="nsa_cmp",
    )(q, k_cmp, v_cmp, zg, _gate_expand(0), overlap)


def _nsa_sel_kernel(flags_ref, q_ref, k_ref, v_ref, sel_ref, zg_ref, ge_ref, prev_ref, o_ref,
                    list_smem, m_scr, acc_scr, *, tq, n_sel, nq):
    b = pl.program_id(0)
    i = pl.program_id(1)
    D = NSA_DIM
    n_tiles = n_sel * SEL_BLOCK // SEL_TK
    n_causal = (i * tq + tq - 1) // SEL_TK + 1
    ks_w = SEL_NT * SEL_TK
    qpos = i * tq + lax.broadcasted_iota(jnp.int32, (tq, 1), 0)
    lane_t = lax.broadcasted_iota(jnp.int32, (1, SEL_TK), 1)
    blk_iota = lax.broadcasted_iota(jnp.int32, (n_sel, ks_w), 0)
    gate_mat = _gate_matrix(zg_ref[...], ge_ref[...])
    lane_half = lax.broadcasted_iota(jnp.int32, (tq, LANES), 1) < D

    for g in range(NSA_GROUPS):
        base = ((b * nq + i) * NSA_GROUPS + g) * n_tiles

        def scan(j, n, base=base):
            list_smem[n] = j
            return n + (flags_ref[base + j] != 0).astype(jnp.int32)

        count = lax.fori_loop(0, n_causal, scan, jnp.int32(0))

        for r in range(NSA_HPG):
            m_scr[r] = jnp.full(m_scr.shape[1:], NEG, F32)
            acc_scr[r] = jnp.zeros(acc_scr.shape[1:], F32)
        qs = [q_ref[:, (g * NSA_HPG + r) * D:(g * NSA_HPG + r + 1) * D] for r in range(NSA_HPG)]
        sel_g = sel_ref[:, g * n_sel:(g + 1) * n_sel]

        def step(st, carry, g=g, count=count, qs=qs, sel_g=sel_g):
            k_parts, v_parts, kpos_parts, kblk_parts = [], [], [], []
            for s in range(SEL_NT):
                idx = st * SEL_NT + s
                j = list_smem[jnp.minimum(idx, count - 1)]
                start = pl.multiple_of(j * SEL_TK, SEL_TK)
                k_parts.append(k_ref[pl.ds(start, SEL_TK), g * D:(g + 1) * D])
                v_parts.append(v_ref[pl.ds(start, SEL_TK), g * D:(g + 1) * D])
                tid = jnp.where(idx < count, j, -1)
                kpos_parts.append(tid * SEL_TK + lane_t)
                kblk_parts.append(tid * (SEL_TK // SEL_BLOCK) + lane_t // SEL_BLOCK)
            k = jnp.concatenate(k_parts, axis=0)
            v = jnp.concatenate(v_parts, axis=0)
            kpos = jnp.concatenate(kpos_parts, axis=1)
            kblk = jnp.concatenate(kblk_parts, axis=1)
            expand = jnp.where(blk_iota == kblk, 1.0, 0.0).astype(BF16)
            picked = jnp.dot(sel_g, expand, preferred_element_type=F32)
            allowed = jnp.where(kpos <= qpos, picked, 0.0) > 0.5
            krel = (kpos - i * tq).astype(F32)
            hs = range(NSA_HPG)
            s_ = [_dot_nt(qs[r], k) + (_slope(g * NSA_HPG + r) * LOG2E) * krel for r in hs]
            s_ = [jnp.where(allowed, x, NEG) for x in s_]
            v_even, v_odd = _value_blocks(v)
            new_m, pv = [], []
            for r in hs:
                tiles = [s_[r][:, c * LANES:(c + 1) * LANES] for c in range(ks_w // LANES)]
                tile_max = functools.reduce(jnp.maximum, tiles)
                m_new = jnp.maximum(m_scr[r], jnp.max(tile_max, axis=1, keepdims=True))
                p = jnp.concatenate([jnp.exp2(t - m_new) for t in tiles], axis=1).astype(BF16)
                pv.append(jnp.dot(p, v_odd if r % 2 else v_even, preferred_element_type=F32))
                new_m.append(m_new)
            for r in hs:
                acc_scr[r] = jnp.exp2(m_scr[r] - new_m[r]) * acc_scr[r] + pv[r]
                m_scr[r] = new_m[r]
            return carry

        lax.fori_loop(0, (count + SEL_NT - 1) // SEL_NT, step, jnp.int32(0))

        o = [acc_scr[r] * (1.0 / pltpu.roll(acc_scr[r], D, 1)) for r in range(NSA_HPG)]
        for t in range(NSA_HPG // 2):
            blk = slice((g * NSA_HPG // 2 + t) * LANES, (g * NSA_HPG // 2 + t + 1) * LANES)
            o_ref[:, blk] = prev_ref[:, blk] + gate_mat[:, blk] * jnp.where(lane_half, o[2 * t], o[2 * t + 1])


def nsa_sel(q, ks, vs, sel, flags, zg, prev, B, S, tq=SEL_TQ):
    T = q.shape[0]
    nq = S // tq
    n_sel = S // SEL_BLOCK
    QW = NSA_HEADS * NSA_DIM
    KW = NSA_GROUPS * NSA_DIM
    qmap = lambda b, i, fl: (b * nq + i, 0)
    kmap = lambda b, i, fl: (b, 0)
    grid_spec = pltpu.PrefetchScalarGridSpec(
        num_scalar_prefetch=1,
        grid=(B, nq),
        in_specs=[pl.BlockSpec((tq, QW), qmap),
                  pl.BlockSpec((S, KW), kmap),
                  pl.BlockSpec((S, KW), kmap),
                  pl.BlockSpec((tq, NSA_GROUPS * n_sel), qmap),
                  pl.BlockSpec((tq, LANES), qmap),
                  pl.BlockSpec((LANES, QW), lambda b, i, fl: (0, 0)),
                  pl.BlockSpec((tq, QW), qmap)],
        out_specs=pl.BlockSpec((tq, QW), qmap),
        scratch_shapes=[pltpu.SMEM((S // SEL_TK,), jnp.int32),
                        pltpu.VMEM((NSA_HPG, tq, LANES), F32),
                        pltpu.VMEM((NSA_HPG, tq, LANES), F32)],
    )
    return pl.pallas_call(
        functools.partial(_nsa_sel_kernel, tq=tq, n_sel=n_sel, nq=nq),
        grid_spec=grid_spec,
        out_shape=jax.ShapeDtypeStruct((T, QW), F32),
        compiler_params=_cparams(("parallel", "arbitrary")),
        name="nsa_sel",
    )(flags.reshape(-1), q, ks, vs, sel, zg, _gate_expand(1), prev)


def _nsa_win_kernel(q_ref, k0_ref, k1_ref, k2_ref, v0_ref, v1_ref, v2_ref, zg_ref, ge_ref, prev_ref, o_ref,
                    *, tq):
    i = pl.program_id(1)
    D = NSA_DIM
    nback = WINDOW // tq
    tkw = (nback + 1) * tq
    k = jnp.concatenate([k0_ref[...], k1_ref[...], k2_ref[...]], axis=0).astype(BF16)
    v = jnp.concatenate([v0_ref[...], v1_ref[...], v2_ref[...]], axis=0).astype(BF16)
    qrel = lax.broadcasted_iota(jnp.int32, (tq, tkw), 0)
    krel = lax.broadcasted_iota(jnp.int32, (tq, tkw), 1) - nback * tq
    dw = qrel - krel
    wvalid = jnp.where(dw >= 0, jnp.where(dw < WINDOW, krel + i * tq, -1), -1) >= 0
    krow = (lax.broadcasted_iota(jnp.int32, (1, tkw), 1) - nback * tq).astype(F32)
    gate_mat = _gate_matrix(zg_ref[...], ge_ref[...])
    lane_half = lax.broadcasted_iota(jnp.int32, (tq, LANES), 1) < D
    heads = range(NSA_HPG)
    for g in range(NSA_GROUPS):
        kg = k[:, g * D:(g + 1) * D]
        vg = v[:, g * D:(g + 1) * D]
        hs = [g * NSA_HPG + r for r in heads]
        s = [_dot_nt(q_ref[:, h * D:(h + 1) * D], kg) + (_slope(h) * LOG2E) * krow for h in hs]
        s = [jnp.where(wvalid, x, NEG) for x in s]
        v_even, v_odd = _value_blocks(vg)
        o = []
        for r, x in enumerate(s):
            tiles = [x[:, c * LANES:(c + 1) * LANES] for c in range(tkw // LANES)]
            tile_max = functools.reduce(jnp.maximum, tiles)
            m = jnp.maximum(jnp.full((tq, LANES), NEG, F32), jnp.max(tile_max, axis=1, keepdims=True))
            e = jnp.concatenate([jnp.exp2(t - m) for t in tiles], axis=1).astype(BF16)
            pv = jnp.dot(e, v_odd if r % 2 else v_even, preferred_element_type=F32)
            o.append(pv * (1.0 / pltpu.roll(pv, D, 1)))
        for t in range(NSA_HPG // 2):
            blk = slice((g * NSA_HPG // 2 + t) * LANES, (g * NSA_HPG // 2 + t + 1) * LANES)
            o_ref[:, blk] = (prev_ref[:, blk] + gate_mat[:, blk] * jnp.where(lane_half, o[2 * t], o[2 * t + 1])
                             ).astype(o_ref.dtype)


def nsa_win(q, kw, vw, zg, prev, B, S, tq=256):
    T = q.shape[0]
    assert WINDOW % tq == 0 and WINDOW // tq == 2
    nq = S // tq
    QW = NSA_HEADS * NSA_DIM
    KW = NSA_GROUPS * NSA_DIM
    qmap = lambda b, i: (b * nq + i, 0)
    back = lambda d: (lambda b, i: (b * nq + jnp.maximum(i - d, 0), 0))
    kspecs = [pl.BlockSpec((tq, KW), back(2)), pl.BlockSpec((tq, KW), back(1)), pl.BlockSpec((tq, KW), back(0))]
    return pl.pallas_call(
        functools.partial(_nsa_win_kernel, tq=tq),
        grid=(B, nq),
        in_specs=[pl.BlockSpec((tq, QW), qmap)] + kspecs + kspecs
                 + [pl.BlockSpec((tq, LANES), qmap), pl.BlockSpec((LANES, QW), lambda b, i: (0, 0)),
                    pl.BlockSpec((tq, QW), qmap)],
        out_specs=pl.BlockSpec((tq, QW), qmap),
        out_shape=jax.ShapeDtypeStruct((T, QW), BF16),
        compiler_params=_cparams(("parallel", "parallel")),
        name="nsa_win",
    )(q, kw, kw, kw, vw, vw, vw, zg, _gate_expand(2), prev)


def _rot_half_cols(w):
    half = w.shape[-1] // 2
    return jnp.concatenate([-w[..., half:], w[..., :half]], axis=-1)


def _even_weights(w_in, w_uq, w_ukv):
    D = w_in.shape[0]
    o = 0
    cuts = {}
    for name, n in (("cq", MLA_Q_RANK), ("ckv", MLA_KV_RANK), ("kr", MLA_ROPE), ("zq", 512), ("zk", 512),
                    ("zv", 512), ("zg", 512), ("zb", GDN_HEADS), ("za", GDN_HEADS)):
        cuts[name] = w_in[:, o:o + n]
        o += n
    z = lambda n: jnp.zeros((D, n), F32)
    misc1 = jnp.concatenate([z(MLA_NOPE), cuts["kr"], cuts["zb"], cuts["za"],
                             z(LANES - MLA_NOPE - MLA_ROPE - 2 * GDN_HEADS)], axis=1)
    misc2 = jnp.concatenate([z(MLA_NOPE), _rot_half_cols(cuts["kr"]), z(LANES - MLA_NOPE - MLA_ROPE)], axis=1)
    w_even = jnp.concatenate([cuts["cq"], cuts["ckv"], misc1, misc2,
                              cuts["zq"], cuts["zk"], cuts["zv"], cuts["zg"]], axis=1).astype(BF16)
    qd = MLA_NOPE + MLA_ROPE
    wq3 = w_uq.reshape(MLA_Q_RANK, MLA_HEADS, qd)
    zq = jnp.zeros((MLA_Q_RANK, MLA_HEADS, MLA_HB - qd), F32)
    wq = jnp.concatenate([wq3, zq], axis=2).reshape(MLA_Q_RANK, MLA_HEADS * MLA_HB).astype(BF16)
    wqr = jnp.concatenate([jnp.zeros((MLA_Q_RANK, MLA_HEADS, MLA_NOPE), F32),
                           _rot_half_cols(wq3[:, :, MLA_NOPE:]), zq], axis=2)
    wqr = wqr.reshape(MLA_Q_RANK, MLA_HEADS * MLA_HB).astype(BF16)
    wkv3 = w_ukv.reshape(MLA_KV_RANK, MLA_HEADS, MLA_NOPE + MLA_V)
    wk = jnp.concatenate([wkv3[:, :, :MLA_NOPE], jnp.zeros((MLA_KV_RANK, MLA_HEADS, MLA_HB - MLA_NOPE), F32)],
                         axis=2).reshape(MLA_KV_RANK, MLA_HEADS * MLA_HB).astype(BF16)
    wv4 = wkv3[:, :, MLA_NOPE:].reshape(MLA_KV_RANK, MLA_HEADS // 2, 2, MLA_V)
    zv = jnp.zeros((MLA_KV_RANK, MLA_HEADS // 2, MLA_V), F32)
    wv = jnp.stack([jnp.concatenate([wv4[:, :, 0], zv], axis=2), jnp.concatenate([zv, wv4[:, :, 1]], axis=2)],
                   axis=2).reshape(MLA_KV_RANK, MLA_HEADS * MLA_HB).astype(BF16)
    return w_even, wq, wqr, wk, wv


def _rope_tables(S):
    half = MLA_ROPE // 2
    inv = ROPE_BASE ** (-jnp.arange(half, dtype=F32) / half)
    ang = jnp.arange(S, dtype=F32)[:, None] * inv[None, :]
    cos = jnp.cos(ang)
    sin = jnp.sin(ang)
    pad = jnp.zeros((S, LANES - MLA_NOPE - MLA_ROPE), F32)
    cos_t = jnp.concatenate([jnp.ones((S, MLA_NOPE), F32), cos, cos, pad], axis=1)
    sin_t = jnp.concatenate([jnp.zeros((S, MLA_NOPE), F32), sin, sin, pad], axis=1)
    return cos_t, sin_t


def _lane_vec(vals, start):
    return jnp.zeros((1, LANES), F32).at[0, start:start + vals.shape[0]].set(vals)


def even_mixer_layer(h, B, S, attn_norm, w_in, q_norm, kv_norm, w_uq, w_ukv, conv_w, a_log, dt_bias,
                     gdn_norm, w_out, tables):
    w_even, wq, wqr, wk, wv = _even_weights(w_in, w_uq, w_ukv)
    zm, zg = rms_matmul(h, attn_norm, w_even, (896, 2048))
    cos_t, sin_t = tables
    q, k, v = mla_prep(zm, q_norm.reshape(1, -1), kv_norm.reshape(1, -1), wq, wqr, wk, wv, cos_t, sin_t, S)
    o_mla = mla_flash(q, k, v, B, S, tq=min(512, S))
    qkv, gb, gbt = gdn_prep(zg, zm, conv_w, _lane_vec(a_log, DECAY_LANE), _lane_vec(dt_bias, DECAY_LANE), S)
    o_gdn = gdn_chunk(qkv, gb, gbt, zg, gdn_norm.reshape(1, -1), B, S)
    nm = MLA_HEADS * MLA_V
    return [o_mla, o_gdn], [w_out[:nm].astype(BF16), w_out[nm:].astype(BF16)]


def _compress_weights(pe, w1, w2):
    G, D = NSA_GROUPS, NSA_DIM
    eye = jnp.eye(G, dtype=F32)
    w1r = w1.reshape(CMP_BLOCK, D, CMP_HIDDEN)

    def expand(wpart):
        return jnp.einsum('ldh,gk->lgdkh', wpart, eye).reshape(CMP_STRIDE * G * D, G * CMP_HIDDEN).astype(BF16)

    def pe_vec(p):
        return jnp.broadcast_to(p[:, None, :], (CMP_STRIDE, G, D)).reshape(1, CMP_STRIDE * G * D)

    w2e = jnp.einsum('hd,gk->ghkd', w2, eye).reshape(G * CMP_HIDDEN, G * D).astype(BF16)
    return (pe_vec(pe[:CMP_STRIDE]), pe_vec(pe[CMP_STRIDE:]), expand(w1r[:CMP_STRIDE]),
            expand(w1r[CMP_STRIDE:]), w2e)


def _overlap_matrix(S):
    nr = S // CMP_STRIDE
    n_sel = S // SEL_BLOCK
    n = np.arange(nr)[:, None]
    j = np.arange(n_sel)[None, :]
    start = n * CMP_STRIDE
    ov = (start <= j * SEL_BLOCK + SEL_BLOCK - 1) & (start + CMP_BLOCK - 1 >= j * SEL_BLOCK)
    ov = ov & (n < nr - 1)
    return jnp.asarray(ov.T.astype(np.float32)).astype(BF16)


def odd_mixer_layer(h, B, S, attn_norm, w_in, pe_k, w1_k, w2_k, pe_v, w1_v, w2_v, w_out):
    D = w_in.shape[0]
    n_g = 3 * NSA_HEADS
    qw = NSA_HEADS * NSA_DIM
    w_odd = jnp.concatenate([w_in[:, :qw] * (NSA_DIM ** -0.5 * LOG2E), w_in[:, qw:],
                             jnp.zeros((D, LANES - n_g), F32)], axis=1).astype(BF16)
    kvw = NSA_GROUPS * NSA_DIM
    q, kc, vc, ks, vs, kw, vw, zg = rms_matmul(
        h, attn_norm, w_odd, (qw,) + (kvw,) * 6 + (LANES,),
        dtypes=(BF16, F32, F32, BF16, BF16, BF16, BF16, F32))
    k_cmp = nsa_compress(kc, *_compress_weights(pe_k, w1_k, w2_k), B, S)
    v_cmp = nsa_compress(vc, *_compress_weights(pe_v, w1_v, w2_v), B, S)
    o1, sel, flags = nsa_cmp(q, k_cmp, v_cmp, zg, _overlap_matrix(S), B, S)
    o2 = nsa_sel(q, ks, vs, sel, flags, zg, o1, B, S)
    o3 = nsa_win(q, kw, vw, zg, o2, B, S)
    return [o3], [w_out.astype(BF16)]


def kernel(x, ev_attn_norm, ev_w_in, ev_q_norm, ev_kv_norm, ev_w_uq, ev_w_ukv, ev_conv_w, ev_a_log, ev_dt_bias, ev_gdn_norm, ev_w_out, od_attn_norm, od_w_in, od_pe_k, od_w1_k, od_w2_k, od_pe_v, od_w1_v, od_w2_v, od_w_out, ffn_norm, ffn_w_gate, ffn_w_up, ffn_w_down, final_norm):
    B, S, D = x.shape
    depth = ffn_norm.shape[0]
    h = x.reshape(B * S, D)
    tables = _rope_tables(S)
    wg_all, wu_all, wd_all = ffn_w_gate.astype(BF16), ffn_w_up.astype(BF16), ffn_w_down.astype(BF16)
    for layer in range(depth):
        i = layer // 2
        if layer % 2 == 0:
            mix, w_mix = even_mixer_layer(h, B, S, ev_attn_norm[i], ev_w_in[i], ev_q_norm[i], ev_kv_norm[i],
                                          ev_w_uq[i], ev_w_ukv[i], ev_conv_w[i], ev_a_log[i], ev_dt_bias[i],
                                          ev_gdn_norm[i], ev_w_out[i], tables)
        else:
            mix, w_mix = odd_mixer_layer(h, B, S, od_attn_norm[i], od_w_in[i], od_pe_k[i], od_w1_k[i], od_w2_k[i],
                                         od_pe_v[i], od_w1_v[i], od_w2_v[i], od_w_out[i])
        h = proj_ffn(mix, w_mix, h, ffn_norm[layer], wg_all[layer], wu_all[layer], wd_all[layer],
                     final_g=final_norm if layer == depth - 1 else None)
    return h.reshape(B, S, D)
```

```python
import functools
import math

import jax
import jax.numpy as jnp
import numpy as np
from jax import lax
from jax.experimental import pallas as pl
from jax.experimental.pallas import tpu as pltpu

F32 = jnp.float32
BF16 = jnp.bfloat16

EPS = 1e-6
NEG = -1e30
BIG = 1e30
LANES = 128

MLA_HEADS = 8
MLA_Q_RANK = 384
MLA_KV_RANK = 256
MLA_NOPE = 64
MLA_ROPE = 32
MLA_V = 64
ROPE_BASE = 10000.0
GDN_HEADS = 4
GDN_DK = 128
GDN_DV = 128
GDN_CONV = 4
GDN_CHUNK = 64
NSA_HEADS = 16
NSA_GROUPS = 4
NSA_HPG = 4
NSA_DIM = 64
CMP_BLOCK = 32
CMP_STRIDE = 16
CMP_HIDDEN = 256
SEL_BLOCK = 64
SEL_TOPN = 16
WINDOW = 512

LOG2E = math.log2(math.e)
FFN_CHUNK = 256
VMEM_LIMIT = 56 * 1024 * 1024
HIGHEST = lax.Precision.HIGHEST


def _cparams(sem):
    return pltpu.CompilerParams(dimension_semantics=sem, vmem_limit_bytes=VMEM_LIMIT)


def _dot(a, b):
    return jnp.dot(a.astype(BF16), b.astype(BF16), preferred_element_type=F32)


def _dot_nt(a, b):
    return lax.dot_general(a.astype(BF16), b.astype(BF16), (((1,), (1,)), ((), ())),
                           preferred_element_type=F32)


def _dot_tn(a, b):
    return lax.dot_general(a.astype(BF16), b.astype(BF16), (((0,), (0,)), ((), ())),
                           preferred_element_type=F32)


def _dot_f32(a, b):
    return jnp.dot(a, b, preferred_element_type=F32, precision=HIGHEST)


def _rms(x, g):
    var = jnp.mean(x * x, axis=-1, keepdims=True)
    return x * lax.rsqrt(var + EPS) * g


def _silu(x):
    return x * (1.0 / (1.0 + jnp.exp(-x)))


def _sigmoid(x):
    return 1.0 / (1.0 + jnp.exp(-x))


def _normalise_pair(even, odd, lower_half, half):
    num = jnp.where(lower_half, even, odd)
    den = jnp.where(lower_half, odd, even)
    return num * (1.0 / pltpu.roll(den, half, 1))


def _rms_matmul_kernel(x_ref, g_ref, w_ref, *out_refs, splits):
    xn = _rms(x_ref[...], g_ref[...])
    acc = _dot(xn, w_ref[...])
    off = 0
    for o_ref, n in zip(out_refs, splits):
        o_ref[...] = acc[:, off:off + n].astype(o_ref.dtype)
        off += n


def rms_matmul(x, g, w, splits, tm=512, dtypes=None):
    T, K = x.shape
    N = w.shape[1]
    assert sum(splits) == N and T % tm == 0
    dtypes = dtypes or (F32,) * len(splits)
    return pl.pallas_call(
        functools.partial(_rms_matmul_kernel, splits=splits),
        grid=(T // tm,),
        in_specs=[pl.BlockSpec((tm, K), lambda i: (i, 0)),
                  pl.BlockSpec((1, K), lambda i: (0, 0)),
                  pl.BlockSpec((K, N), lambda i: (0, 0))],
        out_specs=[pl.BlockSpec((tm, n), lambda i: (i, 0)) for n in splits],
        out_shape=[jax.ShapeDtypeStruct((T, n), dt) for n, dt in zip(splits, dtypes)],
        compiler_params=_cparams(("parallel",)),
        name="rms_matmul",
    )(x, g.reshape(1, K), w)


def _proj_ffn_kernel(*refs, n_in, chunks, final_norm):
    a_refs = refs[:n_in]
    w_refs = refs[n_in:2 * n_in]
    res_ref, g_ref, wg_ref, wu_ref, wd_ref = refs[2 * n_in:2 * n_in + 5]
    fg_ref = refs[2 * n_in + 5] if final_norm else None
    o_ref = refs[-1]
    proj = functools.reduce(lambda x, y: x + y, [jnp.dot(a_ref[...], w_ref[...], preferred_element_type=F32)
                                                 for a_ref, w_ref in zip(a_refs, w_refs)])
    h = res_ref[...] + proj
    xn = _rms(h, g_ref[...]).astype(BF16)
    acc = h
    off = 0
    for n in chunks:
        gate = jnp.dot(xn, wg_ref[:, off:off + n], preferred_element_type=F32)
        up = jnp.dot(xn, wu_ref[:, off:off + n], preferred_element_type=F32)
        act = (_silu(gate) * up).astype(BF16)
        acc = acc + jnp.dot(act, wd_ref[off:off + n, :], preferred_element_type=F32)
        off += n
    o_ref[...] = _rms(acc, fg_ref[...]) if final_norm else acc


def proj_ffn(a_list, w_list, res, g, wg, wu, wd, final_g=None, tm=512):
    T, D = res.shape
    Hd = wg.shape[1]
    n_in = len(a_list)
    nch = Hd // FFN_CHUNK if (Hd % FFN_CHUNK == 0) else 1
    chunks = (Hd // nch,) * nch
    single = pl.Buffered(1)
    row = lambda n: pl.BlockSpec((tm, n), lambda i: (i, 0))
    const = lambda a: pl.BlockSpec(a.shape, lambda i: (0, 0))
    in_specs = [row(a.shape[1]) for a in a_list] + [const(w) for w in w_list]
    in_specs += [row(D), pl.BlockSpec((1, D), lambda i: (0, 0)),
                 pl.BlockSpec((D, Hd), lambda i: (0, 0), pipeline_mode=single),
                 pl.BlockSpec((D, Hd), lambda i: (0, 0), pipeline_mode=single),
                 pl.BlockSpec((Hd, D), lambda i: (0, 0), pipeline_mode=single)]
    args = [*a_list, *w_list, res, g.reshape(1, D), wg, wu, wd]
    if final_g is not None:
        in_specs.append(pl.BlockSpec((1, D), lambda i: (0, 0)))
        args.append(final_g.reshape(1, D))
    return pl.pallas_call(
        functools.partial(_proj_ffn_kernel, n_in=n_in, chunks=chunks, final_norm=final_g is not None),
        grid=(T // tm,),
        in_specs=in_specs,
        out_specs=row(D),
        out_shape=jax.ShapeDtypeStruct((T, D), F32),
        compiler_params=_cparams(("parallel",)),
        name="proj_ffn",
    )(*args)


MLA_HB = 128
MLA_FLASH_HEADS = 8


def _mla_prep_kernel(zm_ref, qn_ref, kvn_ref, wq_ref, wqr_ref, wk_ref, wv_ref, c_ref, s_ref,
                     q_out, k_out, v_out):
    zm = zm_ref[...]
    cq = zm[:, :MLA_Q_RANK]
    ckv = zm[:, MLA_Q_RANK:MLA_Q_RANK + MLA_KV_RANK]
    m1 = zm[:, 640:768]
    m2 = zm[:, 768:896]
    cqn = _rms(cq, qn_ref[...]).astype(BF16)
    ckvn = _rms(ckv, kvn_ref[...]).astype(BF16)
    q = jnp.dot(cqn, wq_ref[...], preferred_element_type=F32)
    qr = jnp.dot(cqn, wqr_ref[...], preferred_element_type=F32)
    kn = jnp.dot(ckvn, wk_ref[...], preferred_element_type=F32)
    cos = c_ref[...]
    sin = s_ref[...]
    lane = lax.broadcasted_iota(jnp.int32, cos.shape, 1)
    rope_lane = (lane >= MLA_NOPE) & (lane < MLA_NOPE + MLA_ROPE)
    krot = jnp.where(rope_lane, m1 * cos + m2 * sin, 0.0)
    scale = (MLA_NOPE + MLA_ROPE) ** -0.5 * LOG2E
    for h in range(MLA_HEADS):
        sl = slice(h * MLA_HB, (h + 1) * MLA_HB)
        q_out[:, sl] = ((q[:, sl] * cos + qr[:, sl] * sin) * scale).astype(BF16)
        k_out[:, sl] = (kn[:, sl] + krot).astype(BF16)
    vlane = lax.broadcasted_iota(jnp.int32, (1, MLA_HEADS * LANES), 1)
    ones_half = ((vlane // LANES) % 2 == 0) == ((vlane % LANES) >= MLA_V)
    v = jnp.dot(ckvn, wv_ref[...], preferred_element_type=F32)
    v_out[...] = jnp.where(ones_half, 1.0, v).astype(BF16)


def mla_prep(zm, qn, kvn, wq, wqr, wk, wv, cos_t, sin_t, S, tm=512):
    T = zm.shape[0]
    nsb = S // tm
    HW = MLA_HEADS * MLA_HB
    full = lambda a: pl.BlockSpec(a.shape, lambda i: (0, 0))
    return pl.pallas_call(
        _mla_prep_kernel,
        grid=(T // tm,),
        in_specs=[pl.BlockSpec((tm, zm.shape[1]), lambda i: (i, 0)),
                  full(qn), full(kvn), full(wq), full(wqr), full(wk), full(wv),
                  pl.BlockSpec((tm, LANES), lambda i: (i % nsb, 0)),
                  pl.BlockSpec((tm, LANES), lambda i: (i % nsb, 0))],
        out_specs=[pl.BlockSpec((tm, HW), lambda i: (i, 0)),
                   pl.BlockSpec((tm, HW), lambda i: (i, 0)),
                   pl.BlockSpec((tm, HW), lambda i: (i, 0))],
        out_shape=[jax.ShapeDtypeStruct((T, HW), BF16),
                   jax.ShapeDtypeStruct((T, HW), BF16),
                   jax.ShapeDtypeStruct((T, HW), BF16)],
        compiler_params=_cparams(("parallel",)),
        name="mla_prep",
    )(zm, qn, kvn, wq, wqr, wk, wv, cos_t, sin_t)


def _mla_flash_kernel(q_ref, k_ref, v_ref, o_ref, m_scr, acc_scr, *, tq, tk):
    i = pl.program_id(2)
    per_q = tq // tk
    m_scr[...] = jnp.full(m_scr.shape, NEG, F32)
    acc_scr[...] = jnp.zeros(acc_scr.shape, F32)
    heads = range(MLA_FLASH_HEADS)
    lane_tiles = range(tk // LANES)

    def update(j, diag):
        rows = pl.ds(pl.multiple_of(j * tk, tk), tk)
        s = [lax.dot_general(q_ref[:, h * MLA_HB:(h + 1) * MLA_HB], k_ref[rows, h * MLA_HB:(h + 1) * MLA_HB],
                             (((1,), (1,)), ((), ())), preferred_element_type=F32) for h in heads]
        if diag is not None:
            qpos = lax.broadcasted_iota(jnp.int32, (tq, tk), 0)
            kpos = lax.broadcasted_iota(jnp.int32, (tq, tk), 1) + diag * tk
            mask = kpos <= qpos
            s = [jnp.where(mask, x, NEG) for x in s]
        new_m = []
        pv = []
        for h in heads:
            tiles = [s[h][:, c * LANES:(c + 1) * LANES] for c in lane_tiles]
            tile_max = functools.reduce(jnp.maximum, tiles)
            m_new = jnp.maximum(m_scr[h], jnp.max(tile_max, axis=1, keepdims=True))
            p = jnp.concatenate([jnp.exp2(t - m_new) for t in tiles], axis=1).astype(BF16)
            pv.append(jnp.dot(p, v_ref[rows, h * LANES:(h + 1) * LANES], preferred_element_type=F32))
            new_m.append(m_new)
        for h in heads:
            acc_scr[h] = jnp.exp2(m_scr[h] - new_m[h]) * acc_scr[h] + pv[h]
            m_scr[h] = new_m[h]

    def body(j, carry):
        update(j, None)
        return carry

    lax.fori_loop(0, i * per_q, body, jnp.int32(0))
    for d in range(per_q):
        update(i * per_q + d, d)
    lane = lax.broadcasted_iota(jnp.int32, (tq, LANES), 1)
    for pr in range(MLA_FLASH_HEADS // 2):
        o_ref[:, pr * LANES:(pr + 1) * LANES] = _normalise_pair(
            acc_scr[2 * pr], acc_scr[2 * pr + 1], lane < MLA_V, MLA_V).astype(o_ref.dtype)


def mla_flash(q, k, v, B, S, tq=512, tk=512):
    T = q.shape[0]
    tk = min(tk, tq)
    nq = S // tq
    nh = MLA_FLASH_HEADS
    return pl.pallas_call(
        functools.partial(_mla_flash_kernel, tq=tq, tk=tk),
        grid=(B, MLA_HEADS // nh, nq),
        in_specs=[pl.BlockSpec((tq, nh * MLA_HB), lambda b, p, i: (b * nq + i, p)),
                  pl.BlockSpec((S, nh * MLA_HB), lambda b, p, i: (b, p), pipeline_mode=pl.Buffered(1)),
                  pl.BlockSpec((S, nh * LANES), lambda b, p, i: (b, p), pipeline_mode=pl.Buffered(1))],
        out_specs=pl.BlockSpec((tq, nh * MLA_V), lambda b, p, i: (b * nq + i, p)),
        out_shape=jax.ShapeDtypeStruct((T, MLA_HEADS * MLA_V), BF16),
        scratch_shapes=[pltpu.VMEM((nh, tq, LANES), F32),
                        pltpu.VMEM((nh, tq, LANES), F32)],
        compiler_params=_cparams(("parallel", "parallel", "arbitrary")),
        name="mla_flash",
    )(q, k, v)


GDN_QKV = 3 * GDN_HEADS * GDN_DK
BETA_LANE = 96
DECAY_LANE = 100
HALO = 8


def _gdn_prep_kernel(z_ref, halo_ref, m1_ref, cw_ref, alog_ref, dt_ref, qkv_out, gb_out, gbt_out,
                     *, tm, tiles_per_seq):
    i = pl.program_id(0)
    x = z_ref[...]
    halo = halo_ref[...]
    halo = jnp.where(i % tiles_per_seq == 0, jnp.zeros_like(halo), halo)
    xe = jnp.concatenate([halo, x], axis=0)
    cw = cw_ref[...]
    acc = x * cw[GDN_CONV - 1:GDN_CONV, :]
    for d in range(1, GDN_CONV):
        acc = acc + xe[HALO - d:HALO - d + tm, :] * cw[GDN_CONV - 1 - d:GDN_CONV - d, :]
    y = _silu(acc)
    nq = GDN_HEADS * GDN_DK
    for h in range(GDN_HEADS):
        sl = slice(h * GDN_DK, (h + 1) * GDN_DK)
        qh = y[:, sl]
        qkv_out[:, sl] = qh * lax.rsqrt(jnp.sum(qh * qh, axis=-1, keepdims=True) + EPS) * (GDN_DK ** -0.5)
        sl2 = slice(nq + h * GDN_DK, nq + (h + 1) * GDN_DK)
        kh = y[:, sl2]
        qkv_out[:, sl2] = kh * lax.rsqrt(jnp.sum(kh * kh, axis=-1, keepdims=True) + EPS)
    qkv_out[:, 2 * nq:] = y[:, 2 * nq:]
    m1 = m1_ref[...]
    lane = lax.broadcasted_iota(jnp.int32, m1.shape, 1)
    beta = _sigmoid(m1)
    xa = m1 + dt_ref[...]
    softplus = jnp.maximum(xa, 0.0) + jnp.log(1.0 + jnp.exp(-jnp.abs(xa)))
    decay = -jnp.exp(alog_ref[...]) * softplus
    ri = lax.broadcasted_iota(jnp.int32, (tm, tm), 0)
    ci = lax.broadcasted_iota(jnp.int32, (tm, tm), 1)
    ltri = jnp.where((ri >= ci) & (ri // GDN_CHUNK == ci // GDN_CHUNK), 1.0, 0.0).astype(BF16)
    d_hi = decay.astype(BF16)
    rem = decay - d_hi.astype(F32)
    d_mid = rem.astype(BF16)
    d_lo = (rem - d_mid.astype(F32)).astype(BF16)
    gcum = (jnp.dot(ltri, d_hi, preferred_element_type=F32) + jnp.dot(ltri, d_mid, preferred_element_type=F32)
            + jnp.dot(ltri, d_lo, preferred_element_type=F32))
    gb = jnp.where(lane < DECAY_LANE, beta, gcum)
    gb_out[...] = gb
    r = lax.broadcasted_iota(jnp.int32, (8, LANES), 0)
    c = lax.broadcasted_iota(jnp.int32, (8, LANES), 1)
    pick = (c == r + BETA_LANE).astype(F32)
    gbt_out[...] = lax.dot_general(pick, gb, (((1,), (1,)), ((), ())),
                                   preferred_element_type=F32, precision=HIGHEST)


def gdn_prep(zg, zm, conv_w, alog_v, dt_v, S, tm=512):
    T = zg.shape[0]
    tiles_per_seq = S // tm
    hb = tm // HALO
    m1_blk = 640 // LANES
    return pl.pallas_call(
        functools.partial(_gdn_prep_kernel, tm=tm, tiles_per_seq=tiles_per_seq),
        grid=(T // tm,),
        in_specs=[pl.BlockSpec((tm, GDN_QKV), lambda i: (i, 0)),
                  pl.BlockSpec((HALO, GDN_QKV), lambda i: (jnp.maximum(i * hb - 1, 0), 0)),
                  pl.BlockSpec((tm, LANES), lambda i: (i, m1_blk)),
                  pl.BlockSpec((GDN_CONV, GDN_QKV), lambda i: (0, 0)),
                  pl.BlockSpec((1, LANES), lambda i: (0, 0)),
                  pl.BlockSpec((1, LANES), lambda i: (0, 0))],
        out_specs=[pl.BlockSpec((tm, GDN_QKV), lambda i: (i, 0)),
                   pl.BlockSpec((tm, LANES), lambda i: (i, 0)),
                   pl.BlockSpec((8, tm), lambda i: (0, i))],
        out_shape=[jax.ShapeDtypeStruct((T, GDN_QKV), F32),
                   jax.ShapeDtypeStruct((T, LANES), F32),
                   jax.ShapeDtypeStruct((8, T), F32)],
        compiler_params=_cparams(("parallel",)),
        name="gdn_prep",
    )(zg, zg, zm, conv_w, alog_v, dt_v)


def _dot3(a, b):
    a_hi = a.astype(BF16)
    a_lo = (a - a_hi.astype(F32)).astype(BF16)
    b_hi = b.astype(BF16)
    b_lo = (b - b_hi.astype(F32)).astype(BF16)
    return (jnp.dot(a_hi, b_hi, preferred_element_type=F32) + jnp.dot(a_hi, b_lo, preferred_element_type=F32)
            + jnp.dot(a_lo, b_hi, preferred_element_type=F32))


def _tri_inverse_all(a_list, eye, diag_blocks):
    ad = [jnp.where(diag_blocks, a, 0.0) for a in a_list]
    ao = [a - d for a, d in zip(a_list, ad)]
    a2 = [_dot(d, d) for d in ad]
    a4 = [_dot(x, x) for x in a2]
    a8 = [_dot(x, x) for x in a4]
    t = [_dot(eye - d, eye + x) for d, x in zip(ad, a2)]
    t = [_dot(y, eye + x) for y, x in zip(t, a4)]
    dinv = [_dot(y, eye + x) for y, x in zip(t, a8)]
    n = [_dot(d, o) for d, o in zip(dinv, ao)]
    n2 = [_dot(x, x) for x in n]
    t = [_dot(eye - x, eye + y) for x, y in zip(n, n2)]
    x0 = [_dot(y, d) for y, d in zip(t, dinv)]
    res = [eye - x - _dot3(a, x) for a, x in zip(a_list, x0)]
    return [x + _dot(x, r) for x, r in zip(x0, res)]


def _gdn_chunk_kernel(qkv_ref, gb_ref, gbt_ref, zg_ref, norm_ref, o_ref, state_scr, *, lb):
    C = GDN_CHUNK
    DK = GDN_DK
    H = GDN_HEADS
    nq = H * DK
    n_chunks = lb // C

    @pl.when(pl.program_id(1) == 0)
    def _():
        state_scr[...] = jnp.zeros(state_scr.shape, F32)

    ii = lax.broadcasted_iota(jnp.int32, (C, C), 0)
    jj = lax.broadcasted_iota(jnp.int32, (C, C), 1)
    lower = ii >= jj
    strict = ii > jj
    eye = (ii == jj).astype(F32)
    diag_blocks = (ii // 16) == (jj // 16)
    norm_w = norm_ref[...]

    items = [(c, h) for c in range(n_chunks) for h in range(H)]
    rows = lambda c: slice(c * C, (c + 1) * C)
    q = [qkv_ref[rows(c), h * DK:(h + 1) * DK] for c, h in items]
    k = [qkv_ref[rows(c), nq + h * DK:nq + (h + 1) * DK] for c, h in items]
    v = [qkv_ref[rows(c), 2 * nq + h * DK:2 * nq + (h + 1) * DK] for c, h in items]
    beta = [jnp.broadcast_to(gb_ref[rows(c), BETA_LANE + h:BETA_LANE + h + 1], (C, DK)) for c, h in items]
    gc = [jnp.broadcast_to(gb_ref[rows(c), DECAY_LANE + h:DECAY_LANE + h + 1], (C, DK)) for c, h in items]
    gr = [jnp.broadcast_to(gbt_ref[4 + h:5 + h, rows(c)], (C, C)) for c, h in items]
    decay = [jnp.exp(jnp.where(lower, x[:, :C] - y, NEG)) for x, y in zip(gc, gr)]
    eg = [jnp.exp(x) for x in gc]
    kb = [x * b for x, b in zip(k, beta)]
    a = [jnp.where(strict, _dot_nt(x, y) * d, 0.0) for x, y, d in zip(kb, k, decay)]
    t_inv = _tri_inverse_all(a, eye, diag_blocks)
    uw = [_dot(t, jnp.concatenate([x * b, y * e], axis=1))
          for t, x, b, y, e in zip(t_inv, v, beta, kb, eg)]
    intra = [_dot_nt(x, y) * d for x, y, d in zip(q, k, decay)]
    g_last = [x[C - 1:C, :] for x in gc]
    k_dec = [x * jnp.exp(gl - g) for x, gl, g in zip(k, g_last, gc)]
    wq = [jnp.concatenate([x[:, DK:], y * e], axis=0) for x, y, e in zip(uw, q, eg)]

    states = [state_scr[h] for h in range(H)]
    for c in range(n_chunks):
        idx = [c * H + h for h in range(H)]
        ws = [_dot(wq[i], states[h]) for h, i in enumerate(idx)]
        v_new = [uw[i][:, :DK] - y[:C] for i, y in zip(idx, ws)]
        o = [y[C:] + _dot(intra[i], vn) for i, y, vn in zip(idx, ws, v_new)]
        states = [s * jnp.exp(g_last[i]) + _dot_tn(k_dec[i], vn) for s, i, vn in zip(states, idx, v_new)]
        for h in range(H):
            on = o[h] * lax.rsqrt(jnp.mean(o[h] * o[h], axis=-1, keepdims=True) + EPS) * norm_w
            o_ref[rows(c), h * DK:(h + 1) * DK] = (on * _silu(zg_ref[rows(c), h * DK:(h + 1) * DK])
                                                   ).astype(o_ref.dtype)
    for h in range(H):
        state_scr[h] = states[h]


def gdn_chunk(qkv, gb, gbt, zg, norm_w, B, S, lb=512):
    T = qkv.shape[0]
    nsb = S // lb
    VW = GDN_HEADS * GDN_DV
    zg_blk = GDN_QKV // VW
    return pl.pallas_call(
        functools.partial(_gdn_chunk_kernel, lb=lb),
        grid=(B, nsb),
        in_specs=[pl.BlockSpec((lb, GDN_QKV), lambda b, s: (b * nsb + s, 0)),
                  pl.BlockSpec((lb, LANES), lambda b, s: (b * nsb + s, 0)),
                  pl.BlockSpec((8, lb), lambda b, s: (0, b * nsb + s)),
                  pl.BlockSpec((lb, VW), lambda b, s: (b * nsb + s, zg_blk)),
                  pl.BlockSpec((1, GDN_DV), lambda b, s: (0, 0))],
        out_specs=pl.BlockSpec((lb, VW), lambda b, s: (b * nsb + s, 0)),
        out_shape=jax.ShapeDtypeStruct((T, VW), BF16),
        scratch_shapes=[pltpu.VMEM((GDN_HEADS, GDN_DK, GDN_DV), F32)],
        compiler_params=_cparams(("parallel", "arbitrary")),
        name="gdn_chunk",
    )(qkv, gb, gbt, zg, norm_w)


def _compress_kernel(xl_ref, xh_ref, pea_ref, peb_ref, w1a_ref, w1b_ref, w2_ref, o_ref, p1_scr, p2_scr,
                     *, per_step):
    kc = pl.program_id(1)
    nr = p1_scr.shape[0]
    width = xl_ref.shape[1] + xh_ref.shape[1]

    @pl.when(kc == 0)
    def _():
        p1_scr[...] = jnp.zeros(p1_scr.shape, F32)
        p2_scr[...] = jnp.zeros(p2_scr.shape, F32)

    for t in range(per_step):
        rows = pl.ds(kc * per_step + t, nr, stride=CMP_STRIDE)
        x = jnp.concatenate([xl_ref[rows, :], xh_ref[rows, :]], axis=1)
        cols = slice(t * width, (t + 1) * width)
        p1_scr[...] += _dot(x + pea_ref[:, cols], w1a_ref[cols, :])
        p2_scr[...] += _dot(x + peb_ref[:, cols], w1b_ref[cols, :])

    @pl.when(kc == pl.num_programs(1) - 1)
    def _():
        p2 = p2_scr[...]
        rows = p2.shape[0]
        hid = p1_scr[...] + pltpu.roll(p2, rows - 1, 0)
        c0 = math.sqrt(2.0 / math.pi)
        act = 0.5 * hid * (1.0 + jnp.tanh(c0 * (hid + 0.044715 * (hid * hid * hid))))
        o_ref[0] = _dot(act, w2_ref[...])


def nsa_compress(x, pe_a, pe_b, w1a, w1b, w2, B, S, per_step=2):
    nr = S // CMP_STRIDE
    width = x.shape[1]
    kchunk = per_step * width
    KW = CMP_STRIDE * width
    HW = NSA_GROUPS * CMP_HIDDEN
    OW = NSA_GROUPS * NSA_DIM
    return pl.pallas_call(
        functools.partial(_compress_kernel, per_step=per_step),
        grid=(B, KW // kchunk),
        in_specs=[pl.BlockSpec((S, LANES), lambda b, k: (b, 0)),
                  pl.BlockSpec((S, LANES), lambda b, k: (b, 1)),
                  pl.BlockSpec((1, kchunk), lambda b, k: (0, k)),
                  pl.BlockSpec((1, kchunk), lambda b, k: (0, k)),
                  pl.BlockSpec((kchunk, HW), lambda b, k: (k, 0)),
                  pl.BlockSpec((kchunk, HW), lambda b, k: (k, 0)),
                  pl.BlockSpec((HW, OW), lambda b, k: (0, 0))],
        out_specs=pl.BlockSpec((1, nr, OW), lambda b, k: (b, 0, 0)),
        out_shape=jax.ShapeDtypeStruct((B, nr, OW), F32),
        scratch_shapes=[pltpu.VMEM((nr, HW), F32), pltpu.VMEM((nr, HW), F32)],
        compiler_params=_cparams(("parallel", "arbitrary")),
        name="nsa_compress",
    )(x, x, pe_a, pe_b, w1a, w1b, w2)


def _slope(h):
    return float(2.0 ** (-8.0 * (h + 1) / NSA_HEADS))


SEL_TK = 2 * SEL_BLOCK
SEL_NT = 4
SEL_TQ = 256
TOPN_ROW_STEP = 16


def _gate_expand(branch):
    e = np.zeros((LANES, NSA_HEADS * NSA_DIM), np.float32)
    for h in range(NSA_HEADS):
        e[3 * h + branch, h * NSA_DIM:(h + 1) * NSA_DIM] = 1.0
    return jnp.asarray(e).astype(BF16)


def _gate_matrix(zg, expand):
    g = _sigmoid(zg)
    g_hi = g.astype(BF16)
    rem = g - g_hi.astype(F32)
    g_mid = rem.astype(BF16)
    g_lo = (rem - g_mid.astype(F32)).astype(BF16)
    dot = lambda a: jnp.dot(a, expand, preferred_element_type=F32)
    return (dot(g_hi) + dot(g_mid)) + dot(g_lo)


def _value_blocks(vg):
    ones = jnp.ones_like(vg)
    return jnp.concatenate([vg, ones], axis=1), jnp.concatenate([ones, vg], axis=1)


def _nsa_cmp_kernel(q_ref, kc_ref, vc_ref, zg_ref, ge_ref, ov_ref, o_ref, sel_ref, flag_ref, work_scr, sel_scr,
                    *, tq, n_sel):
    i = pl.program_id(1)
    n_tiles = n_sel * SEL_BLOCK // SEL_TK
    blk_tile = (lax.broadcasted_iota(jnp.int32, (n_sel, n_tiles), 0) * SEL_BLOCK // SEL_TK
                == lax.broadcasted_iota(jnp.int32, (n_sel, n_tiles), 1))
    to_tile = jnp.where(blk_tile, 1.0, 0.0).astype(BF16)
    ncmp = kc_ref.shape[1]
    D = NSA_DIM
    qpos = i * tq + lax.broadcasted_iota(jnp.int32, (tq, 1), 0)
    nidx = lax.broadcasted_iota(jnp.int32, (1, ncmp), 1)
    valid = (nidx * CMP_STRIDE + (CMP_BLOCK - 1)) <= qpos
    any_valid = (qpos >= CMP_BLOCK - 1).astype(F32)
    centre_rel = (nidx * CMP_STRIDE - i * tq).astype(F32) + 0.5 * (CMP_BLOCK - 1)
    gate_mat = _gate_matrix(zg_ref[...], ge_ref[...])
    kc = kc_ref[0].astype(BF16)
    vc = vc_ref[0].astype(BF16)
    ov_t = ov_ref[...]
    jf = lax.broadcasted_iota(jnp.int32, (n_sel, 1), 0).astype(F32)
    qblk = ((i * tq + lax.broadcasted_iota(jnp.int32, (1, tq), 1)) // SEL_BLOCK).astype(F32)
    forced = (jf == 0.0) | (jf == qblk) | (jf == qblk - 1.0)
    causal_blk = jf <= qblk
    heads = range(NSA_HPG)
    groups = range(NSA_GROUPS)
    lane_half = lax.broadcasted_iota(jnp.int32, (tq, LANES), 1) < D
    assert ncmp % LANES == 0

    def scores(width):
        for g in groups:
            kg = kc[:width, g * D:(g + 1) * D]
            vg = vc[:width, g * D:(g + 1) * D]
            hs = [g * NSA_HPG + r for r in heads]
            s = [_dot_nt(q_ref[:, h * D:(h + 1) * D], kg) + (_slope(h) * LOG2E) * centre_rel[:, :width] for h in hs]
            s = [jnp.where(valid[:, :width], x, NEG) for x in s]
            v_even, v_odd = _value_blocks(vg)
            p, o = [], []
            for r, h in enumerate(hs):
                tiles = [s[r][:, c * LANES:(c + 1) * LANES] for c in range(width // LANES)]
                tile_max = functools.reduce(jnp.maximum, tiles)
                m = jnp.maximum(jnp.full((tq, LANES), NEG, F32), jnp.max(tile_max, axis=1, keepdims=True))
                e_tiles = [jnp.exp2(t - m) for t in tiles]
                pv = jnp.dot(jnp.concatenate(e_tiles, axis=1).astype(BF16), v_odd if r % 2 else v_even,
                             preferred_element_type=F32)
                rolled = pltpu.roll(pv, D, 1)
                row_sum = jnp.where(lane_half, pv, rolled) if r % 2 else jnp.where(lane_half, rolled, pv)
                norm = any_valid / row_sum
                o.append(pv * norm)
                p.append([t * norm for t in e_tiles])
            for t in range(NSA_HPG // 2):
                blk = slice((g * NSA_HPG // 2 + t) * LANES, (g * NSA_HPG // 2 + t + 1) * LANES)
                o_ref[:, blk] = gate_mat[:, blk] * jnp.where(lane_half, o[2 * t], o[2 * t + 1])
            psum = jnp.concatenate([(a + b) + (c + d) for a, b, c, d in zip(*p)], axis=1)
            p_hi = psum.astype(BF16)
            rem = psum - p_hi.astype(F32)
            p_mid = rem.astype(BF16)
            p_lo = (rem - p_mid.astype(F32)).astype(BF16)
            ov_w = ov_t[:, :width]
            imp_t = (_dot_nt(ov_w, p_hi) + _dot_nt(ov_w, p_mid)) + _dot_nt(ov_w, p_lo)
            work_scr[g] = jnp.where(forced, BIG, jnp.where(causal_blk, imp_t, NEG))

    n_valid = jnp.maximum((i * tq + tq - CMP_BLOCK) // CMP_STRIDE + 1, 1)
    tiles_needed = jnp.minimum((n_valid + LANES - 1) // LANES, ncmp // LANES)
    for nt in range(1, ncmp // LANES + 1):
        pl.when(tiles_needed == nt)(functools.partial(scores, nt * LANES))
    def top_n(rows):
        work = [work_scr[g][:rows] for g in groups]
        idx = jf[:rows]
        selm = [jnp.zeros((rows, tq), F32) for _ in groups]
        for _ in range(min(SEL_TOPN, n_sel)):
            mx = [jnp.max(w, axis=0, keepdims=True) for w in work]
            first = [jnp.min(jnp.where(w == m, idx, float(n_sel)), axis=0, keepdims=True) for w, m in zip(work, mx)]
            pick = [idx == f for f in first]
            selm = [jnp.where(pk, 1.0, sm) for pk, sm in zip(pick, selm)]
            work = [jnp.where(pk, -jnp.inf, w) for pk, w in zip(pick, work)]
        for g in groups:
            sel_scr[g] = jnp.zeros((n_sel, tq), F32)
            sel_scr[g, :rows, :] = selm[g]

    row_step = min(TOPN_ROW_STEP, n_sel)
    causal_blocks = (i * tq + tq - 1) // SEL_BLOCK + 1
    rows_needed = jnp.minimum((jnp.maximum(causal_blocks, SEL_TOPN) + row_step - 1) // row_step * row_step, n_sel)
    for rows in range(row_step, n_sel + 1, row_step):
        pl.when(rows_needed == rows)(functools.partial(top_n, rows))
    for g in groups:
        selb = sel_scr[g].T.astype(BF16)
        sel_ref[:, g * n_sel:(g + 1) * n_sel] = selb
        tile_hits = jnp.dot(selb, to_tile, preferred_element_type=F32)
        for part in range(tq // SEL_TQ):
            hits = jnp.max(tile_hits[part * SEL_TQ:(part + 1) * SEL_TQ], axis=0, keepdims=True)
            flag_ref[part, g:g + 1, :] = (hits > 0.5).astype(jnp.int32)


def nsa_cmp(q, k_cmp, v_cmp, zg, overlap, B, S, tq=SEL_TQ):
    T = q.shape[0]
    tq = min(tq, S)
    nq = S // tq
    parts = tq // SEL_TQ
    n_sel = S // SEL_BLOCK
    ncmp = k_cmp.shape[1]
    QW = NSA_HEADS * NSA_DIM
    KW = NSA_GROUPS * NSA_DIM
    return pl.pallas_call(
        functools.partial(_nsa_cmp_kernel, tq=tq, n_sel=n_sel),
        grid=(B, nq),
        in_specs=[pl.BlockSpec((tq, QW), lambda b, i: (b * nq + i, 0)),
                  pl.BlockSpec((1, ncmp, KW), lambda b, i: (b, 0, 0)),
                  pl.BlockSpec((1, ncmp, KW), lambda b, i: (b, 0, 0)),
                  pl.BlockSpec((tq, LANES), lambda b, i: (b * nq + i, 0)),
                  pl.BlockSpec((LANES, QW), lambda b, i: (0, 0)),
                  pl.BlockSpec((n_sel, ncmp), lambda b, i: (0, 0))],
        out_specs=[pl.BlockSpec((tq, QW), lambda b, i: (b * nq + i, 0)),
                   pl.BlockSpec((tq, NSA_GROUPS * n_sel), lambda b, i: (b * nq + i, 0)),
                   pl.BlockSpec((parts, NSA_GROUPS, S // SEL_TK), lambda b, i: (b * nq + i, 0, 0))],
        out_shape=[jax.ShapeDtypeStruct((T, QW), F32),
                   jax.ShapeDtypeStruct((T, NSA_GROUPS * n_sel), BF16),
                   jax.ShapeDtypeStruct((B * nq * parts, NSA_GROUPS, S // SEL_TK), jnp.int32)],
        scratch_shapes=[pltpu.VMEM((NSA_GROUPS, n_sel, tq), F32), pltpu.VMEM((NSA_GROUPS, n_sel, tq), F32)],
        compiler_params=_cparams(("parallel", "parallel")),
        name="nsa_cmp",
    )(q, k_cmp, v_cmp, zg, _gate_expand(0), overlap)


def _nsa_sel_kernel(flags_ref, q_ref, k_ref, v_ref, sel_ref, zg_ref, ge_ref, prev_ref, o_ref,
                    list_smem, m_scr, acc_scr, *, tq, n_sel, nq):
    b = pl.program_id(0)
    i = pl.program_id(1)
    D = NSA_DIM
    n_tiles = n_sel * SEL_BLOCK // SEL_TK
    n_causal = (i * tq + tq - 1) // SEL_TK + 1
    ks_w = SEL_NT * SEL_TK
    qpos = i * tq + lax.broadcasted_iota(jnp.int32, (tq, 1), 0)
    lane_t = lax.broadcasted_iota(jnp.int32, (1, SEL_TK), 1)
    blk_iota = lax.broadcasted_iota(jnp.int32, (n_sel, ks_w), 0)
    gate_mat = _gate_matrix(zg_ref[...], ge_ref[...])
    lane_half = lax.broadcasted_iota(jnp.int32, (tq, LANES), 1) < D

    for g in range(NSA_GROUPS):
        base = ((b * nq + i) * NSA_GROUPS + g) * n_tiles

        def scan(j, n, base=base):
            list_smem[n] = j
            return n + (flags_ref[base + j] != 0).astype(jnp.int32)

        count = lax.fori_loop(0, n_causal, scan, jnp.int32(0))

        for r in range(NSA_HPG):
            m_scr[r] = jnp.full(m_scr.shape[1:], NEG, F32)
            acc_scr[r] = jnp.zeros(acc_scr.shape[1:], F32)
        qs = [q_ref[:, (g * NSA_HPG + r) * D:(g * NSA_HPG + r + 1) * D] for r in range(NSA_HPG)]
        sel_g = sel_ref[:, g * n_sel:(g + 1) * n_sel]

        def step(st, carry, g=g, count=count, qs=qs, sel_g=sel_g):
            k_parts, v_parts, kpos_parts, kblk_parts = [], [], [], []
            for s in range(SEL_NT):
                idx = st * SEL_NT + s
                j = list_smem[jnp.minimum(idx, count - 1)]
                start = pl.multiple_of(j * SEL_TK, SEL_TK)
                k_parts.append(k_ref[pl.ds(start, SEL_TK), g * D:(g + 1) * D])
                v_parts.append(v_ref[pl.ds(start, SEL_TK), g * D:(g + 1) * D])
                tid = jnp.where(idx < count, j, -1)
                kpos_parts.append(tid * SEL_TK + lane_t)
                kblk_parts.append(tid * (SEL_TK // SEL_BLOCK) + lane_t // SEL_BLOCK)
            k = jnp.concatenate(k_parts, axis=0)
            v = jnp.concatenate(v_parts, axis=0)
            kpos = jnp.concatenate(kpos_parts, axis=1)
            kblk = jnp.concatenate(kblk_parts, axis=1)
            expand = jnp.where(blk_iota == kblk, 1.0, 0.0).astype(BF16)
            picked = jnp.dot(sel_g, expand, preferred_element_type=F32)
            allowed = jnp.where(kpos <= qpos, picked, 0.0) > 0.5
            krel = (kpos - i * tq).astype(F32)
            hs = range(NSA_HPG)
            s_ = [_dot_nt(qs[r], k) + (_slope(g * NSA_HPG + r) * LOG2E) * krel for r in hs]
            s_ = [jnp.where(allowed, x, NEG) for x in s_]
            v_even, v_odd = _value_blocks(v)
            new_m, pv = [], []
            for r in hs:
                tiles = [s_[r][:, c * LANES:(c + 1) * LANES] for c in range(ks_w // LANES)]
                tile_max = functools.reduce(jnp.maximum, tiles)
                m_new = jnp.maximum(m_scr[r], jnp.max(tile_max, axis=1, keepdims=True))
                p = jnp.concatenate([jnp.exp2(t - m_new) for t in tiles], axis=1).astype(BF16)
                pv.append(jnp.dot(p, v_odd if r % 2 else v_even, preferred_element_type=F32))
                new_m.append(m_new)
            for r in hs:
                acc_scr[r] = jnp.exp2(m_scr[r] - new_m[r]) * acc_scr[r] + pv[r]
                m_scr[r] = new_m[r]
            return carry

        lax.fori_loop(0, (count + SEL_NT - 1) // SEL_NT, step, jnp.int32(0))

        for t in range(NSA_HPG // 2):
            blk = slice((g * NSA_HPG // 2 + t) * LANES, (g * NSA_HPG // 2 + t + 1) * LANES)
            o_ref[:, blk] = prev_ref[:, blk] + gate_mat[:, blk] * _normalise_pair(
                acc_scr[2 * t], acc_scr[2 * t + 1], lane_half, D)


def nsa_sel(q, ks, vs, sel, flags, zg, prev, B, S, tq=SEL_TQ):
    T = q.shape[0]
    nq = S // tq
    n_sel = S // SEL_BLOCK
    QW = NSA_HEADS * NSA_DIM
    KW = NSA_GROUPS * NSA_DIM
    qmap = lambda b, i, fl: (b * nq + i, 0)
    kmap = lambda b, i, fl: (b, 0)
    grid_spec = pltpu.PrefetchScalarGridSpec(
        num_scalar_prefetch=1,
        grid=(B, nq),
        in_specs=[pl.BlockSpec((tq, QW), qmap),
                  pl.BlockSpec((S, KW), kmap),
                  pl.BlockSpec((S, KW), kmap),
                  pl.BlockSpec((tq, NSA_GROUPS * n_sel), qmap),
                  pl.BlockSpec((tq, LANES), qmap),
                  pl.BlockSpec((LANES, QW), lambda b, i, fl: (0, 0)),
                  pl.BlockSpec((tq, QW), qmap)],
        out_specs=pl.BlockSpec((tq, QW), qmap),
        scratch_shapes=[pltpu.SMEM((S // SEL_TK,), jnp.int32),
                        pltpu.VMEM((NSA_HPG, tq, LANES), F32),
                        pltpu.VMEM((NSA_HPG, tq, LANES), F32)],
    )
    return pl.pallas_call(
        functools.partial(_nsa_sel_kernel, tq=tq, n_sel=n_sel, nq=nq),
        grid_spec=grid_spec,
        out_shape=jax.ShapeDtypeStruct((T, QW), F32),
        compiler_params=_cparams(("parallel", "arbitrary")),
        name="nsa_sel",
    )(flags.reshape(-1), q, ks, vs, sel, zg, _gate_expand(1), prev)


def _nsa_win_kernel(q_ref, k0_ref, k1_ref, k2_ref, v0_ref, v1_ref, v2_ref, zg_ref, ge_ref, prev_ref, o_ref,
                    *, tq):
    i = pl.program_id(1)
    D = NSA_DIM
    nback = WINDOW // tq
    tkw = (nback + 1) * tq
    k = jnp.concatenate([k0_ref[...], k1_ref[...], k2_ref[...]], axis=0).astype(BF16)
    v = jnp.concatenate([v0_ref[...], v1_ref[...], v2_ref[...]], axis=0).astype(BF16)
    qrel = lax.broadcasted_iota(jnp.int32, (tq, tkw), 0)
    krel = lax.broadcasted_iota(jnp.int32, (tq, tkw), 1) - nback * tq
    dw = qrel - krel
    wvalid = jnp.where(dw >= 0, jnp.where(dw < WINDOW, krel + i * tq, -1), -1) >= 0
    krow = (lax.broadcasted_iota(jnp.int32, (1, tkw), 1) - nback * tq).astype(F32)
    gate_mat = _gate_matrix(zg_ref[...], ge_ref[...])
    lane_half = lax.broadcasted_iota(jnp.int32, (tq, LANES), 1) < D
    heads = range(NSA_HPG)
    for g in range(NSA_GROUPS):
        kg = k[:, g * D:(g + 1) * D]
        vg = v[:, g * D:(g + 1) * D]
        hs = [g * NSA_HPG + r for r in heads]
        s = [_dot_nt(q_ref[:, h * D:(h + 1) * D], kg) + (_slope(h) * LOG2E) * krow for h in hs]
        s = [jnp.where(wvalid, x, NEG) for x in s]
        v_even, v_odd = _value_blocks(vg)
        o = []
        for r, x in enumerate(s):
            tiles = [x[:, c * LANES:(c + 1) * LANES] for c in range(tkw // LANES)]
            tile_max = functools.reduce(jnp.maximum, tiles)
            m = jnp.maximum(jnp.full((tq, LANES), NEG, F32), jnp.max(tile_max, axis=1, keepdims=True))
            e = jnp.concatenate([jnp.exp2(t - m) for t in tiles], axis=1).astype(BF16)
            pv = jnp.dot(e, v_odd if r % 2 else v_even, preferred_element_type=F32)
            o.append(pv * (1.0 / pltpu.roll(pv, D, 1)))
        for t in range(NSA_HPG // 2):
            blk = slice((g * NSA_HPG // 2 + t) * LANES, (g * NSA_HPG // 2 + t + 1) * LANES)
            o_ref[:, blk] = (prev_ref[:, blk] + gate_mat[:, blk] * jnp.where(lane_half, o[2 * t], o[2 * t + 1])
                             ).astype(o_ref.dtype)


def nsa_win(q, kw, vw, zg, prev, B, S, tq=256):
    T = q.shape[0]
    assert WINDOW % tq == 0 and WINDOW // tq == 2
    nq = S // tq
    QW = NSA_HEADS * NSA_DIM
    KW = NSA_GROUPS * NSA_DIM
    qmap = lambda b, i: (b * nq + i, 0)
    back = lambda d: (lambda b, i: (b * nq + jnp.maximum(i - d, 0), 0))
    kspecs = [pl.BlockSpec((tq, KW), back(2)), pl.BlockSpec((tq, KW), back(1)), pl.BlockSpec((tq, KW), back(0))]
    return pl.pallas_call(
        functools.partial(_nsa_win_kernel, tq=tq),
        grid=(B, nq),
        in_specs=[pl.BlockSpec((tq, QW), qmap)] + kspecs + kspecs
                 + [pl.BlockSpec((tq, LANES), qmap), pl.BlockSpec((LANES, QW), lambda b, i: (0, 0)),
                    pl.BlockSpec((tq, QW), qmap)],
        out_specs=pl.BlockSpec((tq, QW), qmap),
        out_shape=jax.ShapeDtypeStruct((T, QW), BF16),
        compiler_params=_cparams(("parallel", "parallel")),
        name="nsa_win",
    )(q, kw, kw, kw, vw, vw, vw, zg, _gate_expand(2), prev)


def _rot_half_cols(w):
    half = w.shape[-1] // 2
    return jnp.concatenate([-w[..., half:], w[..., :half]], axis=-1)


def _even_weights(w_in, w_uq, w_ukv):
    D = w_in.shape[0]
    o = 0
    cuts = {}
    for name, n in (("cq", MLA_Q_RANK), ("ckv", MLA_KV_RANK), ("kr", MLA_ROPE), ("zq", 512), ("zk", 512),
                    ("zv", 512), ("zg", 512), ("zb", GDN_HEADS), ("za", GDN_HEADS)):
        cuts[name] = w_in[:, o:o + n]
        o += n
    z = lambda n: jnp.zeros((D, n), F32)
    misc1 = jnp.concatenate([z(MLA_NOPE), cuts["kr"], cuts["zb"], cuts["za"],
                             z(LANES - MLA_NOPE - MLA_ROPE - 2 * GDN_HEADS)], axis=1)
    misc2 = jnp.concatenate([z(MLA_NOPE), _rot_half_cols(cuts["kr"]), z(LANES - MLA_NOPE - MLA_ROPE)], axis=1)
    w_even = jnp.concatenate([cuts["cq"], cuts["ckv"], misc1, misc2,
                              cuts["zq"], cuts["zk"], cuts["zv"], cuts["zg"]], axis=1).astype(BF16)
    qd = MLA_NOPE + MLA_ROPE
    wq3 = w_uq.reshape(MLA_Q_RANK, MLA_HEADS, qd)
    zq = jnp.zeros((MLA_Q_RANK, MLA_HEADS, MLA_HB - qd), F32)
    wq = jnp.concatenate([wq3, zq], axis=2).reshape(MLA_Q_RANK, MLA_HEADS * MLA_HB).astype(BF16)
    wqr = jnp.concatenate([jnp.zeros((MLA_Q_RANK, MLA_HEADS, MLA_NOPE), F32),
                           _rot_half_cols(wq3[:, :, MLA_NOPE:]), zq], axis=2)
    wqr = wqr.reshape(MLA_Q_RANK, MLA_HEADS * MLA_HB).astype(BF16)
    wkv3 = w_ukv.reshape(MLA_KV_RANK, MLA_HEADS, MLA_NOPE + MLA_V)
    wk = jnp.concatenate([wkv3[:, :, :MLA_NOPE], jnp.zeros((MLA_KV_RANK, MLA_HEADS, MLA_HB - MLA_NOPE), F32)],
                         axis=2).reshape(MLA_KV_RANK, MLA_HEADS * MLA_HB).astype(BF16)
    wv4 = wkv3[:, :, MLA_NOPE:].reshape(MLA_KV_RANK, MLA_HEADS // 2, 2, MLA_V)
    zv = jnp.zeros((MLA_KV_RANK, MLA_HEADS // 2, MLA_V), F32)
    wv = jnp.stack([jnp.concatenate([wv4[:, :, 0], zv], axis=2), jnp.concatenate([zv, wv4[:, :, 1]], axis=2)],
                   axis=2).reshape(MLA_KV_RANK, MLA_HEADS * MLA_HB).astype(BF16)
    return w_even, wq, wqr, wk, wv


def _rope_tables(S):
    half = MLA_ROPE // 2
    inv = ROPE_BASE ** (-jnp.arange(half, dtype=F32) / half)
    ang = jnp.arange(S, dtype=F32)[:, None] * inv[None, :]
    cos = jnp.cos(ang)
    sin = jnp.sin(ang)
    pad = jnp.zeros((S, LANES - MLA_NOPE - MLA_ROPE), F32)
    cos_t = jnp.concatenate([jnp.ones((S, MLA_NOPE), F32), cos, cos, pad], axis=1)
    sin_t = jnp.concatenate([jnp.zeros((S, MLA_NOPE), F32), sin, sin, pad], axis=1)
    return cos_t, sin_t


def _lane_vec(vals, start):
    return jnp.zeros((1, LANES), F32).at[0, start:start + vals.shape[0]].set(vals)


def even_mixer_layer(h, B, S, attn_norm, w_in, q_norm, kv_norm, w_uq, w_ukv, conv_w, a_log, dt_bias,
                     gdn_norm, w_out, tables):
    w_even, wq, wqr, wk, wv = _even_weights(w_in, w_uq, w_ukv)
    zm, zg = rms_matmul(h, attn_norm, w_even, (896, 2048))
    cos_t, sin_t = tables
    q, k, v = mla_prep(zm, q_norm.reshape(1, -1), kv_norm.reshape(1, -1), wq, wqr, wk, wv, cos_t, sin_t, S)
    o_mla = mla_flash(q, k, v, B, S, tq=min(512, S))
    qkv, gb, gbt = gdn_prep(zg, zm, conv_w, _lane_vec(a_log, DECAY_LANE), _lane_vec(dt_bias, DECAY_LANE), S)
    o_gdn = gdn_chunk(qkv, gb, gbt, zg, gdn_norm.reshape(1, -1), B, S)
    nm = MLA_HEADS * MLA_V
    return [o_mla, o_gdn], [w_out[:nm].astype(BF16), w_out[nm:].astype(BF16)]


def _compress_weights(pe, w1, w2):
    G, D = NSA_GROUPS, NSA_DIM
    eye = jnp.eye(G, dtype=F32)
    w1r = w1.reshape(CMP_BLOCK, D, CMP_HIDDEN)

    def expand(wpart):
        return jnp.einsum('ldh,gk->lgdkh', wpart, eye).reshape(CMP_STRIDE * G * D, G * CMP_HIDDEN).astype(BF16)

    def pe_vec(p):
        return jnp.broadcast_to(p[:, None, :], (CMP_STRIDE, G, D)).reshape(1, CMP_STRIDE * G * D)

    w2e = jnp.einsum('hd,gk->ghkd', w2, eye).reshape(G * CMP_HIDDEN, G * D).astype(BF16)
    return (pe_vec(pe[:CMP_STRIDE]), pe_vec(pe[CMP_STRIDE:]), expand(w1r[:CMP_STRIDE]),
            expand(w1r[CMP_STRIDE:]), w2e)


def _overlap_matrix(S):
    nr = S // CMP_STRIDE
    n_sel = S // SEL_BLOCK
    n = np.arange(nr)[:, None]
    j = np.arange(n_sel)[None, :]
    start = n * CMP_STRIDE
    ov = (start <= j * SEL_BLOCK + SEL_BLOCK - 1) & (start + CMP_BLOCK - 1 >= j * SEL_BLOCK)
    ov = ov & (n < nr - 1)
    return jnp.asarray(ov.T.astype(np.float32)).astype(BF16)


def odd_mixer_layer(h, B, S, attn_norm, w_in, pe_k, w1_k, w2_k, pe_v, w1_v, w2_v, w_out):
    D = w_in.shape[0]
    n_g = 3 * NSA_HEADS
    qw = NSA_HEADS * NSA_DIM
    w_odd = jnp.concatenate([w_in[:, :qw] * (NSA_DIM ** -0.5 * LOG2E), w_in[:, qw:],
                             jnp.zeros((D, LANES - n_g), F32)], axis=1).astype(BF16)
    kvw = NSA_GROUPS * NSA_DIM
    q, kc, vc, ks, vs, kw, vw, zg = rms_matmul(
        h, attn_norm, w_odd, (qw,) + (kvw,) * 6 + (LANES,),
        dtypes=(BF16, F32, F32, BF16, BF16, BF16, BF16, F32))
    k_cmp = nsa_compress(kc, *_compress_weights(pe_k, w1_k, w2_k), B, S)
    v_cmp = nsa_compress(vc, *_compress_weights(pe_v, w1_v, w2_v), B, S)
    o1, sel, flags = nsa_cmp(q, k_cmp, v_cmp, zg, _overlap_matrix(S), B, S)
    o2 = nsa_sel(q, ks, vs, sel, flags, zg, o1, B, S)
    o3 = nsa_win(q, kw, vw, zg, o2, B, S)
    return [o3], [w_out.astype(BF16)]


def kernel(x, ev_attn_norm, ev_w_in, ev_q_norm, ev_kv_norm, ev_w_uq, ev_w_ukv, ev_conv_w, ev_a_log, ev_dt_bias, ev_gdn_norm, ev_w_out, od_attn_norm, od_w_in, od_pe_k, od_w1_k, od_w2_k, od_pe_v, od_w1_v, od_w2_v, od_w_out, ffn_norm, ffn_w_gate, ffn_w_up, ffn_w_down, final_norm):
    B, S, D = x.shape
    depth = ffn_norm.shape[0]
    h = x.reshape(B * S, D)
    tables = _rope_tables(S)
    wg_all, wu_all, wd_all = ffn_w_gate.astype(BF16), ffn_w_up.astype(BF16), ffn_w_down.astype(BF16)
    for layer in range(depth):
        i = layer // 2
        if layer % 2 == 0:
            mix, w_mix = even_mixer_layer(h, B, S, ev_attn_norm[i], ev_w_in[i], ev_q_norm[i], ev_kv_norm[i],
                                          ev_w_uq[i], ev_w_ukv[i], ev_conv_w[i], ev_a_log[i], ev_dt_bias[i],
                                          ev_gdn_norm[i], ev_w_out[i], tables)
        else:
            mix, w_mix = odd_mixer_layer(h, B, S, od_attn_norm[i], od_w_in[i], od_pe_k[i], od_w1_k[i], od_w2_k[i],
                                         od_pe_v[i], od_w1_v[i], od_w2_v[i], od_w_out[i])
        h = proj_ffn(mix, w_mix, h, ffn_norm[layer], wg_all[layer], wu_all[layer], wd_all[layer],
                     final_g=final_norm if layer == depth - 1 else None)
    return h.reshape(B, S, D)
```
